```python
import jax, jax.numpy as jnp
from jax import lax
import numpy as np

D_MODEL = 2048
BATCH = 4
SEQ = 2048
DEPTH = 1
DEC_BATCH = 128
DEC_SEQ = 8
PAST_LEN = 16384
PAGE_SIZE = 128

N_HEADS = 4
D_QK = 256
D_V = 512
D_MLSTM = N_HEADS * D_V
D_CONV = D_MODEL
CONV_W = 31
D_FF = 5632
CHUNK = 128
LN_EPS = 1e-5
ALPHA = (2.0 * DEPTH) ** 0.25
BETA = (8.0 * DEPTH) ** -0.25

_SIZES = (N_HEADS * D_QK, N_HEADS * D_QK, D_MLSTM, D_MLSTM, N_HEADS, N_HEADS,
          D_CONV, D_CONV, D_MODEL, D_MODEL)
D_IN = int(sum(_SIZES))
SPLIT_IDX = [int(s) for s in np.cumsum(_SIZES)[:-1]]

kernel_name = "hybrid_mlstm_conformer_decoder_step"


def layer_norm(x, g, b):
    xf = x.astype(jnp.float32)
    mu = jnp.mean(xf, axis=-1, keepdims=True)
    var = jnp.mean(jnp.square(xf - mu), axis=-1, keepdims=True)
    return ((xf - mu) * lax.rsqrt(var + LN_EPS) * g.astype(jnp.float32) + b.astype(jnp.float32)).astype(x.dtype)


def swiglu_ffn(x, w1, w3, w2):
    return (jax.nn.silu(x @ w1) * (x @ w3)) @ w2


def mlstm_chunk(carry, inp):
    C0, n0, m0 = carry
    q, k, v, ig, lf = inp
    L = q.shape[2]
    b = jnp.cumsum(lf, axis=-1)
    causal = jnp.tril(jnp.ones((L, L), dtype=bool))
    log_d = b[..., :, None] - b[..., None, :] + ig[..., None, :]
    log_d = jnp.where(causal, log_d, -jnp.inf)
    inter = b + m0[..., None]
    m_t = jnp.maximum(inter, jnp.max(log_d, axis=-1))
    d = jnp.exp(log_d - m_t[..., None])
    w_inter = jnp.exp(inter - m_t)
    s = jnp.einsum('bhtd,bhsd->bhts', q, k) * d
    num = jnp.einsum('bhts,bhsv->bhtv', s, v) + w_inter[..., None] * jnp.einsum('bhtd,bhdv->bhtv', q, C0)
    qn = jnp.sum(s, axis=-1) + w_inter * jnp.einsum('bhtd,bhd->bht', q, n0)
    den = jnp.maximum(jnp.abs(qn), jnp.exp(-m_t))
    h = num / den[..., None]
    m_new = m_t[..., -1]
    w_k = jnp.exp(b[..., -1:] - b + ig - m_new[..., None])
    w_c = jnp.exp(inter[..., -1] - m_new)
    C_new = w_c[..., None, None] * C0 + jnp.einsum('bhs,bhsd,bhsv->bhdv', w_k, k, v)
    n_new = w_c[..., None] * n0 + jnp.einsum('bhs,bhsd->bhd', w_k, k)
    return (C_new, n_new, m_new), h


def mlstm_branch(q, k, v, o, ig_pre, fg_pre, b_igate, b_fgate, mh_g, C0, n0, m0, chunk):
    B, T, _ = q.shape
    f32 = jnp.float32

    def heads(t, d):
        return t.reshape(B, T, N_HEADS, d).transpose(0, 2, 1, 3).astype(f32)

    qh = heads(q, D_QK)
    kh = heads(k, D_QK) * (D_QK ** -0.5)
    vh = heads(v, D_V)
    ig = (ig_pre.astype(f32) + b_igate.astype(f32)).transpose(0, 2, 1)
    lf = jax.nn.log_sigmoid(fg_pre.astype(f32) + b_fgate.astype(f32)).transpose(0, 2, 1)
    nc = T // chunk

    def to_chunks(t):
        return jnp.moveaxis(t.reshape((B, N_HEADS, nc, chunk) + t.shape[3:]), 2, 0)

    carry, h = lax.scan(mlstm_chunk, (C0.astype(f32), n0.astype(f32), m0.astype(f32)),
                        (to_chunks(qh), to_chunks(kh), to_chunks(vh), to_chunks(ig), to_chunks(lf)))
    h = jnp.moveaxis(h, 0, 2).reshape(B, N_HEADS, T, D_V)
    mu = jnp.mean(h, axis=-1, keepdims=True)
    var = jnp.mean(jnp.square(h - mu), axis=-1, keepdims=True)
    h = ((h - mu) * lax.rsqrt(var + LN_EPS)).transpose(0, 2, 1, 3).reshape(B, T, D_MLSTM) * mh_g.astype(f32)
    return jax.nn.sigmoid(o) * h.astype(o.dtype), carry


def conv_branch(a, g, buf, conv_w, conv_b, ln_g, ln_b):
    u = a * jax.nn.sigmoid(g)
    upad = jnp.concatenate([buf.astype(u.dtype), u], axis=1)
    y = lax.conv_general_dilated(upad, conv_w[:, None, :].astype(u.dtype), (1,), 'VALID',
                                 dimension_numbers=('NWC', 'WIO', 'NWC'),
                                 feature_group_count=D_CONV) + conv_b
    y = jax.nn.silu(layer_norm(y, ln_g, ln_b))
    return y, upad[:, -(CONV_W - 1):]


def decoder_layer(x, C0, n0, m0, conv_buf, p, chunk):
    (ffn1_w1, ffn1_w3, ffn1_w2, ln1_g, ln1_b, w_in, b_igate, b_fgate, mh_norm_g,
     conv_w, conv_b, conv_ln_g, conv_ln_b, w_out, ln2_g, ln2_b,
     ffn2_w1, ffn2_w3, ffn2_w2, ln3_g, ln3_b) = p
    x1 = layer_norm(ALPHA * x + 0.5 * swiglu_ffn(x, ffn1_w1, ffn1_w3, ffn1_w2), ln1_g, ln1_b)
    proj = x1 @ w_in
    q, k, v, o, ig, fg, glu_a, glu_b, gate_a, gate_b = jnp.split(proj, SPLIT_IDX, axis=-1)
    h_a, (C_new, n_new, m_new) = mlstm_branch(q, k, v, o, ig, fg, b_igate, b_fgate, mh_norm_g,
                                              C0, n0, m0, chunk)
    h_b, conv_new = conv_branch(glu_a, glu_b, conv_buf, conv_w, conv_b, conv_ln_g, conv_ln_b)
    mix = (jax.nn.sigmoid(gate_a) * h_a + jax.nn.sigmoid(gate_b) * h_b) @ w_out
    x2 = layer_norm(ALPHA * x1 + mix, ln2_g, ln2_b)
    y = layer_norm(ALPHA * x2 + 0.5 * swiglu_ffn(x2, ffn2_w1, ffn2_w3, ffn2_w2), ln3_g, ln3_b)
    return y, C_new, n_new, m_new, conv_new


def setup_inputs(seed: int = 0) -> dict:
    key = jax.random.key(seed)
    ks = iter(jax.random.split(key, 40))
    f32 = jnp.float32

    def nrm(shape, scale):
        return jax.random.normal(next(ks), shape, f32) * scale

    def gain(shape):
        return 1.0 + nrm(shape, 0.02)

    L = DEPTH
    return {
        "x_prompt": nrm((BATCH, SEQ, D_MODEL), 1.0),
        "x_sample": nrm((DEC_BATCH, DEC_SEQ, D_MODEL), 1.0),
        "state_C": nrm((L, DEC_BATCH, N_HEADS, D_QK, D_V), 0.1),
        "state_n": nrm((L, DEC_BATCH, N_HEADS, D_QK), 0.1),
        "state_m": nrm((L, DEC_BATCH, N_HEADS), 1.0),
        "cache_conv": nrm((L, DEC_BATCH, CONV_W - 1, D_CONV), 0.5),
        "ffn1_w1": nrm((L, D_MODEL, D_FF), D_MODEL ** -0.5),
        "ffn1_w3": nrm((L, D_MODEL, D_FF), D_MODEL ** -0.5),
        "ffn1_w2": nrm((L, D_FF, D_MODEL), BETA * D_FF ** -0.5),
        "ln1_g": gain((L, D_MODEL)),
        "ln1_b": nrm((L, D_MODEL), 0.02),
        "w_in": nrm((L, D_MODEL, D_IN), D_MODEL ** -0.5),
        "b_igate": nrm((L, N_HEADS), 0.1),
        "b_fgate": 3.0 + nrm((L, N_HEADS), 0.5),
        "mh_norm_g": gain((L, D_MLSTM)),
        "conv_w": nrm((L, CONV_W, D_CONV), CONV_W ** -0.5),
        "conv_b": nrm((L, D_CONV), 0.02),
        "conv_ln_g": gain((L, D_CONV)),
        "conv_ln_b": nrm((L, D_CONV), 0.02),
        "w_out": nrm((L, D_MODEL, D_MODEL), BETA * D_MODEL ** -0.5),
        "ln2_g": gain((L, D_MODEL)),
        "ln2_b": nrm((L, D_MODEL), 0.02),
        "ffn2_w1": nrm((L, D_MODEL, D_FF), D_MODEL ** -0.5),
        "ffn2_w3": nrm((L, D_MODEL, D_FF), D_MODEL ** -0.5),
        "ffn2_w2": nrm((L, D_FF, D_MODEL), BETA * D_FF ** -0.5),
        "ln3_g": gain((L, D_MODEL)),
        "ln3_b": nrm((L, D_MODEL), 0.02),
    }


def reference(x_prompt, x_sample, state_C, state_n, state_m, cache_conv,
              ffn1_w1, ffn1_w3, ffn1_w2, ln1_g, ln1_b, w_in, b_igate, b_fgate, mh_norm_g,
              conv_w, conv_b, conv_ln_g, conv_ln_b, w_out, ln2_g, ln2_b,
              ffn2_w1, ffn2_w3, ffn2_w2, ln3_g, ln3_b):
    yp, ys = x_prompt, x_sample
    Bp = x_prompt.shape[0]
    Cp_l, np_l, mp_l, cp_l = [], [], [], []
    Cs_l, ns_l, ms_l, cs_l = [], [], [], []
    for l in range(DEPTH):
        p = (ffn1_w1[l], ffn1_w3[l], ffn1_w2[l], ln1_g[l], ln1_b[l], w_in[l], b_igate[l], b_fgate[l],
             mh_norm_g[l], conv_w[l], conv_b[l], conv_ln_g[l], conv_ln_b[l], w_out[l], ln2_g[l], ln2_b[l],
             ffn2_w1[l], ffn2_w3[l], ffn2_w2[l], ln3_g[l], ln3_b[l])
        C0 = jnp.zeros((Bp, N_HEADS, D_QK, D_V), jnp.float32)
        n0 = jnp.zeros((Bp, N_HEADS, D_QK), jnp.float32)
        m0 = jnp.zeros((Bp, N_HEADS), jnp.float32)
        buf0 = jnp.zeros((Bp, CONV_W - 1, D_CONV), x_prompt.dtype)
        yp, Cp, np_, mp, cp = decoder_layer(yp, C0, n0, m0, buf0, p, CHUNK)
        ys, Cs, ns, ms, cs = decoder_layer(ys, state_C[l], state_n[l], state_m[l], cache_conv[l], p,
                                           x_sample.shape[1])
        Cp_l.append(Cp); np_l.append(np_); mp_l.append(mp); cp_l.append(cp)
        Cs_l.append(Cs); ns_l.append(ns); ms_l.append(ms); cs_l.append(cs)
    return (yp, ys,
            jnp.stack(Cp_l), jnp.stack(np_l), jnp.stack(mp_l), jnp.stack(cp_l),
            jnp.stack(Cs_l), jnp.stack(ns_l), jnp.stack(ms_l), jnp.stack(cs_l))
```

```python
import functools

import jax
import jax.numpy as jnp
from jax import lax
from jax.experimental import pallas as pl
from jax.experimental.pallas import tpu as pltpu

F32 = jnp.float32
BF16 = jnp.bfloat16
LN_EPS = 1e-5
LANES = 128
SUBLANES = 8
HALO = 32
VMEM_LIMIT = 56 * 1024 * 1024

TM_FFN = 512
TF_FFN = 512
TM_PROJ = 512
TN_PROJ = 256
TM_MIX = 512
TM_CONV = 256
CONV_ROWS = 64
BS_CONV = 8
BS_MLSTM = 2
CHUNK = 128
N_GROUPS = 7


def _params(sem):
    return pltpu.CompilerParams(dimension_semantics=sem, vmem_limit_bytes=VMEM_LIMIT)


def _sigmoid(x):
    return jax.nn.sigmoid(x)


def _layer_norm(z, g, b):
    mu = jnp.mean(z, axis=-1, keepdims=True)
    zc = z - mu
    var = jnp.mean(zc * zc, axis=-1, keepdims=True)
    return zc * lax.rsqrt(var + LN_EPS) * g + b


def _dot(a, b):
    return jnp.dot(a, b, preferred_element_type=F32)


def _ffn_ln_kernel(x_ref, w1_ref, w3_ref, w2_ref, g_ref, b_ref, o_ref, xb_ref, acc_ref, *, alpha):
    f = pl.program_id(1)

    @pl.when(f == 0)
    def _():
        xb_ref[...] = x_ref[...].astype(BF16)

    xb = xb_ref[...]
    a = _dot(xb, w1_ref[...])
    c = _dot(xb, w3_ref[...])
    h = (a * _sigmoid(a) * c).astype(BF16)
    p = _dot(h, w2_ref[...])

    @pl.when(f == 0)
    def _():
        acc_ref[...] = p

    @pl.when(f > 0)
    def _():
        acc_ref[...] += p

    @pl.when(f == pl.num_programs(1) - 1)
    def _():
        z = alpha * x_ref[...] + 0.5 * acc_ref[...]
        o_ref[...] = _layer_norm(z, g_ref[...], b_ref[...])


def _ffn_ln(x, w1, w3, w2, g, b, alpha):
    n, d = x.shape
    dff = w1.shape[1]
    tm, tf = min(TM_FFN, n), min(TF_FFN, dff)
    assert n % tm == 0 and dff % tf == 0
    return pl.pallas_call(
        functools.partial(_ffn_ln_kernel, alpha=alpha),
        out_shape=jax.ShapeDtypeStruct((n, d), F32),
        grid=(n // tm, dff // tf),
        in_specs=[
            pl.BlockSpec((tm, d), lambda i, f: (i, 0)),
            pl.BlockSpec((d, tf), lambda i, f: (0, f)),
            pl.BlockSpec((d, tf), lambda i, f: (0, f)),
            pl.BlockSpec((tf, d), lambda i, f: (f, 0)),
            pl.BlockSpec((1, d), lambda i, f: (0, 0)),
            pl.BlockSpec((1, d), lambda i, f: (0, 0)),
        ],
        out_specs=pl.BlockSpec((tm, d), lambda i, f: (i, 0)),
        scratch_shapes=[pltpu.VMEM((tm, d), BF16), pltpu.VMEM((tm, d), F32)],
        compiler_params=_params(("parallel", "arbitrary")),
        name="ffn_ln",
    )(x, w1, w3, w2, g, b)


def _proj_kernel(x_ref, w_ref, wif_ref, qk_ref, v_ref, ga_ref, gb_ref, u_ref, if_ref, xb_ref):
    j = pl.program_id(1)

    @pl.when(j == 0)
    def _():
        xb0 = x_ref[...].astype(BF16)
        xb_ref[...] = xb0
        if_ref[...] = _dot(xb0, wif_ref[...])

    xb = xb_ref[...]
    qk_ref[...] = _dot(xb, w_ref[0]).astype(BF16)
    v_ref[...] = _dot(xb, w_ref[1]).astype(BF16)
    ga_ref[...] = (_sigmoid(_dot(xb, w_ref[2])) * _sigmoid(_dot(xb, w_ref[3]))).astype(BF16)
    gb_ref[...] = _sigmoid(_dot(xb, w_ref[4])).astype(BF16)
    u_ref[...] = _dot(xb, w_ref[5]) * _sigmoid(_dot(xb, w_ref[6]))


def _proj(x1, w7, wif):
    n, d = x1.shape
    tm, tn = min(TM_PROJ, n), min(TN_PROJ, d)
    assert n % tm == 0 and d % tn == 0
    col = pl.BlockSpec((tm, tn), lambda i, j: (i, j))
    return pl.pallas_call(
        _proj_kernel,
        out_shape=(
            jax.ShapeDtypeStruct((n, d), BF16),
            jax.ShapeDtypeStruct((n, d), BF16),
            jax.ShapeDtypeStruct((n, d), BF16),
            jax.ShapeDtypeStruct((n, d), BF16),
            jax.ShapeDtypeStruct((n, d), F32),
            jax.ShapeDtypeStruct((n, LANES), F32),
        ),
        grid=(n // tm, d // tn),
        in_specs=[
            pl.BlockSpec((tm, d), lambda i, j: (i, 0)),
            pl.BlockSpec((N_GROUPS, d, tn), lambda i, j: (0, 0, j)),
            pl.BlockSpec((d, LANES), lambda i, j: (0, 0)),
        ],
        out_specs=(col, col, col, col, col, pl.BlockSpec((tm, LANES), lambda i, j: (i, 0))),
        scratch_shapes=[pltpu.VMEM((tm, d), BF16)],
        compiler_params=_params(("parallel", "arbitrary")),
        name="proj",
    )(x1, w7, wif)


def _mlstm_chunk(q, k, v, ig_c, lf_c, c0, n0, m0, last):
    L = q.shape[0]
    row = lax.broadcasted_iota(jnp.int32, (L, L), 0)
    colm = lax.broadcasted_iota(jnp.int32, (L, L), 1)
    causal = colm <= row
    eye = colm == row
    b_r = jnp.sum(jnp.where(row <= colm, lf_c, 0.0), axis=0, keepdims=True)
    b_c = jnp.sum(jnp.where(eye, b_r, 0.0), axis=1, keepdims=True)
    ig_r = jnp.sum(jnp.where(eye, ig_c, 0.0), axis=0, keepdims=True)

    log_d = jnp.where(causal, b_c - b_r + ig_r, -jnp.inf)
    inter = b_c + m0
    m_t = jnp.maximum(inter, jnp.max(log_d, axis=1, keepdims=True))
    d = jnp.exp(log_d - m_t)
    w_inter = jnp.exp(inter - m_t)
    s = lax.dot_general(q, k, (((1,), (1,)), ((), ())), preferred_element_type=F32) * d
    num = _dot(s.astype(BF16), v) + w_inter * _dot(q, c0.astype(BF16))
    qn = jnp.sum(s, axis=1, keepdims=True) + w_inter * jnp.sum(q.astype(F32) * n0, axis=1, keepdims=True)
    den = jnp.maximum(jnp.abs(qn), jnp.exp(-m_t))
    h = num * (1.0 / den)

    m_new = m_t[last:last + 1, :]
    w_k = jnp.exp(b_c[last:last + 1, :] - b_c + ig_c - m_new)
    w_c = jnp.exp(inter[last:last + 1, :] - m_new)
    kw = k.astype(F32) * w_k
    c_new = w_c * c0 + lax.dot_general(kw.astype(BF16), v, (((0,), (0,)), ((), ())),
                                       preferred_element_type=F32)
    n_new = w_c * n0 + jnp.sum(kw, axis=0, keepdims=True)
    return h, c_new, n_new, m_new


def _head_norm(h, g):
    mu = jnp.mean(h, axis=-1, keepdims=True)
    hc = h - mu
    var = jnp.mean(hc * hc, axis=-1, keepdims=True)
    return hc * lax.rsqrt(var + LN_EPS) * g


def _log_sigmoid(x):
    return jnp.minimum(x, 0.0) - jnp.log1p(jnp.exp(-jnp.abs(x)))


def _mlstm_prompt_kernel(qk_ref, v_ref, if_ref, bias_ref, g_ref, hn_ref, c_ref, n_ref, m_ref,
                         *, n_heads, d_qk, d_v):
    c = pl.program_id(1)

    @pl.when(c == 0)
    def _():
        c_ref[...] = jnp.zeros_like(c_ref)
        n_ref[...] = jnp.zeros_like(n_ref)
        m_ref[...] = jnp.zeros_like(m_ref)

    gates = if_ref[...] + bias_ref[...]
    lf_all = _log_sigmoid(gates)
    L = gates.shape[0]
    for h in range(n_heads):
        q = qk_ref[:, h * d_qk:(h + 1) * d_qk]
        k = qk_ref[:, (n_heads + h) * d_qk:(n_heads + h + 1) * d_qk]
        v = v_ref[:, h * d_v:(h + 1) * d_v]
        hh, c_new, n_new, m_new = _mlstm_chunk(
            q, k, v, gates[:, h:h + 1], lf_all[:, n_heads + h:n_heads + h + 1],
            c_ref[0, h], n_ref[0, h], m_ref[0, h], L - 1)
        c_ref[0, h] = c_new
        n_ref[0, h] = n_new
        m_ref[0, h] = m_new
        hn_ref[:, h * d_v:(h + 1) * d_v] = _head_norm(hh, g_ref[:, h * d_v:(h + 1) * d_v]).astype(BF16)


def _mlstm_prompt(qk, v, iff, bias, g, batch, seq, n_heads, d_qk, d_v):
    d = v.shape[1]
    L = min(CHUNK, seq)
    assert seq % L == 0
    nc = seq // L
    tok = lambda b, c: (b * nc + c, 0)
    return pl.pallas_call(
        functools.partial(_mlstm_prompt_kernel, n_heads=n_heads, d_qk=d_qk, d_v=d_v),
        out_shape=(
            jax.ShapeDtypeStruct((batch * seq, d), BF16),
            jax.ShapeDtypeStruct((batch, n_heads, d_qk, d_v), F32),
            jax.ShapeDtypeStruct((batch, n_heads, 1, d_qk), F32),
            jax.ShapeDtypeStruct((batch, n_heads, 1, 1), F32),
        ),
        grid=(batch, nc),
        in_specs=[
            pl.BlockSpec((L, qk.shape[1]), tok),
            pl.BlockSpec((L, d), tok),
            pl.BlockSpec((L, LANES), tok),
            pl.BlockSpec((1, LANES), lambda b, c: (0, 0)),
            pl.BlockSpec((1, d), lambda b, c: (0, 0)),
        ],
        out_specs=(
            pl.BlockSpec((L, d), tok),
            pl.BlockSpec((1, n_heads, d_qk, d_v), lambda b, c: (b, 0, 0, 0)),
            pl.BlockSpec((1, n_heads, 1, d_qk), lambda b, c: (b, 0, 0, 0)),
            pl.BlockSpec((1, n_heads, 1, 1), lambda b, c: (b, 0, 0, 0)),
        ),
        compiler_params=_params(("parallel", "arbitrary")),
        name="mlstm_prompt",
    )(qk, v, iff, bias, g)


def _mlstm_sample_kernel(qk_ref, v_ref, if_ref, bias_ref, g_ref, c0_ref, n0_ref, m0_ref,
                         hn_ref, c_ref, n_ref, m_ref, *, n_heads, d_qk, d_v, t_dec, bs):
    pad = (-t_dec) % (2 * SUBLANES)
    zpad = lambda x: jnp.concatenate([x, jnp.zeros((pad, x.shape[1]), x.dtype)], axis=0) if pad else x
    qk_all = qk_ref[...].astype(F32)
    v_all = v_ref[...].astype(F32)
    for b in range(bs):
        rows = slice(b * t_dec, (b + 1) * t_dec)
        gates = zpad(if_ref[rows, :] + bias_ref[...])
        lf_all = _log_sigmoid(gates)
        qk = zpad(qk_all[rows, :])
        vv = zpad(v_all[rows, :])
        for h in range(n_heads):
            q = qk[:, h * d_qk:(h + 1) * d_qk].astype(BF16)
            k = qk[:, (n_heads + h) * d_qk:(n_heads + h + 1) * d_qk].astype(BF16)
            v = vv[:, h * d_v:(h + 1) * d_v].astype(BF16)
            hh, c_new, n_new, m_new = _mlstm_chunk(
                q, k, v, gates[:, h:h + 1], lf_all[:, n_heads + h:n_heads + h + 1],
                c0_ref[b, h], n0_ref[b, h], m0_ref[b, h], t_dec - 1)
            c_ref[b, h] = c_new
            n_ref[b, h] = n_new
            m_ref[b, h] = m_new
            hn = _head_norm(hh[:t_dec], g_ref[:, h * d_v:(h + 1) * d_v])
            hn_ref[rows, h * d_v:(h + 1) * d_v] = hn


def _mlstm_sample(qk, v, iff, bias, g, c0, n0, m0, row0, t_dec):
    batch, n_heads, d_qk, d_v = c0.shape
    d = v.shape[1]
    bs = min(BS_MLSTM, batch)
    assert batch % bs == 0 and row0 % (bs * t_dec) == 0 and (bs * t_dec) % SUBLANES == 0
    blk0 = row0 // (bs * t_dec)
    tok = lambda i: (blk0 + i, 0)
    st4 = lambda i: (i, 0, 0, 0)
    return pl.pallas_call(
        functools.partial(_mlstm_sample_kernel, n_heads=n_heads, d_qk=d_qk, d_v=d_v, t_dec=t_dec, bs=bs),
        out_shape=(
            jax.ShapeDtypeStruct((batch * t_dec, d), F32),
            jax.ShapeDtypeStruct((batch, n_heads, d_qk, d_v), F32),
            jax.ShapeDtypeStruct((batch, n_heads, 1, d_qk), F32),
            jax.ShapeDtypeStruct((batch, n_heads, 1, 1), F32),
        ),
        grid=(batch // bs,),
        in_specs=[
            pl.BlockSpec((bs * t_dec, qk.shape[1]), tok),
            pl.BlockSpec((bs * t_dec, d), tok),
            pl.BlockSpec((bs * t_dec, LANES), tok),
            pl.BlockSpec((1, LANES), lambda i: (0, 0)),
            pl.BlockSpec((1, d), lambda i: (0, 0)),
            pl.BlockSpec((bs, n_heads, d_qk, d_v), st4),
            pl.BlockSpec((bs, n_heads, 1, d_qk), st4),
            pl.BlockSpec((bs, n_heads, 1, 1), st4),
        ],
        out_specs=(
            pl.BlockSpec((bs * t_dec, d), lambda i: (i, 0)),
            pl.BlockSpec((bs, n_heads, d_qk, d_v), st4),
            pl.BlockSpec((bs, n_heads, 1, d_qk), st4),
            pl.BlockSpec((bs, n_heads, 1, 1), st4),
        ),
        compiler_params=_params(("parallel",)),
        name="mlstm_sample",
    )(qk, v, iff, bias, g, c0, n0, m0)


def _conv_strip(ubuf_ref, w_ref, y_ref, base, out0, rows, width):
    n_col = y_ref.shape[1] // LANES

    def body(ci, carry):
        cs = pl.ds(pl.multiple_of(ci * LANES, LANES), LANES)
        acc = w_ref[0:1, cs] * ubuf_ref[pl.ds(base, rows), cs]
        for j in range(1, width):
            acc = acc + w_ref[j:j + 1, cs] * ubuf_ref[pl.ds(base + j, rows), cs]
        y_ref[pl.ds(out0, rows), cs] = acc
        return carry

    lax.fori_loop(0, n_col, body, 0)


def _conv_act(y, cb, g, b):
    z = _layer_norm(y + cb, g, b)
    return z * _sigmoid(z)


def _conv_prompt_kernel(u_ref, halo_ref, w_ref, cb_ref, g_ref, b_ref, hb_ref, ubuf_ref, y_ref,
                        *, tiles_per_seq, width):
    i = pl.program_id(0)
    tm = u_ref.shape[0]
    first = (i % tiles_per_seq) == 0

    @pl.when(first)
    def _():
        ubuf_ref[0:HALO, :] = jnp.zeros((HALO, ubuf_ref.shape[1]), F32)

    @pl.when(jnp.logical_not(first))
    def _():
        ubuf_ref[0:HALO, :] = halo_ref[...]

    ubuf_ref[HALO:HALO + tm, :] = u_ref[...]
    rows = min(CONV_ROWS, tm)
    for r0 in range(0, tm, rows):
        _conv_strip(ubuf_ref, w_ref, y_ref, HALO - (width - 1) + r0, r0, rows, width)
    hb_ref[...] = _conv_act(y_ref[...], cb_ref[...], g_ref[...], b_ref[...]).astype(BF16)


def _conv_prompt(u, w, cb, g, b, batch, seq):
    d = u.shape[1]
    width = w.shape[0]
    tm = min(TM_CONV, seq)
    assert seq % tm == 0 and tm % HALO == 0 and width - 1 <= HALO
    tps = seq // tm
    vec = pl.BlockSpec((1, d), lambda i: (0, 0))
    return pl.pallas_call(
        functools.partial(_conv_prompt_kernel, tiles_per_seq=tps, width=width),
        out_shape=jax.ShapeDtypeStruct((batch * seq, d), BF16),
        grid=(batch * tps,),
        in_specs=[
            pl.BlockSpec((tm, d), lambda i: (i, 0)),
            pl.BlockSpec((HALO, d), lambda i: (jnp.maximum(i * (tm // HALO) - 1, 0), 0)),
            pl.BlockSpec((width, d), lambda i: (0, 0)),
            vec, vec, vec,
        ],
        out_specs=pl.BlockSpec((tm, d), lambda i: (i, 0)),
        scratch_shapes=[pltpu.VMEM((HALO + tm, d), F32), pltpu.VMEM((tm, d), F32)],
        compiler_params=_params(("parallel",)),
        name="conv_prompt",
    )(u, u, w, cb, g, b)


def _conv_sample_kernel(u_ref, cache_ref, w_ref, cb_ref, g_ref, b_ref, hb_ref, ubuf_ref, y_ref,
                        *, width, t_dec, bs):
    hist = width - 1
    for b in range(bs):
        ubuf_ref[0:hist, :] = cache_ref[b]
        ubuf_ref[hist:hist + t_dec, :] = u_ref[b * t_dec:(b + 1) * t_dec, :]
        _conv_strip(ubuf_ref, w_ref, y_ref, 0, b * t_dec, t_dec, width)
    hb_ref[...] = _conv_act(y_ref[...], cb_ref[...], g_ref[...], b_ref[...]).astype(BF16)


def _conv_sample(u, cache, w, cb, g, b, row0, t_dec):
    batch, hist, d = cache.shape
    width = w.shape[0]
    bs = min(BS_CONV, batch)
    assert batch % bs == 0 and hist == width - 1 and row0 % (bs * t_dec) == 0
    blk0 = row0 // (bs * t_dec)
    vec = pl.BlockSpec((1, d), lambda i: (0, 0))
    return pl.pallas_call(
        functools.partial(_conv_sample_kernel, width=width, t_dec=t_dec, bs=bs),
        out_shape=jax.ShapeDtypeStruct((batch * t_dec, d), BF16),
        grid=(batch // bs,),
        in_specs=[
            pl.BlockSpec((bs * t_dec, d), lambda i: (blk0 + i, 0)),
            pl.BlockSpec((bs, hist, d), lambda i: (i, 0, 0)),
            pl.BlockSpec((width, d), lambda i: (0, 0)),
            vec, vec, vec,
        ],
        out_specs=pl.BlockSpec((bs * t_dec, d), lambda i: (i, 0)),
        scratch_shapes=[pltpu.VMEM((hist + t_dec + SUBLANES, d), F32), pltpu.VMEM((bs * t_dec, d), F32)],
        compiler_params=_params(("parallel",)),
        name="conv_sample",
    )(u, cache, w, cb, g, b)


def _mix_ln_kernel(ga_ref, gb_ref, hn_ref, hb_ref, x1_ref, w_ref, g_ref, b_ref, o_ref, *, alpha):
    mixin = (ga_ref[...].astype(F32) * hn_ref[...].astype(F32)
             + gb_ref[...].astype(F32) * hb_ref[...].astype(F32))
    mix = _dot(mixin.astype(BF16), w_ref[...])
    o_ref[...] = _layer_norm(alpha * x1_ref[...] + mix, g_ref[...], b_ref[...])


def _mix_ln(ga, gb, hn, hb, x1, w, g, b, alpha):
    n, d = x1.shape
    tm = min(TM_MIX, n)
    assert n % tm == 0
    tok = pl.BlockSpec((tm, d), lambda i: (i, 0))
    vec = pl.BlockSpec((1, d), lambda i: (0, 0))
    return pl.pallas_call(
        functools.partial(_mix_ln_kernel, alpha=alpha),
        out_shape=jax.ShapeDtypeStruct((n, d), F32),
        grid=(n // tm,),
        in_specs=[tok, tok, tok, tok, tok, pl.BlockSpec((d, d), lambda i: (0, 0)), vec, vec],
        out_specs=tok,
        compiler_params=_params(("parallel",)),
        name="mix_ln",
    )(ga, gb, hn, hb, x1, w, g, b)


def _layer(x, batch, seq, dec_batch, t_dec, c0, n0, m0, cache, p, alpha):
    (ffn1_w1, ffn1_w3, ffn1_w2, ln1_g, ln1_b, w_in, b_igate, b_fgate, mh_norm_g,
     conv_w, conv_b, conv_ln_g, conv_ln_b, w_out, ln2_g, ln2_b,
     ffn2_w1, ffn2_w3, ffn2_w2, ln3_g, ln3_b) = p
    d = x.shape[1]
    n_heads, d_qk, d_v = c0.shape[1:]
    n_prompt = batch * seq
    hqk, dm = n_heads * d_qk, n_heads * d_v
    assert 2 * hqk == d and dm == d and conv_w.shape[1] == d and 2 * n_heads <= LANES
    vec = lambda a: a.reshape(1, -1).astype(F32)
    bf = lambda a: a.astype(BF16)

    sizes = (hqk, hqk, dm, dm, n_heads, n_heads, d, d, d, d)
    offs = [0]
    for s in sizes:
        offs.append(offs[-1] + s)
    part = lambda a: w_in[:, offs[a]:offs[a + 1]]
    k_scale = float(d_qk) ** -0.5
    w_qk = jnp.concatenate([part(0), part(1) * k_scale], axis=1)
    w7 = bf(jnp.stack([w_qk, part(2), part(3), part(8), part(9), part(6), part(7)]))
    wif = bf(jnp.pad(jnp.concatenate([part(4), part(5)], axis=1), ((0, 0), (0, LANES - 2 * n_heads))))
    gate_bias = jnp.pad(jnp.concatenate([b_igate, b_fgate]).astype(F32), (0, LANES - 2 * n_heads)).reshape(1, LANES)

    x1 = _ffn_ln(x, bf(ffn1_w1), bf(ffn1_w3), bf(ffn1_w2), vec(ln1_g), vec(ln1_b), alpha)
    qk, v, ga, gb, u, iff = _proj(x1, w7, wif)

    mh_g = vec(mh_norm_g)
    hn_p, c_p, n_p, m_p = _mlstm_prompt(qk, v, iff, gate_bias, mh_g, batch, seq, n_heads, d_qk, d_v)
    hn_s, c_s, n_s, m_s = _mlstm_sample(qk, v, iff, gate_bias, mh_g, c0,
                                        n0.reshape(dec_batch, n_heads, 1, d_qk),
                                        m0.reshape(dec_batch, n_heads, 1, 1), n_prompt, t_dec)
    hn = jnp.concatenate([hn_p, hn_s.astype(BF16)], axis=0)

    cw, cb, cg, cbb = conv_w.astype(F32), vec(conv_b), vec(conv_ln_g), vec(conv_ln_b)
    hb_p = _conv_prompt(u, cw, cb, cg, cbb, batch, seq)
    hb_s = _conv_sample(u, cache, cw, cb, cg, cbb, n_prompt, t_dec)
    hb = jnp.concatenate([hb_p, hb_s], axis=0)

    x2 = _mix_ln(ga, gb, hn, hb, x1, bf(w_out), vec(ln2_g), vec(ln2_b), alpha)
    y = _ffn_ln(x2, bf(ffn2_w1), bf(ffn2_w3), bf(ffn2_w2), vec(ln3_g), vec(ln3_b), alpha)

    hist = conv_w.shape[0] - 1
    u_p = u[:n_prompt].reshape(batch, seq, d)
    u_s = u[n_prompt:].reshape(dec_batch, t_dec, d)
    conv_p = u_p[:, seq - hist:]
    conv_s = jnp.concatenate([cache[:, t_dec:], u_s], axis=1) if t_dec < hist else u_s[:, t_dec - hist:]
    states_p = (c_p, n_p.reshape(batch, n_heads, d_qk), m_p.reshape(batch, n_heads), conv_p)
    states_s = (c_s, n_s.reshape(dec_batch, n_heads, d_qk), m_s.reshape(dec_batch, n_heads), conv_s)
    return y, states_p, states_s


def kernel(x_prompt, x_sample, state_C, state_n, state_m, cache_conv, ffn1_w1, ffn1_w3, ffn1_w2, ln1_g, ln1_b, w_in, b_igate, b_fgate, mh_norm_g, conv_w, conv_b, conv_ln_g, conv_ln_b, w_out, ln2_g, ln2_b, ffn2_w1, ffn2_w3, ffn2_w2, ln3_g, ln3_b):
    batch, seq, d = x_prompt.shape
    dec_batch, t_dec, _ = x_sample.shape
    depth = ffn1_w1.shape[0]
    alpha = (2.0 * depth) ** 0.25
    assert seq >= conv_w.shape[1] - 1
    x = jnp.concatenate([x_prompt.reshape(batch * seq, d), x_sample.reshape(dec_batch * t_dec, d)], axis=0)
    weights = (ffn1_w1, ffn1_w3, ffn1_w2, ln1_g, ln1_b, w_in, b_igate, b_fgate, mh_norm_g,
               conv_w, conv_b, conv_ln_g, conv_ln_b, w_out, ln2_g, ln2_b,
               ffn2_w1, ffn2_w3, ffn2_w2, ln3_g, ln3_b)
    outs_p, outs_s = [], []
    for l in range(depth):
        p = tuple(w[l] for w in weights)
        x, st_p, st_s = _layer(x, batch, seq, dec_batch, t_dec, state_C[l], state_n[l], state_m[l],
                               cache_conv[l], p, alpha)
        outs_p.append(st_p)
        outs_s.append(st_s)
    n_prompt = batch * seq
    y_p = x[:n_prompt].reshape(batch, seq, d)
    y_s = x[n_prompt:].reshape(dec_batch, t_dec, d)
    stack = lambda outs, i: jnp.stack([o[i] for o in outs])
    return (y_p, y_s,
            stack(outs_p, 0), stack(outs_p, 1), stack(outs_p, 2), stack(outs_p, 3),
            stack(outs_s, 0), stack(outs_s, 1), stack(outs_s, 2), stack(outs_s, 3))
```

```python
import functools

import jax
import jax.numpy as jnp
from jax import lax
from jax.experimental import pallas as pl
from jax.experimental.pallas import tpu as pltpu

F32 = jnp.float32
BF16 = jnp.bfloat16
LN_EPS = 1e-5
LANES = 128
SUBLANES = 8
HALO = 32
VMEM_LIMIT = 56 * 1024 * 1024

TM_FFN = 512
TF_FFN = 512
TM_PROJ = 512
TN_PROJ = 256
TW_PREP = 512
TR_PREP = 256
TM_MIX = 256
CONV_ROWS = 64
BS_MIX = 16
BS_MLSTM = 2
CHUNK = 128


def _params(sem):
    return pltpu.CompilerParams(dimension_semantics=sem, vmem_limit_bytes=VMEM_LIMIT)


def _sigmoid(x):
    return jax.nn.sigmoid(x)


def _layer_norm(z, g, b):
    mu = jnp.mean(z, axis=-1, keepdims=True)
    zc = z - mu
    var = jnp.mean(zc * zc, axis=-1, keepdims=True)
    return zc * lax.rsqrt(var + LN_EPS) * g + b


def _dot(a, b):
    return jnp.dot(a, b, preferred_element_type=F32)


def _for_owner(refs, split, i, fn):
    if len(refs) == 1:
        fn(refs[0])
        return
    pl.when(i < split)(lambda: fn(refs[0]))
    pl.when(i >= split)(lambda: fn(refs[1]))


def _segment_specs(arrays_rows, tm, d):
    if len(arrays_rows) == 1:
        return [pl.BlockSpec((tm, d), lambda i, f: (i, 0))], 0
    split = arrays_rows[0] // tm
    last0 = split - 1
    return [pl.BlockSpec((tm, d), lambda i, f: (jnp.minimum(i, last0), 0)),
            pl.BlockSpec((tm, d), lambda i, f: (jnp.maximum(i - split, 0), 0))], split


def _ffn_ln_kernel(*refs, alpha, n_in, n_out, in_split, out_split):
    x_refs = refs[:n_in]
    w1_ref, w3_ref, w2_ref, g_ref, b_ref = refs[n_in:n_in + 5]
    o_refs = refs[n_in + 5:n_in + 5 + n_out]
    xb_ref, acc_ref = refs[n_in + 5 + n_out:]
    i = pl.program_id(0)
    f = pl.program_id(1)

    @pl.when(f == 0)
    def _():
        def cast(x_ref):
            xb_ref[...] = x_ref[...].astype(BF16)
        _for_owner(x_refs, in_split, i, cast)

    xb = xb_ref[...]
    a = _dot(xb, w1_ref[...])
    c = _dot(xb, w3_ref[...])
    h = (a * _sigmoid(a) * c).astype(BF16)
    p = _dot(h, w2_ref[...])

    @pl.when(f == 0)
    def _():
        acc_ref[...] = p

    @pl.when(f > 0)
    def _():
        acc_ref[...] += p

    @pl.when(f == pl.num_programs(1) - 1)
    def _():
        def finish(x_ref):
            z = alpha * x_ref[...] + 0.5 * acc_ref[...]
            acc_ref[...] = _layer_norm(z, g_ref[...], b_ref[...])
        _for_owner(x_refs, in_split, i, finish)

        def emit(o_ref):
            o_ref[...] = acc_ref[...]
        _for_owner(o_refs, out_split, i, emit)


def _ffn_ln(xs, w1, w3, w2, g, b, alpha, out_rows):
    d = xs[0].shape[1]
    n = sum(x.shape[0] for x in xs)
    dff = w1.shape[1]
    tm, tf = min(TM_FFN, n), min(TF_FFN, dff)
    assert sum(out_rows) == n and dff % tf == 0
    assert all(x.shape[0] % tm == 0 for x in xs) and all(r % tm == 0 for r in out_rows)
    in_specs, in_split = _segment_specs([x.shape[0] for x in xs], tm, d)
    out_specs, out_split = _segment_specs(list(out_rows), tm, d)
    outs = pl.pallas_call(
        functools.partial(_ffn_ln_kernel, alpha=alpha, n_in=len(xs), n_out=len(out_rows),
                          in_split=in_split, out_split=out_split),
        out_shape=tuple(jax.ShapeDtypeStruct((r, d), F32) for r in out_rows),
        grid=(n // tm, dff // tf),
        in_specs=in_specs + [
            pl.BlockSpec((d, tf), lambda i, f: (0, f)),
            pl.BlockSpec((d, tf), lambda i, f: (0, f)),
            pl.BlockSpec((tf, d), lambda i, f: (f, 0)),
            pl.BlockSpec((1, d), lambda i, f: (0, 0)),
            pl.BlockSpec((1, d), lambda i, f: (0, 0)),
        ],
        out_specs=tuple(out_specs),
        scratch_shapes=[pltpu.VMEM((tm, d), BF16), pltpu.VMEM((tm, d), F32)],
        compiler_params=_params(("arbitrary", "arbitrary")),
        name="ffn_ln",
    )(*xs, w1, w3, w2, g, b)
    return outs


def _prep_w_in_kernel(a_ref, nb_ref, if_ref, w_ref, wif_ref, *, n_aligned, shift, n_gate):
    j = pl.program_id(0)
    k_rows, tw = a_ref.shape
    tr = min(TR_PREP, k_rows)

    @pl.when(j < n_aligned)
    def _():
        w_ref[...] = a_ref[...].astype(BF16)

    @pl.when(j >= n_aligned)
    def _():
        for r0 in range(0, k_rows, tr):
            cat = jnp.concatenate([a_ref[r0:r0 + tr, :], nb_ref[r0:r0 + tr, :]], axis=1)
            w_ref[r0:r0 + tr, :] = cat[:, shift:shift + tw].astype(BF16)

    @pl.when(j == 0)
    def _():
        lane = lax.broadcasted_iota(jnp.int32, if_ref.shape, 1)
        wif_ref[...] = jnp.where(lane < n_gate, if_ref[...], 0.0).astype(BF16)


def _prep_w_in(w_in, layer, n_aligned_cols, n_gate, n_shifted_cols):
    _, k_rows, n_in = w_in.shape
    tw = TW_PREP
    assert n_aligned_cols % tw == 0 and n_shifted_cols % tw == 0 and 0 < n_gate < LANES
    assert k_rows % min(TR_PREP, k_rows) == 0 and n_aligned_cols + n_gate + n_shifted_cols == n_in
    n_aligned = n_aligned_cols // tw
    n_tiles = n_aligned + n_shifted_cols // tw
    per = tw // LANES
    return pl.pallas_call(
        functools.partial(_prep_w_in_kernel, n_aligned=n_aligned, shift=n_gate, n_gate=n_gate),
        out_shape=(jax.ShapeDtypeStruct((k_rows, n_tiles * tw), BF16),
                   jax.ShapeDtypeStruct((k_rows, LANES), BF16)),
        grid=(n_tiles,),
        in_specs=[
            pl.BlockSpec((None, k_rows, tw), lambda j: (layer, 0, j)),
            pl.BlockSpec((None, k_rows, LANES), lambda j: (layer, 0, (j + 1) * per)),
            pl.BlockSpec((None, k_rows, LANES), lambda j: (layer, 0, n_aligned * per)),
        ],
        out_specs=(pl.BlockSpec((k_rows, tw), lambda j: (0, j)),
                   pl.BlockSpec((k_rows, LANES), lambda j: (0, 0))),
        compiler_params=_params(("arbitrary",)),
        name="prep_w_in",
    )(w_in, w_in, w_in)


def _proj_kernel(x_ref, wqk_ref, wv_ref, wo_ref, wla_ref, wlb_ref, wga_ref, wgb_ref, wif_ref,
                 qk_ref, v_ref, ga_ref, gb_ref, u_ref, if_ref, xb_ref, *, k_col0, k_scale):
    j = pl.program_id(1)

    @pl.when(j == 0)
    def _():
        xb0 = x_ref[...].astype(BF16)
        xb_ref[...] = xb0
        if_ref[...] = _dot(xb0, wif_ref[...])

    xb = xb_ref[...]
    tn = qk_ref.shape[1]
    qk_ref[...] = (_dot(xb, wqk_ref[...]) * jnp.where(j * tn >= k_col0, k_scale, 1.0)).astype(BF16)
    v_ref[...] = _dot(xb, wv_ref[...]).astype(BF16)
    ga_ref[...] = (_sigmoid(_dot(xb, wo_ref[...])) * _sigmoid(_dot(xb, wga_ref[...]))).astype(BF16)
    gb_ref[...] = _sigmoid(_dot(xb, wgb_ref[...])).astype(BF16)
    u_ref[...] = _dot(xb, wla_ref[...]) * _sigmoid(_dot(xb, wlb_ref[...]))


def _proj(x1, w, wif, hqk, k_scale):
    n, d = x1.shape
    tm, tn = min(TM_PROJ, n), min(TN_PROJ, d)
    assert n % tm == 0 and d % tn == 0 and hqk % tn == 0 and 2 * hqk == d
    per = d // tn
    col = pl.BlockSpec((tm, tn), lambda i, j: (i, j))
    wspec = lambda g: pl.BlockSpec((d, tn), lambda i, j: (0, g * per + j))
    return pl.pallas_call(
        functools.partial(_proj_kernel, k_col0=hqk, k_scale=k_scale),
        out_shape=(
            jax.ShapeDtypeStruct((n, d), BF16),
            jax.ShapeDtypeStruct((n, d), BF16),
            jax.ShapeDtypeStruct((n, d), BF16),
            jax.ShapeDtypeStruct((n, d), BF16),
            jax.ShapeDtypeStruct((n, d), F32),
            jax.ShapeDtypeStruct((n, LANES), F32),
        ),
        grid=(n // tm, d // tn),
        in_specs=[pl.BlockSpec((tm, d), lambda i, j: (i, 0)),
                  wspec(0), wspec(1), wspec(2), wspec(3), wspec(4), wspec(5), wspec(6),
                  pl.BlockSpec((d, LANES), lambda i, j: (0, 0))],
        out_specs=(col, col, col, col, col, pl.BlockSpec((tm, LANES), lambda i, j: (i, 0))),
        scratch_shapes=[pltpu.VMEM((tm, d), BF16)],
        compiler_params=_params(("arbitrary", "arbitrary")),
        name="proj",
    )(x1, w, w, w, w, w, w, w, wif)


def _mlstm_chunk(q, k, v, ig_c, lf_c, c0, n0, m0, last):
    L = q.shape[0]
    row = lax.broadcasted_iota(jnp.int32, (L, L), 0)
    colm = lax.broadcasted_iota(jnp.int32, (L, L), 1)
    causal = colm <= row
    eye = colm == row
    b_r = jnp.sum(jnp.where(row <= colm, lf_c, 0.0), axis=0, keepdims=True)
    b_c = jnp.sum(jnp.where(eye, b_r, 0.0), axis=1, keepdims=True)
    ig_r = jnp.sum(jnp.where(eye, ig_c, 0.0), axis=0, keepdims=True)

    log_d = jnp.where(causal, b_c - b_r + ig_r, -jnp.inf)
    inter = b_c + m0
    m_t = jnp.maximum(inter, jnp.max(log_d, axis=1, keepdims=True))
    d = jnp.exp(log_d - m_t)
    w_inter = jnp.exp(inter - m_t)
    s = lax.dot_general(q, k, (((1,), (1,)), ((), ())), preferred_element_type=F32) * d
    num = _dot(s.astype(BF16), v) + w_inter * _dot(q, c0.astype(BF16))
    qn = jnp.sum(s, axis=1, keepdims=True) + w_inter * jnp.sum(q.astype(F32) * n0, axis=1, keepdims=True)
    den = jnp.maximum(jnp.abs(qn), jnp.exp(-m_t))
    h = num * (1.0 / den)

    m_new = m_t[last:last + 1, :]
    w_k = jnp.exp(b_c[last:last + 1, :] - b_c + ig_c - m_new)
    w_c = jnp.exp(inter[last:last + 1, :] - m_new)
    kw = k.astype(F32) * w_k
    c_new = w_c * c0 + lax.dot_general(kw.astype(BF16), v, (((0,), (0,)), ((), ())),
                                       preferred_element_type=F32)
    n_new = w_c * n0 + jnp.sum(kw, axis=0, keepdims=True)
    return h, c_new, n_new, m_new


def _head_norm(h, g):
    mu = jnp.mean(h, axis=-1, keepdims=True)
    hc = h - mu
    var = jnp.mean(hc * hc, axis=-1, keepdims=True)
    return hc * lax.rsqrt(var + LN_EPS) * g


def _log_sigmoid(x):
    return jnp.minimum(x, 0.0) - jnp.log1p(jnp.exp(-jnp.abs(x)))


def _mlstm_prompt_kernel(qk_ref, v_ref, if_ref, bias_ref, g_ref, hn_ref, c_ref, n_ref, m_ref,
                         *, n_heads, d_qk, d_v):
    c = pl.program_id(1)

    @pl.when(c == 0)
    def _():
        c_ref[...] = jnp.zeros_like(c_ref)
        n_ref[...] = jnp.zeros_like(n_ref)
        m_ref[...] = jnp.zeros_like(m_ref)

    gates = if_ref[...] + bias_ref[...]
    lf_all = _log_sigmoid(gates)
    L = gates.shape[0]
    for h in range(n_heads):
        q = qk_ref[:, h * d_qk:(h + 1) * d_qk]
        k = qk_ref[:, (n_heads + h) * d_qk:(n_heads + h + 1) * d_qk]
        v = v_ref[:, h * d_v:(h + 1) * d_v]
        hh, c_new, n_new, m_new = _mlstm_chunk(
            q, k, v, gates[:, h:h + 1], lf_all[:, n_heads + h:n_heads + h + 1],
            c_ref[0, h], n_ref[0, h], m_ref[0, h], L - 1)
        c_ref[0, h] = c_new
        n_ref[0, h] = n_new
        m_ref[0, h] = m_new
        hn_ref[:, h * d_v:(h + 1) * d_v] = _head_norm(hh, g_ref[:, h * d_v:(h + 1) * d_v]).astype(BF16)


def _mlstm_prompt(qk, v, iff, bias, g, batch, seq, n_heads, d_qk, d_v):
    d = v.shape[1]
    L = min(CHUNK, seq)
    assert seq % L == 0
    nc = seq // L
    tok = lambda b, c: (b * nc + c, 0)
    return pl.pallas_call(
        functools.partial(_mlstm_prompt_kernel, n_heads=n_heads, d_qk=d_qk, d_v=d_v),
        out_shape=(
            jax.ShapeDtypeStruct((batch * seq, d), BF16),
            jax.ShapeDtypeStruct((batch, n_heads, d_qk, d_v), F32),
            jax.ShapeDtypeStruct((batch, n_heads, 1, d_qk), F32),
            jax.ShapeDtypeStruct((batch, n_heads, 1, 1), F32),
        ),
        grid=(batch, nc),
        in_specs=[
            pl.BlockSpec((L, qk.shape[1]), tok),
            pl.BlockSpec((L, d), tok),
            pl.BlockSpec((L, LANES), tok),
            pl.BlockSpec((1, LANES), lambda b, c: (0, 0)),
            pl.BlockSpec((1, d), lambda b, c: (0, 0)),
        ],
        out_specs=(
            pl.BlockSpec((L, d), tok),
            pl.BlockSpec((1, n_heads, d_qk, d_v), lambda b, c: (b, 0, 0, 0)),
            pl.BlockSpec((1, n_heads, 1, d_qk), lambda b, c: (b, 0, 0, 0)),
            pl.BlockSpec((1, n_heads, 1, 1), lambda b, c: (b, 0, 0, 0)),
        ),
        compiler_params=_params(("parallel", "arbitrary")),
        name="mlstm_prompt",
    )(qk, v, iff, bias, g)


def _mlstm_sample_kernel(qk_ref, v_ref, if_ref, bias_ref, g_ref, c0_ref, n0_ref, m0_ref,
                         hn_ref, c_ref, n_ref, m_ref, *, n_heads, d_qk, d_v, t_dec, bs):
    pad = (-t_dec) % (2 * SUBLANES)
    zpad = lambda x: jnp.concatenate([x, jnp.zeros((pad, x.shape[1]), x.dtype)], axis=0) if pad else x
    qk_all = qk_ref[...].astype(F32)
    v_all = v_ref[...].astype(F32)
    for b in range(bs):
        rows = slice(b * t_dec, (b + 1) * t_dec)
        gates = zpad(if_ref[rows, :] + bias_ref[...])
        lf_all = _log_sigmoid(gates)
        qk = zpad(qk_all[rows, :])
        vv = zpad(v_all[rows, :])
        for h in range(n_heads):
            q = qk[:, h * d_qk:(h + 1) * d_qk].astype(BF16)
            k = qk[:, (n_heads + h) * d_qk:(n_heads + h + 1) * d_qk].astype(BF16)
            v = vv[:, h * d_v:(h + 1) * d_v].astype(BF16)
            hh, c_new, n_new, m_new = _mlstm_chunk(
                q, k, v, gates[:, h:h + 1], lf_all[:, n_heads + h:n_heads + h + 1],
                c0_ref[b, h], n0_ref[b, h], m0_ref[b, h], t_dec - 1)
            c_ref[b, h] = c_new
            n_ref[b, h] = n_new
            m_ref[b, h] = m_new
            hn = _head_norm(hh[:t_dec], g_ref[:, h * d_v:(h + 1) * d_v])
            hn_ref[rows, h * d_v:(h + 1) * d_v] = hn


def _mlstm_sample(qk, v, iff, bias, g, c0, n0, m0, row0, t_dec):
    batch, n_heads, d_qk, d_v = c0.shape
    d = v.shape[1]
    bs = min(BS_MLSTM, batch)
    assert batch % bs == 0 and row0 % (bs * t_dec) == 0 and (bs * t_dec) % SUBLANES == 0
    blk0 = row0 // (bs * t_dec)
    tok = lambda i: (blk0 + i, 0)
    st4 = lambda i: (i, 0, 0, 0)
    return pl.pallas_call(
        functools.partial(_mlstm_sample_kernel, n_heads=n_heads, d_qk=d_qk, d_v=d_v, t_dec=t_dec, bs=bs),
        out_shape=(
            jax.ShapeDtypeStruct((batch * t_dec, d), F32),
            jax.ShapeDtypeStruct((batch, n_heads, d_qk, d_v), F32),
            jax.ShapeDtypeStruct((batch, n_heads, 1, d_qk), F32),
            jax.ShapeDtypeStruct((batch, n_heads, 1, 1), F32),
        ),
        grid=(batch // bs,),
        in_specs=[
            pl.BlockSpec((bs * t_dec, qk.shape[1]), tok),
            pl.BlockSpec((bs * t_dec, d), tok),
            pl.BlockSpec((bs * t_dec, LANES), tok),
            pl.BlockSpec((1, LANES), lambda i: (0, 0)),
            pl.BlockSpec((1, d), lambda i: (0, 0)),
            pl.BlockSpec((bs, n_heads, d_qk, d_v), st4),
            pl.BlockSpec((bs, n_heads, 1, d_qk), st4),
            pl.BlockSpec((bs, n_heads, 1, 1), st4),
        ],
        out_specs=(
            pl.BlockSpec((bs * t_dec, d), lambda i: (i, 0)),
            pl.BlockSpec((bs, n_heads, d_qk, d_v), st4),
            pl.BlockSpec((bs, n_heads, 1, d_qk), st4),
            pl.BlockSpec((bs, n_heads, 1, 1), st4),
        ),
        compiler_params=_params(("parallel",)),
        name="mlstm_sample",
    )(qk, v, iff, bias, g, c0, n0, m0)


def _conv_tile(ubuf_ref, w_ref, y_ref, u0, out0, rows, width, cs):
    y = None
    for r in range(min(SUBLANES, width)):
        q = None
        for a in range((width - 1 - r) // SUBLANES + 1):
            k = width - 1 - (SUBLANES * a + r)
            term = w_ref[k:k + 1, cs] * ubuf_ref[pl.ds(u0 - SUBLANES * (a + 1), rows + SUBLANES), cs]
            q = term if q is None else q + term
        part = q[SUBLANES - r:SUBLANES - r + rows]
        y = part if y is None else y + part
    y_ref[pl.ds(out0, rows), cs] = y


def _conv_cols(ubuf_ref, w_ref, y_ref, u0, out0, n_rows, width):
    strip = min(CONV_ROWS, n_rows)

    def body(ci, carry):
        cs = pl.ds(pl.multiple_of(ci * LANES, LANES), LANES)
        for r0 in range(0, n_rows, strip):
            _conv_tile(ubuf_ref, w_ref, y_ref, u0 + r0, out0 + r0, strip, width, cs)
        return carry

    lax.fori_loop(0, y_ref.shape[1] // LANES, body, 0)


def _mix_tail(y_ref, cb_ref, cg_ref, cbb_ref, ga_ref, gb_ref, hn_ref, x1_ref, w_ref, g_ref, b_ref, o_ref, alpha):
    z = _layer_norm(y_ref[...] + cb_ref[...], cg_ref[...], cbb_ref[...])
    hb = z * _sigmoid(z)
    mixin = ga_ref[...].astype(F32) * hn_ref[...].astype(F32) + gb_ref[...].astype(F32) * hb
    mix = _dot(mixin.astype(BF16), w_ref[...])
    o_ref[...] = _layer_norm(alpha * x1_ref[...] + mix, g_ref[...], b_ref[...])


def _mix_prompt_kernel(u_ref, halo_ref, ga_ref, gb_ref, hn_ref, x1_ref, cw_ref, cb_ref, cg_ref, cbb_ref,
                       w_ref, g_ref, b_ref, o_ref, cs_ref, ubuf_ref, y_ref, *, alpha, tiles_per_seq, width):
    i = pl.program_id(0)
    tm = u_ref.shape[0]
    first = (i % tiles_per_seq) == 0

    @pl.when(first)
    def _():
        ubuf_ref[0:HALO, :] = jnp.zeros((HALO, ubuf_ref.shape[1]), F32)

    @pl.when(jnp.logical_not(first))
    def _():
        ubuf_ref[0:HALO, :] = halo_ref[...]

    ubuf_ref[HALO:HALO + tm, :] = u_ref[...]
    _conv_cols(ubuf_ref, cw_ref, y_ref, HALO, 0, tm, width)
    _mix_tail(y_ref, cb_ref, cg_ref, cbb_ref, ga_ref, gb_ref, hn_ref, x1_ref, w_ref, g_ref, b_ref, o_ref, alpha)

    @pl.when((i % tiles_per_seq) == tiles_per_seq - 1)
    def _():
        cs_ref[...] = u_ref[tm - (width - 1):tm, :]


def _conv_halo_ok(width):
    return SUBLANES * ((width - 1) // SUBLANES + 1) <= HALO


def _mix_prompt(u, ga, gb, hn, x1, cw, cb, cg, cbb, w, g, b, alpha, batch, seq):
    n, d = u.shape
    width = cw.shape[0]
    hist = width - 1
    tm = min(TM_MIX, seq)
    assert seq % tm == 0 and tm % HALO == 0 and _conv_halo_ok(width) and tm >= hist
    tps = seq // tm
    tok = pl.BlockSpec((tm, d), lambda i: (i, 0))
    vec = pl.BlockSpec((1, d), lambda i: (0, 0))
    return pl.pallas_call(
        functools.partial(_mix_prompt_kernel, alpha=alpha, tiles_per_seq=tps, width=width),
        out_shape=(jax.ShapeDtypeStruct((n, d), F32),
                   jax.ShapeDtypeStruct((batch, hist, d), F32)),
        grid=(batch * tps,),
        in_specs=[
            tok,
            pl.BlockSpec((HALO, d), lambda i: (jnp.maximum(i * (tm // HALO) - 1, 0), 0)),
            tok, tok, tok, tok,
            pl.BlockSpec((width, d), lambda i: (0, 0)),
            vec, vec, vec,
            pl.BlockSpec((d, d), lambda i: (0, 0)),
            vec, vec,
        ],
        out_specs=(tok, pl.BlockSpec((None, hist, d), lambda i: (i // tps, 0, 0))),
        scratch_shapes=[pltpu.VMEM((HALO + tm, d), F32), pltpu.VMEM((tm, d), F32)],
        compiler_params=_params(("arbitrary",)),
        name="mix_prompt",
    )(u, u, ga, gb, hn, x1, cw, cb, cg, cbb, w, g, b)


def _mix_sample_kernel(u_ref, cache_ref, ga_ref, gb_ref, hn_ref, x1_ref, cw_ref, cb_ref, cg_ref, cbb_ref,
                       w_ref, g_ref, b_ref, x2_any_ref, o_ref, cs_ref, ubuf_ref, y_ref, *, alpha, width, t_dec, bs):
    del x2_any_ref
    hist = width - 1
    ubuf_ref[0:HALO, :] = jnp.zeros((HALO, ubuf_ref.shape[1]), F32)
    for bi in range(bs):
        ubuf_ref[HALO - hist:HALO, :] = cache_ref[bi]
        ubuf_ref[HALO:HALO + t_dec, :] = u_ref[bi * t_dec:(bi + 1) * t_dec, :]
        _conv_cols(ubuf_ref, cw_ref, y_ref, HALO, bi * t_dec, t_dec, width)
        cs_ref[bi] = ubuf_ref[HALO + t_dec - hist:HALO + t_dec, :]
    _mix_tail(y_ref, cb_ref, cg_ref, cbb_ref, ga_ref, gb_ref, hn_ref, x1_ref, w_ref, g_ref, b_ref, o_ref, alpha)


def _mix_sample(u, cache, ga, gb, hn, x1, cw, cb, cg, cbb, w, g, b, alpha, row0, t_dec, x2):
    batch, hist, d = cache.shape
    width = cw.shape[0]
    bs = min(BS_MIX, batch)
    rows = bs * t_dec
    assert batch % bs == 0 and hist == width - 1 and row0 % rows == 0 and rows % SUBLANES == 0
    assert _conv_halo_ok(width) and hist <= HALO and t_dec % SUBLANES == 0
    blk0 = row0 // rows
    tok = pl.BlockSpec((rows, d), lambda i: (blk0 + i, 0))
    loc = pl.BlockSpec((rows, d), lambda i: (i, 0))
    vec = pl.BlockSpec((1, d), lambda i: (0, 0))
    st = pl.BlockSpec((bs, hist, d), lambda i: (i, 0, 0))
    return pl.pallas_call(
        functools.partial(_mix_sample_kernel, alpha=alpha, width=width, t_dec=t_dec, bs=bs),
        out_shape=(jax.ShapeDtypeStruct(x2.shape, F32),
                   jax.ShapeDtypeStruct((batch, hist, d), F32)),
        grid=(batch // bs,),
        in_specs=[
            tok, st, tok, tok, loc, tok,
            pl.BlockSpec((width, d), lambda i: (0, 0)),
            vec, vec, vec,
            pl.BlockSpec((d, d), lambda i: (0, 0)),
            vec, vec,
            pl.BlockSpec(memory_space=pl.ANY),
        ],
        out_specs=(tok, st),
        input_output_aliases={13: 0},
        scratch_shapes=[pltpu.VMEM((HALO + t_dec, d), F32), pltpu.VMEM((rows, d), F32)],
        compiler_params=_params(("arbitrary",)),
        name="mix_sample",
    )(u, cache, ga, gb, hn, x1, cw, cb, cg, cbb, w, g, b, x2)


def _layer(xs, layer, batch, seq, dec_batch, t_dec, c0, n0, m0, cache, w_in, p, alpha):
    (ffn1_w1, ffn1_w3, ffn1_w2, ln1_g, ln1_b, b_igate, b_fgate, mh_norm_g,
     conv_w, conv_b, conv_ln_g, conv_ln_b, w_out, ln2_g, ln2_b,
     ffn2_w1, ffn2_w3, ffn2_w2, ln3_g, ln3_b) = p
    d = xs[0].shape[1]
    n_heads, d_qk, d_v = c0.shape[1:]
    n_prompt, n_sample = batch * seq, dec_batch * t_dec
    hqk, dm = n_heads * d_qk, n_heads * d_v
    assert 2 * hqk == d and dm == d and conv_w.shape[1] == d
    vec = lambda a: a.reshape(1, -1).astype(F32)
    bf = lambda a: a.astype(BF16)

    (x1,) = _ffn_ln(xs, bf(ffn1_w1), bf(ffn1_w3), bf(ffn1_w2), vec(ln1_g), vec(ln1_b), alpha,
                    (n_prompt + n_sample,))
    w, wif = _prep_w_in(w_in, layer, 2 * hqk + 2 * dm, 2 * n_heads, 4 * d)
    qk, v, ga, gb, u, iff = _proj(x1, w, wif, hqk, float(d_qk) ** -0.5)

    gate_bias = jnp.pad(jnp.concatenate([b_igate, b_fgate]).astype(F32), (0, LANES - 2 * n_heads)).reshape(1, LANES)
    mh_g = vec(mh_norm_g)
    hn_p, c_p, n_p, m_p = _mlstm_prompt(qk, v, iff, gate_bias, mh_g, batch, seq, n_heads, d_qk, d_v)
    hn_s, c_s, n_s, m_s = _mlstm_sample(qk, v, iff, gate_bias, mh_g, c0,
                                        n0.reshape(dec_batch, n_heads, 1, d_qk),
                                        m0.reshape(dec_batch, n_heads, 1, 1), n_prompt, t_dec)

    mix_args = (conv_w.astype(F32), vec(conv_b), vec(conv_ln_g), vec(conv_ln_b), bf(w_out), vec(ln2_g), vec(ln2_b), alpha)
    x2, conv_p = _mix_prompt(u, ga, gb, hn_p, x1, *mix_args, batch, seq)
    x2, conv_s = _mix_sample(u, cache, ga, gb, hn_s, x1, *mix_args, n_prompt, t_dec, x2)

    y_p, y_s = _ffn_ln((x2,), bf(ffn2_w1), bf(ffn2_w3), bf(ffn2_w2), vec(ln3_g), vec(ln3_b), alpha,
                       (n_prompt, n_sample))
    states_p = (c_p, n_p.reshape(batch, n_heads, d_qk), m_p.reshape(batch, n_heads), conv_p)
    states_s = (c_s, n_s.reshape(dec_batch, n_heads, d_qk), m_s.reshape(dec_batch, n_heads), conv_s)
    return (y_p, y_s), states_p, states_s


def kernel(x_prompt, x_sample, state_C, state_n, state_m, cache_conv, ffn1_w1, ffn1_w3, ffn1_w2, ln1_g, ln1_b, w_in, b_igate, b_fgate, mh_norm_g, conv_w, conv_b, conv_ln_g, conv_ln_b, w_out, ln2_g, ln2_b, ffn2_w1, ffn2_w3, ffn2_w2, ln3_g, ln3_b):
    batch, seq, d = x_prompt.shape
    dec_batch, t_dec, _ = x_sample.shape
    depth = ffn1_w1.shape[0]
    alpha = (2.0 * depth) ** 0.25
    xs = (x_prompt.reshape(batch * seq, d), x_sample.reshape(dec_batch * t_dec, d))
    weights = (ffn1_w1, ffn1_w3, ffn1_w2, ln1_g, ln1_b, b_igate, b_fgate, mh_norm_g,
               conv_w, conv_b, conv_ln_g, conv_ln_b, w_out, ln2_g, ln2_b,
               ffn2_w1, ffn2_w3, ffn2_w2, ln3_g, ln3_b)
    outs_p, outs_s = [], []
    for l in range(depth):
        p = tuple(wt[l] for wt in weights)
        xs, st_p, st_s = _layer(xs, l, batch, seq, dec_batch, t_dec, state_C[l], state_n[l], state_m[l],
                                cache_conv[l], w_in, p, alpha)
        outs_p.append(st_p)
        outs_s.append(st_s)
    y_p = xs[0].reshape(batch, seq, d)
    y_s = xs[1].reshape(dec_batch, t_dec, d)
    stack = lambda outs, i: jnp.stack([o[i] for o in outs])
    return (y_p, y_s,
            stack(outs_p, 0), stack(outs_p, 1), stack(outs_p, 2), stack(outs_p, 3),
            stack(outs_s, 0), stack(outs_s, 1), stack(outs_s, 2), stack(outs_s, 3))
```

```python
import functools

import jax
import jax.numpy as jnp
from jax import lax
from jax.experimental import pallas as pl
from jax.experimental.pallas import tpu as pltpu

F32 = jnp.float32
BF16 = jnp.bfloat16
LN_EPS = 1e-5
LANES = 128
SUBLANES = 8
HALO = 32
VMEM_LIMIT = 56 * 1024 * 1024

TM_FFN = 512
TF_FFN = 512
TM_PROJ = 512
TN_PROJ = 256
TW_PREP = 512
TR_PREP = 512
TM_MIX = 256
CONV_ROWS = 64
BS_MLSTM = 2
CHUNK = 128


def _params(sem):
    return pltpu.CompilerParams(dimension_semantics=sem, vmem_limit_bytes=VMEM_LIMIT)


def _sigmoid(x):
    return jax.nn.sigmoid(x)


def _layer_norm(z, g, b):
    mu = jnp.mean(z, axis=-1, keepdims=True)
    zc = z - mu
    var = jnp.mean(zc * zc, axis=-1, keepdims=True)
    return zc * lax.rsqrt(var + LN_EPS) * g + b


def _dot(a, b):
    return jnp.dot(a, b, preferred_element_type=F32)


def _for_owner(refs, split, i, fn):
    if len(refs) == 1:
        fn(refs[0])
        return
    pl.when(i < split)(lambda: fn(refs[0]))
    pl.when(i >= split)(lambda: fn(refs[1]))


def _segment_specs(arrays_rows, tm, d):
    if len(arrays_rows) == 1:
        return [pl.BlockSpec((tm, d), lambda i, *_: (i, 0))], 0
    split = arrays_rows[0] // tm
    last0 = split - 1
    return [pl.BlockSpec((tm, d), lambda i, *_: (jnp.minimum(i, last0), 0)),
            pl.BlockSpec((tm, d), lambda i, *_: (jnp.maximum(i - split, 0), 0))], split


def _ffn_ln_kernel(*refs, alpha, n_in, n_out, in_split, out_split):
    x_refs = refs[:n_in]
    w1_ref, w3_ref, w2_ref, g_ref, b_ref = refs[n_in:n_in + 5]
    o_refs = refs[n_in + 5:n_in + 5 + n_out]
    xb_ref, acc_ref = refs[n_in + 5 + n_out:]
    i = pl.program_id(0)
    f = pl.program_id(1)

    @pl.when(f == 0)
    def _():
        def cast(x_ref):
            xb_ref[...] = x_ref[...].astype(BF16)
        _for_owner(x_refs, in_split, i, cast)
        acc_ref[...] = jnp.zeros_like(acc_ref)

    xb = xb_ref[...]
    a = _dot(xb, w1_ref[...])
    c = _dot(xb, w3_ref[...])
    h = (a * _sigmoid(a) * c).astype(BF16)
    acc_ref[...] += _dot(h, w2_ref[...])

    @pl.when(f == pl.num_programs(1) - 1)
    def _():
        def finish(x_ref):
            z = alpha * x_ref[...] + 0.5 * acc_ref[...]
            acc_ref[...] = _layer_norm(z, g_ref[...], b_ref[...])
        _for_owner(x_refs, in_split, i, finish)

        def emit(o_ref):
            o_ref[...] = acc_ref[...]
        _for_owner(o_refs, out_split, i, emit)


def _ffn_ln(xs, w1, w3, w2, g, b, alpha, out_rows):
    d = xs[0].shape[1]
    n = sum(x.shape[0] for x in xs)
    dff = w1.shape[1]
    tm, tf = min(TM_FFN, n), min(TF_FFN, dff)
    assert sum(out_rows) == n and dff % tf == 0
    assert all(x.shape[0] % tm == 0 for x in xs) and all(r % tm == 0 for r in out_rows)
    in_specs, in_split = _segment_specs([x.shape[0] for x in xs], tm, d)
    out_specs, out_split = _segment_specs(list(out_rows), tm, d)
    outs = pl.pallas_call(
        functools.partial(_ffn_ln_kernel, alpha=alpha, n_in=len(xs), n_out=len(out_rows),
                          in_split=in_split, out_split=out_split),
        out_shape=tuple(jax.ShapeDtypeStruct((r, d), F32) for r in out_rows),
        grid=(n // tm, dff // tf),
        in_specs=in_specs + [
            pl.BlockSpec((d, tf), lambda i, f: (0, f)),
            pl.BlockSpec((d, tf), lambda i, f: (0, f)),
            pl.BlockSpec((tf, d), lambda i, f: (f, 0)),
            pl.BlockSpec((1, d), lambda i, f: (0, 0)),
            pl.BlockSpec((1, d), lambda i, f: (0, 0)),
        ],
        out_specs=tuple(out_specs),
        scratch_shapes=[pltpu.VMEM((tm, d), BF16), pltpu.VMEM((tm, d), F32)],
        compiler_params=_params(("arbitrary", "arbitrary")),
        name="ffn_ln",
    )(*xs, w1, w3, w2, g, b)
    return outs


def _prep_w_in_kernel(a_ref, nb_ref, if_ref, w_ref, wif_ref, *, n_aligned, n_gate):
    j = pl.program_id(0)
    tw, k_rows = a_ref.shape
    tr = min(TR_PREP, k_rows)

    def emit(rows_of):
        for c0 in range(0, k_rows, tr):
            w_ref[c0:c0 + tr, :] = rows_of(c0).T.astype(BF16)

    @pl.when(j < n_aligned)
    def _():
        emit(lambda c0: a_ref[:, c0:c0 + tr])

    @pl.when(j >= n_aligned)
    def _():
        emit(lambda c0: jnp.concatenate([a_ref[n_gate:tw, c0:c0 + tr], nb_ref[:, c0:c0 + tr]], axis=0))

    @pl.when(j == 0)
    def _():
        for c0 in range(0, k_rows, tr):
            rows = jnp.concatenate([if_ref[:, c0:c0 + tr], jnp.zeros((LANES - n_gate, tr), F32)], axis=0)
            wif_ref[c0:c0 + tr, :] = rows.T.astype(BF16)


def _prep_w_in(w_in_t, layer, n_aligned_cols, n_gate, n_shifted_cols):
    _, n_in, k_rows = w_in_t.shape
    tw = TW_PREP
    assert n_aligned_cols % tw == 0 and n_shifted_cols % tw == 0 and n_gate == SUBLANES
    assert k_rows % min(TR_PREP, k_rows) == 0 and n_aligned_cols + n_gate + n_shifted_cols == n_in
    n_aligned = n_aligned_cols // tw
    n_tiles = n_aligned + n_shifted_cols // tw
    per = tw // n_gate
    return pl.pallas_call(
        functools.partial(_prep_w_in_kernel, n_aligned=n_aligned, n_gate=n_gate),
        out_shape=(jax.ShapeDtypeStruct((k_rows, n_tiles * tw), BF16),
                   jax.ShapeDtypeStruct((k_rows, LANES), BF16)),
        grid=(n_tiles,),
        in_specs=[
            pl.BlockSpec((None, tw, k_rows), lambda j: (layer, j, 0)),
            pl.BlockSpec((None, n_gate, k_rows), lambda j: (layer, (j + 1) * per, 0)),
            pl.BlockSpec((None, n_gate, k_rows), lambda j: (layer, n_aligned * per, 0)),
        ],
        out_specs=(pl.BlockSpec((k_rows, tw), lambda j: (0, j)),
                   pl.BlockSpec((k_rows, LANES), lambda j: (0, 0))),
        compiler_params=_params(("arbitrary",)),
        name="prep_w_in",
    )(w_in_t, w_in_t, w_in_t)


def _conv_tile(ubuf_ref, w_ref, y_ref, u0, out0, rows, width, cs):
    y = None
    for r in range(min(SUBLANES, width)):
        q = None
        for a in range((width - 1 - r) // SUBLANES + 1):
            k = width - 1 - (SUBLANES * a + r)
            term = w_ref[k:k + 1, cs] * ubuf_ref[pl.ds(u0 - SUBLANES * (a + 1), rows + SUBLANES), cs]
            q = term if q is None else q + term
        part = q[SUBLANES - r:SUBLANES - r + rows]
        y = part if y is None else y + part
    y_ref[pl.ds(out0, rows), cs] = y


def _proj_kernel(x_ref, wqk_ref, wv_ref, wo_ref, wla_ref, wlb_ref, wga_ref, wgb_ref, wif_ref, cw_ref, cache_ref,
                 qk_ref, v_ref, ga_ref, gb_ref, yc_ref, if_ref, csp_ref, css_ref,
                 xb_ref, ubuf_ref, carry_ref, tbuf_ref, ybuf_ref, *, k_col0, k_scale, n_prompt_tiles, tiles_per_seq, width, t_dec):
    i = pl.program_id(0)
    j = pl.program_id(1)
    tm, tn = yc_ref.shape
    hist = width - 1

    @pl.when(j == 0)
    def _():
        xb0 = x_ref[...].astype(BF16)
        xb_ref[...] = xb0
        if_ref[...] = _dot(xb0, wif_ref[...])

    def glu(xb):
        return _dot(xb, wla_ref[...]) * _sigmoid(_dot(xb, wlb_ref[...]))

    def dense(xb):
        qk_ref[...] = (_dot(xb, wqk_ref[...]) * jnp.where(j * tn >= k_col0, k_scale, 1.0)).astype(BF16)
        v_ref[...] = _dot(xb, wv_ref[...]).astype(BF16)
        ga_ref[...] = (_sigmoid(_dot(xb, wo_ref[...])) * _sigmoid(_dot(xb, wga_ref[...]))).astype(BF16)
        gb_ref[...] = _sigmoid(_dot(xb, wgb_ref[...])).astype(BF16)

    @pl.when(i < n_prompt_tiles)
    def _():
        xb = xb_ref[...]
        u = glu(xb)
        first = (i % tiles_per_seq) == 0
        ubuf_ref[0:HALO, :] = jnp.where(first, 0.0, carry_ref[j])
        ubuf_ref[HALO:HALO + tm, :] = u
        carry_ref[j] = u[tm - HALO:tm]
        csp_ref[...] = u[tm - hist:tm]
        strip = min(CONV_ROWS, tm)
        for c0 in range(0, tn, LANES):
            for r0 in range(0, tm, strip):
                _conv_tile(ubuf_ref, cw_ref, yc_ref, HALO + r0, r0, strip, width, slice(c0, c0 + LANES))
        dense(xb)

    @pl.when(i >= n_prompt_tiles)
    def _():
        xb = xb_ref[...]
        nb = tm // t_dec
        u = glu(xb)
        for c in range(tn // LANES):
            cs = slice(c * LANES, (c + 1) * LANES)
            tbuf_ref[c] = u[:, cs]
            new = [tbuf_ref[c, pl.ds(t, nb, stride=t_dec), :] for t in range(t_dec)]
            row = lambda p: cache_ref[p, :, cs] if p < hist else new[p - hist]
            for t in range(t_dec):
                acc = cw_ref[0:1, cs] * row(t)
                for k in range(1, width):
                    acc = acc + cw_ref[k:k + 1, cs] * row(t + k)
                ybuf_ref[c, pl.ds(t, nb, stride=t_dec), :] = acc
            yc_ref[:, cs] = ybuf_ref[c]
            for p in range(hist):
                css_ref[p, :, cs] = row(p + t_dec)
        dense(xb)


def _proj(x1, w, wif, cw, cache_t, hqk, k_scale, batch, seq, t_dec):
    n, d = x1.shape
    hist, dec_batch, _ = cache_t.shape
    width = cw.shape[0]
    tm, tn = min(TM_PROJ, seq), min(TN_PROJ, d)
    n_prompt = batch * seq
    assert n_prompt % tm == 0 and (n - n_prompt) % tm == 0 and seq % tm == 0 and d % tn == 0
    assert hqk % tn == 0 and 2 * hqk == d and tn % LANES == 0
    assert hist == width - 1 and SUBLANES * (hist // SUBLANES + 1) <= HALO <= tm and tm % CONV_ROWS == 0
    assert n - n_prompt == dec_batch * t_dec and tm % t_dec == 0 and (tm // t_dec) % SUBLANES == 0 and t_dec <= hist
    npt, tps, nb = n_prompt // tm, seq // tm, tm // t_dec
    per = d // tn
    col = pl.BlockSpec((tm, tn), lambda i, j: (i, j))
    wspec = lambda g: pl.BlockSpec((d, tn), lambda i, j: (0, g * per + j))
    smp_i = lambda i: jnp.maximum(i - npt, 0)
    smp_j = lambda i, j: jnp.where(i >= npt, j, 0)
    prm_i = lambda i: jnp.minimum(i, npt - 1)
    prm_j = lambda i, j: jnp.where(i < npt, j, per - 1)
    return pl.pallas_call(
        functools.partial(_proj_kernel, k_col0=hqk, k_scale=k_scale, n_prompt_tiles=npt, tiles_per_seq=tps,
                          width=width, t_dec=t_dec),
        out_shape=(
            jax.ShapeDtypeStruct((n, d), BF16),
            jax.ShapeDtypeStruct((n, d), BF16),
            jax.ShapeDtypeStruct((n, d), BF16),
            jax.ShapeDtypeStruct((n, d), BF16),
            jax.ShapeDtypeStruct((n, d), F32),
            jax.ShapeDtypeStruct((n, LANES), F32),
            jax.ShapeDtypeStruct((npt, hist, d), F32),
            jax.ShapeDtypeStruct((hist, dec_batch, d), F32),
        ),
        grid=(n // tm, d // tn),
        in_specs=[pl.BlockSpec((tm, d), lambda i, j: (i, 0)),
                  wspec(0), wspec(1), wspec(2), wspec(3), wspec(4), wspec(5), wspec(6),
                  pl.BlockSpec((d, LANES), lambda i, j: (0, 0)),
                  pl.BlockSpec((width, tn), lambda i, j: (0, j)),
                  pl.BlockSpec((hist, nb, tn), lambda i, j: (0, smp_i(i), smp_j(i, j)))],
        out_specs=(col, col, col, col, col, pl.BlockSpec((tm, LANES), lambda i, j: (i, 0)),
                   pl.BlockSpec((None, hist, tn), lambda i, j: (prm_i(i), 0, prm_j(i, j))),
                   pl.BlockSpec((hist, nb, tn), lambda i, j: (0, smp_i(i), smp_j(i, j)))),
        scratch_shapes=[pltpu.VMEM((tm, d), BF16), pltpu.VMEM((HALO + tm, tn), F32),
                        pltpu.VMEM((per, HALO, tn), F32),
                        pltpu.VMEM((tn // LANES, tm, LANES), F32), pltpu.VMEM((tn // LANES, tm, LANES), F32)],
        compiler_params=_params(("arbitrary", "arbitrary")),
        name="proj",
    )(x1, w, w, w, w, w, w, w, wif, cw, cache_t)


def _mlstm_chunk(q, k, v, ig_c, lf_c, c0, n0, m0, last):
    L = q.shape[0]
    row = lax.broadcasted_iota(jnp.int32, (L, L), 0)
    colm = lax.broadcasted_iota(jnp.int32, (L, L), 1)
    causal = colm <= row
    eye = colm == row
    b_r = jnp.sum(jnp.where(row <= colm, lf_c, 0.0), axis=0, keepdims=True)
    b_c = jnp.sum(jnp.where(eye, b_r, 0.0), axis=1, keepdims=True)
    ig_r = jnp.sum(jnp.where(eye, ig_c, 0.0), axis=0, keepdims=True)

    log_d = jnp.where(causal, b_c - b_r + ig_r, -jnp.inf)
    inter = b_c + m0
    m_t = jnp.maximum(inter, jnp.max(log_d, axis=1, keepdims=True))
    d = jnp.exp(log_d - m_t)
    w_inter = jnp.exp(inter - m_t)
    s = lax.dot_general(q, k, (((1,), (1,)), ((), ())), preferred_element_type=F32) * d
    num = _dot(s.astype(BF16), v) + w_inter * _dot(q, c0.astype(BF16))
    qn = jnp.sum(s, axis=1, keepdims=True) + w_inter * jnp.sum(q.astype(F32) * n0, axis=1, keepdims=True)
    den = jnp.maximum(jnp.abs(qn), jnp.exp(-m_t))
    h = num * (1.0 / den)

    m_new = m_t[last:last + 1, :]
    w_k = jnp.exp(b_c[last:last + 1, :] - b_c + ig_c - m_new)
    w_c = jnp.exp(inter[last:last + 1, :] - m_new)
    kw = k.astype(F32) * w_k
    c_new = w_c * c0 + lax.dot_general(kw.astype(BF16), v, (((0,), (0,)), ((), ())),
                                       preferred_element_type=F32)
    n_new = w_c * n0 + jnp.sum(kw, axis=0, keepdims=True)
    return h, c_new, n_new, m_new


def _head_norm(h, g):
    mu = jnp.mean(h, axis=-1, keepdims=True)
    hc = h - mu
    var = jnp.mean(hc * hc, axis=-1, keepdims=True)
    return hc * lax.rsqrt(var + LN_EPS) * g


def _log_sigmoid(x):
    return jnp.minimum(x, 0.0) - jnp.log1p(jnp.exp(-jnp.abs(x)))


def _mlstm_prompt_kernel(qk_ref, v_ref, if_ref, bias_ref, g_ref, hn_ref, c_ref, n_ref, m_ref,
                         *, n_heads, d_qk, d_v):
    c = pl.program_id(1)

    @pl.when(c == 0)
    def _():
        c_ref[...] = jnp.zeros_like(c_ref)
        n_ref[...] = jnp.zeros_like(n_ref)
        m_ref[...] = jnp.zeros_like(m_ref)

    gates = if_ref[...] + bias_ref[...]
    lf_all = _log_sigmoid(gates)
    L = gates.shape[0]
    for h in range(n_heads):
        q = qk_ref[:, h * d_qk:(h + 1) * d_qk]
        k = qk_ref[:, (n_heads + h) * d_qk:(n_heads + h + 1) * d_qk]
        v = v_ref[:, h * d_v:(h + 1) * d_v]
        hh, c_new, n_new, m_new = _mlstm_chunk(
            q, k, v, gates[:, h:h + 1], lf_all[:, n_heads + h:n_heads + h + 1],
            c_ref[0, h], n_ref[0, h], m_ref[0, h], L - 1)
        c_ref[0, h] = c_new
        n_ref[0, h] = n_new
        m_ref[0, h] = m_new
        hn_ref[:, h * d_v:(h + 1) * d_v] = _head_norm(hh, g_ref[:, h * d_v:(h + 1) * d_v]).astype(BF16)


def _mlstm_prompt(qk, v, iff, bias, g, batch, seq, n_heads, d_qk, d_v):
    d = v.shape[1]
    L = min(CHUNK, seq)
    assert seq % L == 0
    nc = seq // L
    tok = lambda b, c: (b * nc + c, 0)
    return pl.pallas_call(
        functools.partial(_mlstm_prompt_kernel, n_heads=n_heads, d_qk=d_qk, d_v=d_v),
        out_shape=(
            jax.ShapeDtypeStruct((batch * seq, d), BF16),
            jax.ShapeDtypeStruct((batch, n_heads, d_qk, d_v), F32),
            jax.ShapeDtypeStruct((batch, n_heads, 1, d_qk), F32),
            jax.ShapeDtypeStruct((batch, n_heads, 1, 1), F32),
        ),
        grid=(batch, nc),
        in_specs=[
            pl.BlockSpec((L, qk.shape[1]), tok),
            pl.BlockSpec((L, d), tok),
            pl.BlockSpec((L, LANES), tok),
            pl.BlockSpec((1, LANES), lambda b, c: (0, 0)),
            pl.BlockSpec((1, d), lambda b, c: (0, 0)),
        ],
        out_specs=(
            pl.BlockSpec((L, d), tok),
            pl.BlockSpec((1, n_heads, d_qk, d_v), lambda b, c: (b, 0, 0, 0)),
            pl.BlockSpec((1, n_heads, 1, d_qk), lambda b, c: (b, 0, 0, 0)),
            pl.BlockSpec((1, n_heads, 1, 1), lambda b, c: (b, 0, 0, 0)),
        ),
        compiler_params=_params(("parallel", "arbitrary")),
        name="mlstm_prompt",
    )(qk, v, iff, bias, g)


def _mlstm_sample_kernel(qk_ref, v_ref, if_ref, bias_ref, g_ref, c0_ref, n0_ref, m0_ref,
                         hn_ref, c_ref, n_ref, m_ref, *, n_heads, d_qk, d_v, t_dec, bs):
    pad = (-t_dec) % (2 * SUBLANES)
    zpad = lambda x: jnp.concatenate([x, jnp.zeros((pad, x.shape[1]), x.dtype)], axis=0) if pad else x
    qk_all = qk_ref[...].astype(F32)
    v_all = v_ref[...].astype(F32)
    for b in range(bs):
        rows = slice(b * t_dec, (b + 1) * t_dec)
        gates = zpad(if_ref[rows, :] + bias_ref[...])
        lf_all = _log_sigmoid(gates)
        qk = zpad(qk_all[rows, :])
        vv = zpad(v_all[rows, :])
        for h in range(n_heads):
            q = qk[:, h * d_qk:(h + 1) * d_qk].astype(BF16)
            k = qk[:, (n_heads + h) * d_qk:(n_heads + h + 1) * d_qk].astype(BF16)
            v = vv[:, h * d_v:(h + 1) * d_v].astype(BF16)
            hh, c_new, n_new, m_new = _mlstm_chunk(
                q, k, v, gates[:, h:h + 1], lf_all[:, n_heads + h:n_heads + h + 1],
                c0_ref[b, h], n0_ref[b, h], m0_ref[b, h], t_dec - 1)
            c_ref[b, h] = c_new
            n_ref[b, h] = n_new
            m_ref[b, h] = m_new
            hn = _head_norm(hh[:t_dec], g_ref[:, h * d_v:(h + 1) * d_v])
            hn_ref[rows, h * d_v:(h + 1) * d_v] = hn


def _mlstm_sample(qk, v, iff, bias, g, c0, n0, m0, row0, t_dec):
    batch, n_heads, d_qk, d_v = c0.shape
    d = v.shape[1]
    bs = min(BS_MLSTM, batch)
    assert batch % bs == 0 and row0 % (bs * t_dec) == 0 and (bs * t_dec) % SUBLANES == 0
    blk0 = row0 // (bs * t_dec)
    tok = lambda i: (blk0 + i, 0)
    st4 = lambda i: (i, 0, 0, 0)
    return pl.pallas_call(
        functools.partial(_mlstm_sample_kernel, n_heads=n_heads, d_qk=d_qk, d_v=d_v, t_dec=t_dec, bs=bs),
        out_shape=(
            jax.ShapeDtypeStruct((batch * t_dec, d), F32),
            jax.ShapeDtypeStruct((batch, n_heads, d_qk, d_v), F32),
            jax.ShapeDtypeStruct((batch, n_heads, 1, d_qk), F32),
            jax.ShapeDtypeStruct((batch, n_heads, 1, 1), F32),
        ),
        grid=(batch // bs,),
        in_specs=[
            pl.BlockSpec((bs * t_dec, qk.shape[1]), tok),
            pl.BlockSpec((bs * t_dec, d), tok),
            pl.BlockSpec((bs * t_dec, LANES), tok),
            pl.BlockSpec((1, LANES), lambda i: (0, 0)),
            pl.BlockSpec((1, d), lambda i: (0, 0)),
            pl.BlockSpec((bs, n_heads, d_qk, d_v), st4),
            pl.BlockSpec((bs, n_heads, 1, d_qk), st4),
            pl.BlockSpec((bs, n_heads, 1, 1), st4),
        ],
        out_specs=(
            pl.BlockSpec((bs * t_dec, d), lambda i: (i, 0)),
            pl.BlockSpec((bs, n_heads, d_qk, d_v), st4),
            pl.BlockSpec((bs, n_heads, 1, d_qk), st4),
            pl.BlockSpec((bs, n_heads, 1, 1), st4),
        ),
        compiler_params=_params(("parallel",)),
        name="mlstm_sample",
    )(qk, v, iff, bias, g, c0, n0, m0)


def _mix_ln_kernel(yc_ref, ga_ref, gb_ref, hnp_ref, hns_ref, x1_ref, cb_ref, cg_ref, cbb_ref, w_ref, g_ref, b_ref,
                   o_ref, *, alpha, split):
    def body(hn_ref):
        z = _layer_norm(yc_ref[...] + cb_ref[...], cg_ref[...], cbb_ref[...])
        hb = z * _sigmoid(z)
        mixin = ga_ref[...].astype(F32) * hn_ref[...].astype(F32) + gb_ref[...].astype(F32) * hb
        mix = _dot(mixin.astype(BF16), w_ref[...])
        o_ref[...] = _layer_norm(alpha * x1_ref[...] + mix, g_ref[...], b_ref[...])

    _for_owner((hnp_ref, hns_ref), split, pl.program_id(0), body)


def _mix_ln(yc, ga, gb, hn_p, hn_s, x1, cb, cg, cbb, w, g, b, alpha):
    n, d = x1.shape
    tm = min(TM_MIX, hn_s.shape[0])
    assert hn_p.shape[0] % tm == 0 and hn_s.shape[0] % tm == 0 and hn_p.shape[0] + hn_s.shape[0] == n
    tok = pl.BlockSpec((tm, d), lambda i: (i, 0))
    vec = pl.BlockSpec((1, d), lambda i: (0, 0))
    hn_specs, split = _segment_specs([hn_p.shape[0], hn_s.shape[0]], tm, d)
    return pl.pallas_call(
        functools.partial(_mix_ln_kernel, alpha=alpha, split=split),
        out_shape=jax.ShapeDtypeStruct((n, d), F32),
        grid=(n // tm,),
        in_specs=[tok, tok, tok, *hn_specs, tok, vec, vec, vec, pl.BlockSpec((d, d), lambda i: (0, 0)), vec, vec],
        out_specs=tok,
        compiler_params=_params(("arbitrary",)),
        name="mix_ln",
    )(yc, ga, gb, hn_p, hn_s, x1, cb, cg, cbb, w, g, b)


def _layer(xs, layer, batch, seq, dec_batch, t_dec, c0, n0, m0, cache, w_in, p, alpha):
    (ffn1_w1, ffn1_w3, ffn1_w2, ln1_g, ln1_b, b_igate, b_fgate, mh_norm_g,
     conv_w, conv_b, conv_ln_g, conv_ln_b, w_out, ln2_g, ln2_b,
     ffn2_w1, ffn2_w3, ffn2_w2, ln3_g, ln3_b) = p
    d = xs[0].shape[1]
    n_heads, d_qk, d_v = c0.shape[1:]
    n_prompt, n_sample = batch * seq, dec_batch * t_dec
    hqk, dm = n_heads * d_qk, n_heads * d_v
    assert 2 * hqk == d and dm == d and conv_w.shape[1] == d
    vec = lambda a: a.reshape(1, -1).astype(F32)
    bf = lambda a: a.astype(BF16)

    (x1,) = _ffn_ln(xs, bf(ffn1_w1), bf(ffn1_w3), bf(ffn1_w2), vec(ln1_g), vec(ln1_b), alpha,
                    (n_prompt + n_sample,))
    w, wif = _prep_w_in(jnp.swapaxes(w_in, 1, 2), layer, 2 * hqk + 2 * dm, 2 * n_heads, 4 * d)
    qk, v, ga, gb, yc, iff, tile_tails, conv_s_t = _proj(x1, w, wif, conv_w.astype(F32), jnp.swapaxes(cache, 0, 1),
                                                         hqk, float(d_qk) ** -0.5, batch, seq, t_dec)

    gate_bias = jnp.pad(jnp.concatenate([b_igate, b_fgate]).astype(F32), (0, LANES - 2 * n_heads)).reshape(1, LANES)
    mh_g = vec(mh_norm_g)
    hn_p, c_p, n_p, m_p = _mlstm_prompt(qk, v, iff, gate_bias, mh_g, batch, seq, n_heads, d_qk, d_v)
    hn_s, c_s, n_s, m_s = _mlstm_sample(qk, v, iff, gate_bias, mh_g, c0,
                                        n0.reshape(dec_batch, n_heads, 1, d_qk),
                                        m0.reshape(dec_batch, n_heads, 1, 1), n_prompt, t_dec)

    x2 = _mix_ln(yc, ga, gb, hn_p, hn_s, x1, vec(conv_b), vec(conv_ln_g), vec(conv_ln_b), bf(w_out),
                 vec(ln2_g), vec(ln2_b), alpha)
    y_p, y_s = _ffn_ln((x2,), bf(ffn2_w1), bf(ffn2_w3), bf(ffn2_w2), vec(ln3_g), vec(ln3_b), alpha,
                       (n_prompt, n_sample))
    tiles_per_seq = tile_tails.shape[0] // batch
    conv_p = tile_tails[tiles_per_seq - 1::tiles_per_seq]
    states_p = (c_p, n_p.reshape(batch, n_heads, d_qk), m_p.reshape(batch, n_heads), conv_p)
    states_s = (c_s, n_s.reshape(dec_batch, n_heads, d_qk), m_s.reshape(dec_batch, n_heads),
                jnp.swapaxes(conv_s_t, 0, 1))
    return (y_p, y_s), states_p, states_s


def kernel(x_prompt, x_sample, state_C, state_n, state_m, cache_conv, ffn1_w1, ffn1_w3, ffn1_w2, ln1_g, ln1_b, w_in, b_igate, b_fgate, mh_norm_g, conv_w, conv_b, conv_ln_g, conv_ln_b, w_out, ln2_g, ln2_b, ffn2_w1, ffn2_w3, ffn2_w2, ln3_g, ln3_b):
    batch, seq, d = x_prompt.shape
    dec_batch, t_dec, _ = x_sample.shape
    depth = ffn1_w1.shape[0]
    alpha = (2.0 * depth) ** 0.25
    xs = (x_prompt.reshape(batch * seq, d), x_sample.reshape(dec_batch * t_dec, d))
    weights = (ffn1_w1, ffn1_w3, ffn1_w2, ln1_g, ln1_b, b_igate, b_fgate, mh_norm_g,
               conv_w, conv_b, conv_ln_g, conv_ln_b, w_out, ln2_g, ln2_b,
               ffn2_w1, ffn2_w3, ffn2_w2, ln3_g, ln3_b)
    outs_p, outs_s = [], []
    for l in range(depth):
        p = tuple(wt[l] for wt in weights)
        xs, st_p, st_s = _layer(xs, l, batch, seq, dec_batch, t_dec, state_C[l], state_n[l], state_m[l],
                                cache_conv[l], w_in, p, alpha)
        outs_p.append(st_p)
        outs_s.append(st_s)
    y_p = xs[0].reshape(batch, seq, d)
    y_s = xs[1].reshape(dec_batch, t_dec, d)
    stack = lambda outs, i: jnp.stack([o[i] for o in outs])
    return (y_p, y_s,
            stack(outs_p, 0), stack(outs_p, 1), stack(outs_p, 2), stack(outs_p, 3),
            stack(outs_s, 0), stack(outs_s, 1), stack(outs_s, 2), stack(outs_s, 3))
```

```python
import functools

import jax
import jax.numpy as jnp
from jax import lax
from jax.experimental import pallas as pl
from jax.experimental.pallas import tpu as pltpu

F32 = jnp.float32
BF16 = jnp.bfloat16
LN_EPS = 1e-5
LANES = 128
SUBLANES = 8
HALO = 32
VMEM_LIMIT = 56 * 1024 * 1024

TM_FFN = 512
TF_FFN = 512
TF_FFN_FIRST = 256
TM_PROJ = 512
TN_PROJ = 256
TW_PREP = 512
TR_PREP = 512
TM_MIX = 256
CONV_ROWS = 64
BS_MLSTM = 2
CHUNK = 128


def _params(sem):
    return pltpu.CompilerParams(dimension_semantics=sem, vmem_limit_bytes=VMEM_LIMIT)


def _sigmoid(x):
    return jax.nn.sigmoid(x)


def _layer_norm(z, g, b):
    mu = jnp.mean(z, axis=-1, keepdims=True)
    zc = z - mu
    var = jnp.mean(zc * zc, axis=-1, keepdims=True)
    return zc * lax.rsqrt(var + LN_EPS) * g + b


def _dot(a, b):
    return jnp.dot(a, b, preferred_element_type=F32)


def _for_owner(refs, split, i, fn):
    if len(refs) == 1:
        fn(refs[0])
        return
    pl.when(i < split)(lambda: fn(refs[0]))
    pl.when(i >= split)(lambda: fn(refs[1]))


def _segment_specs(arrays_rows, tm, d, tile0=0):
    if len(arrays_rows) == 1:
        return [pl.BlockSpec((tm, d), lambda i, *_: (i + tile0, 0))], 0
    split = arrays_rows[0] // tm
    last0 = split - 1
    return [pl.BlockSpec((tm, d), lambda i, *_: (jnp.minimum(i + tile0, last0), 0)),
            pl.BlockSpec((tm, d), lambda i, *_: (jnp.maximum(i + tile0 - split, 0), 0))], split


def _ffn_ln_kernel(*refs, alpha, n_in, n_out, in_split, out_split, tile0, n_pass, emit_w):
    x_refs = refs[:n_in]
    w1_ref, w3_ref, w2_ref, g_ref, b_ref = refs[n_in:n_in + 5]
    pos = n_in + 5 + n_pass
    o_refs = refs[pos:pos + n_out]
    pos += n_out
    wb_refs = refs[pos:pos + 3] if emit_w else ()
    xb_ref, acc_ref = refs[pos + len(wb_refs):]
    i = pl.program_id(0) + tile0
    f = pl.program_id(1)

    @pl.when(f == 0)
    def _():
        def cast(x_ref):
            xb_ref[...] = x_ref[...].astype(BF16)
        _for_owner(x_refs, in_split, i, cast)
        acc_ref[...] = jnp.zeros_like(acc_ref)

    w1, w3, w2 = (r[...].astype(BF16) for r in (w1_ref, w3_ref, w2_ref))
    for wb_ref, w in zip(wb_refs, (w1, w3, w2)):
        wb_ref[...] = w
    xb = xb_ref[...]
    a = _dot(xb, w1)
    c = _dot(xb, w3)
    h = (a * _sigmoid(a) * c).astype(BF16)
    acc_ref[...] += _dot(h, w2)

    @pl.when(f == pl.num_programs(1) - 1)
    def _():
        def finish(x_ref):
            z = alpha * x_ref[...] + 0.5 * acc_ref[...]
            acc_ref[...] = _layer_norm(z, g_ref[...], b_ref[...])
        _for_owner(x_refs, in_split, i, finish)

        def emit(o_ref):
            o_ref[...] = acc_ref[...]
        _for_owner(o_refs, out_split, i, emit)


def _ffn_ln(xs, w1, w3, w2, g, b, alpha, out_rows):
    d = xs[0].shape[1]
    n = sum(x.shape[0] for x in xs)
    dff = w1.shape[1]
    tm, tf, tf0 = min(TM_FFN, n), min(TF_FFN, dff), min(TF_FFN_FIRST, dff)
    assert sum(out_rows) == n and dff % tf == 0 and dff % tf0 == 0 and n // tm >= 2
    assert all(x.shape[0] % tm == 0 for x in xs) and all(r % tm == 0 for r in out_rows)
    vec = pl.BlockSpec((1, d), lambda i, f: (0, 0))
    w_specs = lambda t: [pl.BlockSpec((d, t), lambda i, f: (0, f)), pl.BlockSpec((d, t), lambda i, f: (0, f)),
                         pl.BlockSpec((t, d), lambda i, f: (f, 0))]
    scratch = [pltpu.VMEM((tm, d), BF16), pltpu.VMEM((tm, d), F32)]
    kern = functools.partial(_ffn_ln_kernel, alpha=alpha)

    tile = pl.BlockSpec((tm, d), lambda i, f: (0, 0))
    o0, w1b, w3b, w2b = pl.pallas_call(
        functools.partial(kern, n_in=1, n_out=1, in_split=0, out_split=0, tile0=0, n_pass=0, emit_w=True),
        out_shape=(jax.ShapeDtypeStruct((out_rows[0], d), F32),
                   *(jax.ShapeDtypeStruct(w.shape, BF16) for w in (w1, w3, w2))),
        grid=(1, dff // tf0),
        in_specs=[tile] + w_specs(tf0) + [vec, vec],
        out_specs=(tile, *w_specs(tf0)),
        scratch_shapes=scratch,
        compiler_params=_params(("arbitrary", "arbitrary")),
        name="ffn_ln_first",
    )(xs[0], w1, w3, w2, g, b)

    in_specs, in_split = _segment_specs([x.shape[0] for x in xs], tm, d, tile0=1)
    out_specs, out_split = _segment_specs(list(out_rows), tm, d, tile0=1)
    return pl.pallas_call(
        functools.partial(kern, n_in=len(xs), n_out=len(out_rows), in_split=in_split, out_split=out_split,
                          tile0=1, n_pass=1, emit_w=False),
        out_shape=tuple(jax.ShapeDtypeStruct((r, d), F32) for r in out_rows),
        grid=(n // tm - 1, dff // tf),
        in_specs=in_specs + w_specs(tf) + [vec, vec, pl.BlockSpec(memory_space=pl.ANY)],
        out_specs=tuple(out_specs),
        input_output_aliases={len(xs) + 5: 0},
        scratch_shapes=scratch,
        compiler_params=_params(("arbitrary", "arbitrary")),
        name="ffn_ln",
    )(*xs, w1b, w3b, w2b, g, b, o0)


def _prep_w_in_kernel(a_ref, nb_ref, if_ref, w_ref, wif_ref, *, n_aligned, n_gate):
    j = pl.program_id(0)
    tw, k_rows = a_ref.shape
    tr = min(TR_PREP, k_rows)

    def emit(rows_of):
        for c0 in range(0, k_rows, tr):
            w_ref[c0:c0 + tr, :] = rows_of(c0).T.astype(BF16)

    @pl.when(j < n_aligned)
    def _():
        emit(lambda c0: a_ref[:, c0:c0 + tr])

    @pl.when(j >= n_aligned)
    def _():
        emit(lambda c0: jnp.concatenate([a_ref[n_gate:tw, c0:c0 + tr], nb_ref[:, c0:c0 + tr]], axis=0))

    @pl.when(j == 0)
    def _():
        for c0 in range(0, k_rows, tr):
            rows = jnp.concatenate([if_ref[:, c0:c0 + tr], jnp.zeros((LANES - n_gate, tr), F32)], axis=0)
            wif_ref[c0:c0 + tr, :] = rows.T.astype(BF16)


def _prep_w_in(w_in_t, layer, n_aligned_cols, n_gate, n_shifted_cols):
    _, n_in, k_rows = w_in_t.shape
    tw = TW_PREP
    assert n_aligned_cols % tw == 0 and n_shifted_cols % tw == 0 and n_gate == SUBLANES
    assert k_rows % min(TR_PREP, k_rows) == 0 and n_aligned_cols + n_gate + n_shifted_cols == n_in
    n_aligned = n_aligned_cols // tw
    n_tiles = n_aligned + n_shifted_cols // tw
    per = tw // n_gate
    return pl.pallas_call(
        functools.partial(_prep_w_in_kernel, n_aligned=n_aligned, n_gate=n_gate),
        out_shape=(jax.ShapeDtypeStruct((k_rows, n_tiles * tw), BF16),
                   jax.ShapeDtypeStruct((k_rows, LANES), BF16)),
        grid=(n_tiles,),
        in_specs=[
            pl.BlockSpec((None, tw, k_rows), lambda j: (layer, j, 0)),
            pl.BlockSpec((None, n_gate, k_rows), lambda j: (layer, (j + 1) * per, 0)),
            pl.BlockSpec((None, n_gate, k_rows), lambda j: (layer, n_aligned * per, 0)),
        ],
        out_specs=(pl.BlockSpec((k_rows, tw), lambda j: (0, j)),
                   pl.BlockSpec((k_rows, LANES), lambda j: (0, 0))),
        compiler_params=_params(("arbitrary",)),
        name="prep_w_in",
    )(w_in_t, w_in_t, w_in_t)


def _conv_tile(ubuf_ref, w_ref, y_ref, u0, out0, rows, width, cs):
    y = None
    for r in range(min(SUBLANES, width)):
        q = None
        for a in range((width - 1 - r) // SUBLANES + 1):
            k = width - 1 - (SUBLANES * a + r)
            term = w_ref[k:k + 1, cs] * ubuf_ref[pl.ds(u0 - SUBLANES * (a + 1), rows + SUBLANES), cs]
            q = term if q is None else q + term
        part = q[SUBLANES - r:SUBLANES - r + rows]
        y = part if y is None else y + part
    y_ref[pl.ds(out0, rows), cs] = y


def _proj_kernel(x_ref, wqk_ref, wv_ref, wo_ref, wla_ref, wlb_ref, wga_ref, wgb_ref, wif_ref, cw_ref, cache_ref,
                 qk_ref, v_ref, ga_ref, gb_ref, yc_ref, if_ref, csp_ref, css_ref,
                 xb_ref, ubuf_ref, carry_ref, tbuf_ref, ybuf_ref, *, k_col0, k_scale, n_prompt_tiles, tiles_per_seq, width, t_dec):
    i = pl.program_id(0)
    j = pl.program_id(1)
    tm, tn = yc_ref.shape
    hist = width - 1

    @pl.when(j == 0)
    def _():
        xb0 = x_ref[...].astype(BF16)
        xb_ref[...] = xb0
        if_ref[...] = _dot(xb0, wif_ref[...])

    def glu(xb):
        return _dot(xb, wla_ref[...]) * _sigmoid(_dot(xb, wlb_ref[...]))

    def dense(xb):
        qk_ref[...] = (_dot(xb, wqk_ref[...]) * jnp.where(j * tn >= k_col0, k_scale, 1.0)).astype(BF16)
        v_ref[...] = _dot(xb, wv_ref[...]).astype(BF16)
        ga_ref[...] = (_sigmoid(_dot(xb, wo_ref[...])) * _sigmoid(_dot(xb, wga_ref[...]))).astype(BF16)
        gb_ref[...] = _sigmoid(_dot(xb, wgb_ref[...])).astype(BF16)

    @pl.when(i < n_prompt_tiles)
    def _():
        xb = xb_ref[...]
        u = glu(xb)
        first = (i % tiles_per_seq) == 0
        ubuf_ref[0:HALO, :] = jnp.where(first, 0.0, carry_ref[j])
        ubuf_ref[HALO:HALO + tm, :] = u
        carry_ref[j] = u[tm - HALO:tm]
        csp_ref[...] = u[tm - hist:tm]
        strip = min(CONV_ROWS, tm)
        for c0 in range(0, tn, LANES):
            for r0 in range(0, tm, strip):
                _conv_tile(ubuf_ref, cw_ref, yc_ref, HALO + r0, r0, strip, width, slice(c0, c0 + LANES))
        dense(xb)

    @pl.when(i >= n_prompt_tiles)
    def _():
        xb = xb_ref[...]
        nb = tm // t_dec
        u = glu(xb)
        for c in range(tn // LANES):
            cs = slice(c * LANES, (c + 1) * LANES)
            tbuf_ref[c] = u[:, cs]
            new = [tbuf_ref[c, pl.ds(t, nb, stride=t_dec), :] for t in range(t_dec)]
            row = lambda p: cache_ref[p, :, cs] if p < hist else new[p - hist]
            for t in range(t_dec):
                acc = cw_ref[0:1, cs] * row(t)
                for k in range(1, width):
                    acc = acc + cw_ref[k:k + 1, cs] * row(t + k)
                ybuf_ref[c, pl.ds(t, nb, stride=t_dec), :] = acc
            yc_ref[:, cs] = ybuf_ref[c]
            for p in range(hist):
                css_ref[p, :, cs] = row(p + t_dec)
        dense(xb)


def _proj(x1, w, wif, cw, cache_t, hqk, k_scale, batch, seq, t_dec):
    n, d = x1.shape
    hist, dec_batch, _ = cache_t.shape
    width = cw.shape[0]
    tm, tn = min(TM_PROJ, seq), min(TN_PROJ, d)
    n_prompt = batch * seq
    assert n_prompt % tm == 0 and (n - n_prompt) % tm == 0 and seq % tm == 0 and d % tn == 0
    assert hqk % tn == 0 and 2 * hqk == d and tn % LANES == 0
    assert hist == width - 1 and SUBLANES * (hist // SUBLANES + 1) <= HALO <= tm and tm % CONV_ROWS == 0
    assert n - n_prompt == dec_batch * t_dec and tm % t_dec == 0 and (tm // t_dec) % SUBLANES == 0 and t_dec <= hist
    npt, tps, nb = n_prompt // tm, seq // tm, tm // t_dec
    per = d // tn
    col = pl.BlockSpec((tm, tn), lambda i, j: (i, j))
    wspec = lambda g: pl.BlockSpec((d, tn), lambda i, j: (0, g * per + j))
    smp_i = lambda i: jnp.maximum(i - npt, 0)
    smp_j = lambda i, j: jnp.where(i >= npt, j, 0)
    prm_i = lambda i: jnp.minimum(i, npt - 1)
    prm_j = lambda i, j: jnp.where(i < npt, j, per - 1)
    return pl.pallas_call(
        functools.partial(_proj_kernel, k_col0=hqk, k_scale=k_scale, n_prompt_tiles=npt, tiles_per_seq=tps,
                          width=width, t_dec=t_dec),
        out_shape=(
            jax.ShapeDtypeStruct((n, d), BF16),
            jax.ShapeDtypeStruct((n, d), BF16),
            jax.ShapeDtypeStruct((n, d), BF16),
            jax.ShapeDtypeStruct((n, d), BF16),
            jax.ShapeDtypeStruct((n, d), F32),
            jax.ShapeDtypeStruct((n, LANES), F32),
            jax.ShapeDtypeStruct((npt, hist, d), F32),
            jax.ShapeDtypeStruct((hist, dec_batch, d), F32),
        ),
        grid=(n // tm, d // tn),
        in_specs=[pl.BlockSpec((tm, d), lambda i, j: (i, 0)),
                  wspec(0), wspec(1), wspec(2), wspec(3), wspec(4), wspec(5), wspec(6),
                  pl.BlockSpec((d, LANES), lambda i, j: (0, 0)),
                  pl.BlockSpec((width, tn), lambda i, j: (0, j)),
                  pl.BlockSpec((hist, nb, tn), lambda i, j: (0, smp_i(i), smp_j(i, j)))],
        out_specs=(col, col, col, col, col, pl.BlockSpec((tm, LANES), lambda i, j: (i, 0)),
                   pl.BlockSpec((None, hist, tn), lambda i, j: (prm_i(i), 0, prm_j(i, j))),
                   pl.BlockSpec((hist, nb, tn), lambda i, j: (0, smp_i(i), smp_j(i, j)))),
        scratch_shapes=[pltpu.VMEM((tm, d), BF16), pltpu.VMEM((HALO + tm, tn), F32),
                        pltpu.VMEM((per, HALO, tn), F32),
                        pltpu.VMEM((tn // LANES, tm, LANES), F32), pltpu.VMEM((tn // LANES, tm, LANES), F32)],
        compiler_params=_params(("arbitrary", "arbitrary")),
        name="proj",
    )(x1, w, w, w, w, w, w, w, wif, cw, cache_t)


def _mlstm_chunk(q, k, v, ig_c, lf_c, c0, n0, m0, last):
    L = q.shape[0]
    row = lax.broadcasted_iota(jnp.int32, (L, L), 0)
    colm = lax.broadcasted_iota(jnp.int32, (L, L), 1)
    causal = colm <= row
    eye = colm == row
    b_r = jnp.sum(jnp.where(row <= colm, lf_c, 0.0), axis=0, keepdims=True)
    b_c = jnp.sum(jnp.where(eye, b_r, 0.0), axis=1, keepdims=True)
    ig_r = jnp.sum(jnp.where(eye, ig_c, 0.0), axis=0, keepdims=True)

    log_d = jnp.where(causal, b_c - b_r + ig_r, -jnp.inf)
    inter = b_c + m0
    m_t = jnp.maximum(inter, jnp.max(log_d, axis=1, keepdims=True))
    d = jnp.exp(log_d - m_t)
    w_inter = jnp.exp(inter - m_t)
    s = lax.dot_general(q, k, (((1,), (1,)), ((), ())), preferred_element_type=F32) * d
    num = _dot(s.astype(BF16), v) + w_inter * _dot(q, c0.astype(BF16))
    qn = jnp.sum(s, axis=1, keepdims=True) + w_inter * jnp.sum(q.astype(F32) * n0, axis=1, keepdims=True)
    den = jnp.maximum(jnp.abs(qn), jnp.exp(-m_t))
    h = num * (1.0 / den)

    m_new = m_t[last:last + 1, :]
    w_k = jnp.exp(b_c[last:last + 1, :] - b_c + ig_c - m_new)
    w_c = jnp.exp(inter[last:last + 1, :] - m_new)
    kw = k.astype(F32) * w_k
    c_new = w_c * c0 + lax.dot_general(kw.astype(BF16), v, (((0,), (0,)), ((), ())),
                                       preferred_element_type=F32)
    n_new = w_c * n0 + jnp.sum(kw, axis=0, keepdims=True)
    return h, c_new, n_new, m_new


def _head_norm(h, g):
    mu = jnp.mean(h, axis=-1, keepdims=True)
    hc = h - mu
    var = jnp.mean(hc * hc, axis=-1, keepdims=True)
    return hc * lax.rsqrt(var + LN_EPS) * g


def _log_sigmoid(x):
    return jnp.minimum(x, 0.0) - jnp.log1p(jnp.exp(-jnp.abs(x)))


def _mlstm_prompt_kernel(qk_ref, v_ref, if_ref, bias_ref, g_ref, hn_ref, c_ref, n_ref, m_ref,
                         *, n_heads, d_qk, d_v):
    c = pl.program_id(1)

    @pl.when(c == 0)
    def _():
        c_ref[...] = jnp.zeros_like(c_ref)
        n_ref[...] = jnp.zeros_like(n_ref)
        m_ref[...] = jnp.zeros_like(m_ref)

    gates = if_ref[...] + bias_ref[...]
    lf_all = _log_sigmoid(gates)
    L = gates.shape[0]
    for h in range(n_heads):
        q = qk_ref[:, h * d_qk:(h + 1) * d_qk]
        k = qk_ref[:, (n_heads + h) * d_qk:(n_heads + h + 1) * d_qk]
        v = v_ref[:, h * d_v:(h + 1) * d_v]
        hh, c_new, n_new, m_new = _mlstm_chunk(
            q, k, v, gates[:, h:h + 1], lf_all[:, n_heads + h:n_heads + h + 1],
            c_ref[0, h], n_ref[0, h], m_ref[0, h], L - 1)
        c_ref[0, h] = c_new
        n_ref[0, h] = n_new
        m_ref[0, h] = m_new
        hn_ref[:, h * d_v:(h + 1) * d_v] = _head_norm(hh, g_ref[:, h * d_v:(h + 1) * d_v]).astype(BF16)


def _mlstm_prompt(qk, v, iff, bias, g, batch, seq, n_heads, d_qk, d_v):
    d = v.shape[1]
    L = min(CHUNK, seq)
    assert seq % L == 0
    nc = seq // L
    tok = lambda b, c: (b * nc + c, 0)
    return pl.pallas_call(
        functools.partial(_mlstm_prompt_kernel, n_heads=n_heads, d_qk=d_qk, d_v=d_v),
        out_shape=(
            jax.ShapeDtypeStruct((batch * seq, d), BF16),
            jax.ShapeDtypeStruct((batch, n_heads, d_qk, d_v), F32),
            jax.ShapeDtypeStruct((batch, n_heads, 1, d_qk), F32),
            jax.ShapeDtypeStruct((batch, n_heads, 1, 1), F32),
        ),
        grid=(batch, nc),
        in_specs=[
            pl.BlockSpec((L, qk.shape[1]), tok),
            pl.BlockSpec((L, d), tok),
            pl.BlockSpec((L, LANES), tok),
            pl.BlockSpec((1, LANES), lambda b, c: (0, 0)),
            pl.BlockSpec((1, d), lambda b, c: (0, 0)),
        ],
        out_specs=(
            pl.BlockSpec((L, d), tok),
            pl.BlockSpec((1, n_heads, d_qk, d_v), lambda b, c: (b, 0, 0, 0)),
            pl.BlockSpec((1, n_heads, 1, d_qk), lambda b, c: (b, 0, 0, 0)),
            pl.BlockSpec((1, n_heads, 1, 1), lambda b, c: (b, 0, 0, 0)),
        ),
        compiler_params=_params(("parallel", "arbitrary")),
        name="mlstm_prompt",
    )(qk, v, iff, bias, g)


def _mlstm_sample_kernel(qk_ref, v_ref, if_ref, bias_ref, g_ref, c0_ref, n0_ref, m0_ref,
                         hn_ref, c_ref, n_ref, m_ref, *, n_heads, d_qk, d_v, t_dec, bs):
    pad = (-t_dec) % (2 * SUBLANES)
    zpad = lambda x: jnp.concatenate([x, jnp.zeros((pad, x.shape[1]), x.dtype)], axis=0) if pad else x
    qk_all = qk_ref[...].astype(F32)
    v_all = v_ref[...].astype(F32)
    for b in range(bs):
        rows = slice(b * t_dec, (b + 1) * t_dec)
        gates = zpad(if_ref[rows, :] + bias_ref[...])
        lf_all = _log_sigmoid(gates)
        qk = zpad(qk_all[rows, :])
        vv = zpad(v_all[rows, :])
        for h in range(n_heads):
            q = qk[:, h * d_qk:(h + 1) * d_qk].astype(BF16)
            k = qk[:, (n_heads + h) * d_qk:(n_heads + h + 1) * d_qk].astype(BF16)
            v = vv[:, h * d_v:(h + 1) * d_v].astype(BF16)
            hh, c_new, n_new, m_new = _mlstm_chunk(
                q, k, v, gates[:, h:h + 1], lf_all[:, n_heads + h:n_heads + h + 1],
                c0_ref[b, h], n0_ref[b, h], m0_ref[b, h], t_dec - 1)
            c_ref[b, h] = c_new
            n_ref[b, h] = n_new
            m_ref[b, h] = m_new
            hn = _head_norm(hh[:t_dec], g_ref[:, h * d_v:(h + 1) * d_v])
            hn_ref[rows, h * d_v:(h + 1) * d_v] = hn


def _mlstm_sample(qk, v, iff, bias, g, c0, n0, m0, row0, t_dec):
    batch, n_heads, d_qk, d_v = c0.shape
    d = v.shape[1]
    bs = min(BS_MLSTM, batch)
    assert batch % bs == 0 and row0 % (bs * t_dec) == 0 and (bs * t_dec) % SUBLANES == 0
    blk0 = row0 // (bs * t_dec)
    tok = lambda i: (blk0 + i, 0)
    st4 = lambda i: (i, 0, 0, 0)
    return pl.pallas_call(
        functools.partial(_mlstm_sample_kernel, n_heads=n_heads, d_qk=d_qk, d_v=d_v, t_dec=t_dec, bs=bs),
        out_shape=(
            jax.ShapeDtypeStruct((batch * t_dec, d), F32),
            jax.ShapeDtypeStruct((batch, n_heads, d_qk, d_v), F32),
            jax.ShapeDtypeStruct((batch, n_heads, 1, d_qk), F32),
            jax.ShapeDtypeStruct((batch, n_heads, 1, 1), F32),
        ),
        grid=(batch // bs,),
        in_specs=[
            pl.BlockSpec((bs * t_dec, qk.shape[1]), tok),
            pl.BlockSpec((bs * t_dec, d), tok),
            pl.BlockSpec((bs * t_dec, LANES), tok),
            pl.BlockSpec((1, LANES), lambda i: (0, 0)),
            pl.BlockSpec((1, d), lambda i: (0, 0)),
            pl.BlockSpec((bs, n_heads, d_qk, d_v), st4),
            pl.BlockSpec((bs, n_heads, 1, d_qk), st4),
            pl.BlockSpec((bs, n_heads, 1, 1), st4),
        ],
        out_specs=(
            pl.BlockSpec((bs * t_dec, d), lambda i: (i, 0)),
            pl.BlockSpec((bs, n_heads, d_qk, d_v), st4),
            pl.BlockSpec((bs, n_heads, 1, d_qk), st4),
            pl.BlockSpec((bs, n_heads, 1, 1), st4),
        ),
        compiler_params=_params(("parallel",)),
        name="mlstm_sample",
    )(qk, v, iff, bias, g, c0, n0, m0)


def _mix_ln_kernel(yc_ref, ga_ref, gb_ref, hnp_ref, hns_ref, x1_ref, cb_ref, cg_ref, cbb_ref, w_ref, g_ref, b_ref,
                   o_ref, *, alpha, split):
    def body(hn_ref):
        z = _layer_norm(yc_ref[...] + cb_ref[...], cg_ref[...], cbb_ref[...])
        hb = z * _sigmoid(z)
        mixin = ga_ref[...].astype(F32) * hn_ref[...].astype(F32) + gb_ref[...].astype(F32) * hb
        mix = _dot(mixin.astype(BF16), w_ref[...])
        o_ref[...] = _layer_norm(alpha * x1_ref[...] + mix, g_ref[...], b_ref[...])

    _for_owner((hnp_ref, hns_ref), split, pl.program_id(0), body)


def _mix_ln(yc, ga, gb, hn_p, hn_s, x1, cb, cg, cbb, w, g, b, alpha):
    n, d = x1.shape
    tm = min(TM_MIX, hn_s.shape[0])
    assert hn_p.shape[0] % tm == 0 and hn_s.shape[0] % tm == 0 and hn_p.shape[0] + hn_s.shape[0] == n
    tok = pl.BlockSpec((tm, d), lambda i: (i, 0))
    vec = pl.BlockSpec((1, d), lambda i: (0, 0))
    hn_specs, split = _segment_specs([hn_p.shape[0], hn_s.shape[0]], tm, d)
    return pl.pallas_call(
        functools.partial(_mix_ln_kernel, alpha=alpha, split=split),
        out_shape=jax.ShapeDtypeStruct((n, d), F32),
        grid=(n // tm,),
        in_specs=[tok, tok, tok, *hn_specs, tok, vec, vec, vec, pl.BlockSpec((d, d), lambda i: (0, 0)), vec, vec],
        out_specs=tok,
        compiler_params=_params(("arbitrary",)),
        name="mix_ln",
    )(yc, ga, gb, hn_p, hn_s, x1, cb, cg, cbb, w, g, b)


def _layer(xs, layer, batch, seq, dec_batch, t_dec, c0, n0, m0, cache, w_in, p, alpha):
    (ffn1_w1, ffn1_w3, ffn1_w2, ln1_g, ln1_b, b_igate, b_fgate, mh_norm_g,
     conv_w, conv_b, conv_ln_g, conv_ln_b, w_out, ln2_g, ln2_b,
     ffn2_w1, ffn2_w3, ffn2_w2, ln3_g, ln3_b) = p
    d = xs[0].shape[1]
    n_heads, d_qk, d_v = c0.shape[1:]
    n_prompt, n_sample = batch * seq, dec_batch * t_dec
    hqk, dm = n_heads * d_qk, n_heads * d_v
    assert 2 * hqk == d and dm == d and conv_w.shape[1] == d
    vec = lambda a: a.reshape(1, -1).astype(F32)
    bf = lambda a: a.astype(BF16)

    (x1,) = _ffn_ln(xs, ffn1_w1, ffn1_w3, ffn1_w2, vec(ln1_g), vec(ln1_b), alpha, (n_prompt + n_sample,))
    w, wif = _prep_w_in(jnp.swapaxes(w_in, 1, 2), layer, 2 * hqk + 2 * dm, 2 * n_heads, 4 * d)
    qk, v, ga, gb, yc, iff, tile_tails, conv_s_t = _proj(x1, w, wif, conv_w.astype(F32), jnp.swapaxes(cache, 0, 1),
                                                         hqk, float(d_qk) ** -0.5, batch, seq, t_dec)

    gate_bias = jnp.pad(jnp.concatenate([b_igate, b_fgate]).astype(F32), (0, LANES - 2 * n_heads)).reshape(1, LANES)
    mh_g = vec(mh_norm_g)
    hn_p, c_p, n_p, m_p = _mlstm_prompt(qk, v, iff, gate_bias, mh_g, batch, seq, n_heads, d_qk, d_v)
    hn_s, c_s, n_s, m_s = _mlstm_sample(qk, v, iff, gate_bias, mh_g, c0,
                                        n0.reshape(dec_batch, n_heads, 1, d_qk),
                                        m0.reshape(dec_batch, n_heads, 1, 1), n_prompt, t_dec)

    x2 = _mix_ln(yc, ga, gb, hn_p, hn_s, x1, vec(conv_b), vec(conv_ln_g), vec(conv_ln_b), bf(w_out),
                 vec(ln2_g), vec(ln2_b), alpha)
    y_p, y_s = _ffn_ln((x2,), ffn2_w1, ffn2_w3, ffn2_w2, vec(ln3_g), vec(ln3_b), alpha, (n_prompt, n_sample))
    tiles_per_seq = tile_tails.shape[0] // batch
    conv_p = tile_tails[tiles_per_seq - 1::tiles_per_seq]
    states_p = (c_p, n_p.reshape(batch, n_heads, d_qk), m_p.reshape(batch, n_heads), conv_p)
    states_s = (c_s, n_s.reshape(dec_batch, n_heads, d_qk), m_s.reshape(dec_batch, n_heads),
                jnp.swapaxes(conv_s_t, 0, 1))
    return (y_p, y_s), states_p, states_s


def kernel(x_prompt, x_sample, state_C, state_n, state_m, cache_conv, ffn1_w1, ffn1_w3, ffn1_w2, ln1_g, ln1_b, w_in, b_igate, b_fgate, mh_norm_g, conv_w, conv_b, conv_ln_g, conv_ln_b, w_out, ln2_g, ln2_b, ffn2_w1, ffn2_w3, ffn2_w2, ln3_g, ln3_b):
    batch, seq, d = x_prompt.shape
    dec_batch, t_dec, _ = x_sample.shape
    depth = ffn1_w1.shape[0]
    alpha = (2.0 * depth) ** 0.25
    xs = (x_prompt.reshape(batch * seq, d), x_sample.reshape(dec_batch * t_dec, d))
    weights = (ffn1_w1, ffn1_w3, ffn1_w2, ln1_g, ln1_b, b_igate, b_fgate, mh_norm_g,
               conv_w, conv_b, conv_ln_g, conv_ln_b, w_out, ln2_g, ln2_b,
               ffn2_w1, ffn2_w3, ffn2_w2, ln3_g, ln3_b)
    outs_p, outs_s = [], []
    for l in range(depth):
        p = tuple(wt[l] for wt in weights)
        xs, st_p, st_s = _layer(xs, l, batch, seq, dec_batch, t_dec, state_C[l], state_n[l], state_m[l],
                                cache_conv[l], w_in, p, alpha)
        outs_p.append(st_p)
        outs_s.append(st_s)
    y_p = xs[0].reshape(batch, seq, d)
    y_s = xs[1].reshape(dec_batch, t_dec, d)
    stack = lambda outs, i: jnp.stack([o[i] for o in outs])
    return (y_p, y_s,
            stack(outs_p, 0), stack(outs_p, 1), stack(outs_p, 2), stack(outs_p, 3),
            stack(outs_s, 0), stack(outs_s, 1), stack(outs_s, 2), stack(outs_s, 3))
```

```python
import functools

import jax
import jax.numpy as jnp
from jax import lax
from jax.experimental import pallas as pl
from jax.experimental.pallas import tpu as pltpu

F32 = jnp.float32
BF16 = jnp.bfloat16
LN_EPS = 1e-5
LANES = 128
SUBLANES = 8
HALO = 32
VMEM_LIMIT = 56 * 1024 * 1024

TM_FFN = 512
TF_FFN = 512
TF_FFN_FIRST = 256
TM_PROJ = 512
TN_PROJ = 256
TW_PREP = 512
TR_PREP = 512
TM_MIX = 256
CONV_ROWS = 64
BS_MLSTM = 4
CHUNK = 128


def _params(sem):
    return pltpu.CompilerParams(dimension_semantics=sem, vmem_limit_bytes=VMEM_LIMIT)


def _sigmoid(x):
    return jax.nn.sigmoid(x)


def _layer_norm(z, g, b):
    mu = jnp.mean(z, axis=-1, keepdims=True)
    zc = z - mu
    var = jnp.mean(zc * zc, axis=-1, keepdims=True)
    return zc * lax.rsqrt(var + LN_EPS) * g + b


def _dot(a, b):
    return jnp.dot(a, b, preferred_element_type=F32)


def _for_owner(refs, split, i, fn):
    if len(refs) == 1:
        fn(refs[0])
        return
    pl.when(i < split)(lambda: fn(refs[0]))
    pl.when(i >= split)(lambda: fn(refs[1]))


def _segment_specs(arrays_rows, tm, d, tile0=0):
    if len(arrays_rows) == 1:
        return [pl.BlockSpec((tm, d), lambda i, *_: (i + tile0, 0))], 0
    split = arrays_rows[0] // tm
    last0 = split - 1
    return [pl.BlockSpec((tm, d), lambda i, *_: (jnp.minimum(i + tile0, last0), 0)),
            pl.BlockSpec((tm, d), lambda i, *_: (jnp.maximum(i + tile0 - split, 0), 0))], split


def _ffn_ln_kernel(*refs, alpha, n_in, n_out, in_split, out_split, tile0, n_pass, emit_w):
    x_refs = refs[:n_in]
    w1_ref, w3_ref, w2_ref, g_ref, b_ref = refs[n_in:n_in + 5]
    pos = n_in + 5 + n_pass
    o_refs = refs[pos:pos + n_out]
    pos += n_out
    wb_refs = refs[pos:pos + 3] if emit_w else ()
    xb_ref, acc_ref = refs[pos + len(wb_refs):]
    i = pl.program_id(0) + tile0
    f = pl.program_id(1)

    @pl.when(f == 0)
    def _():
        def cast(x_ref):
            xb_ref[...] = x_ref[...].astype(BF16)
        _for_owner(x_refs, in_split, i, cast)
        acc_ref[...] = jnp.zeros_like(acc_ref)

    w1, w3, w2 = (r[...].astype(BF16) for r in (w1_ref, w3_ref, w2_ref))
    for wb_ref, w in zip(wb_refs, (w1, w3, w2)):
        wb_ref[...] = w
    xb = xb_ref[...]
    a = _dot(xb, w1)
    c = _dot(xb, w3)
    h = (a * _sigmoid(a) * c).astype(BF16)
    acc_ref[...] += _dot(h, w2)

    @pl.when(f == pl.num_programs(1) - 1)
    def _():
        def finish(x_ref):
            z = alpha * x_ref[...] + 0.5 * acc_ref[...]
            acc_ref[...] = _layer_norm(z, g_ref[...], b_ref[...])
        _for_owner(x_refs, in_split, i, finish)

        def emit(o_ref):
            o_ref[...] = acc_ref[...]
        _for_owner(o_refs, out_split, i, emit)


def _ffn_ln(xs, w1, w3, w2, g, b, alpha, out_rows):
    d = xs[0].shape[1]
    n = sum(x.shape[0] for x in xs)
    dff = w1.shape[1]
    tm, tf, tf0 = min(TM_FFN, n), min(TF_FFN, dff), min(TF_FFN_FIRST, dff)
    assert sum(out_rows) == n and dff % tf == 0 and dff % tf0 == 0 and n // tm >= 2
    assert all(x.shape[0] % tm == 0 for x in xs) and all(r % tm == 0 for r in out_rows)
    vec = pl.BlockSpec((1, d), lambda i, f: (0, 0))
    w_specs = lambda t: [pl.BlockSpec((d, t), lambda i, f: (0, f)), pl.BlockSpec((d, t), lambda i, f: (0, f)),
                         pl.BlockSpec((t, d), lambda i, f: (f, 0))]
    scratch = [pltpu.VMEM((tm, d), BF16), pltpu.VMEM((tm, d), F32)]
    kern = functools.partial(_ffn_ln_kernel, alpha=alpha)

    tile = pl.BlockSpec((tm, d), lambda i, f: (0, 0))
    o0, w1b, w3b, w2b = pl.pallas_call(
        functools.partial(kern, n_in=1, n_out=1, in_split=0, out_split=0, tile0=0, n_pass=0, emit_w=True),
        out_shape=(jax.ShapeDtypeStruct((out_rows[0], d), F32),
                   *(jax.ShapeDtypeStruct(w.shape, BF16) for w in (w1, w3, w2))),
        grid=(1, dff // tf0),
        in_specs=[tile] + w_specs(tf0) + [vec, vec],
        out_specs=(tile, *w_specs(tf0)),
        scratch_shapes=scratch,
        compiler_params=_params(("arbitrary", "arbitrary")),
        name="ffn_ln_first",
    )(xs[0], w1, w3, w2, g, b)

    in_specs, in_split = _segment_specs([x.shape[0] for x in xs], tm, d, tile0=1)
    out_specs, out_split = _segment_specs(list(out_rows), tm, d, tile0=1)
    return pl.pallas_call(
        functools.partial(kern, n_in=len(xs), n_out=len(out_rows), in_split=in_split, out_split=out_split,
                          tile0=1, n_pass=1, emit_w=False),
        out_shape=tuple(jax.ShapeDtypeStruct((r, d), F32) for r in out_rows),
        grid=(n // tm - 1, dff // tf),
        in_specs=in_specs + w_specs(tf) + [vec, vec, pl.BlockSpec(memory_space=pl.ANY)],
        out_specs=tuple(out_specs),
        input_output_aliases={len(xs) + 5: 0},
        scratch_shapes=scratch,
        compiler_params=_params(("arbitrary", "arbitrary")),
        name="ffn_ln",
    )(*xs, w1b, w3b, w2b, g, b, o0)


def _prep_w_in_kernel(a_ref, nb_ref, if_ref, w_ref, wif_ref, *, n_aligned, n_gate):
    j = pl.program_id(0)
    tw, k_rows = a_ref.shape
    tr = min(TR_PREP, k_rows)

    def emit(rows_of):
        for c0 in range(0, k_rows, tr):
            w_ref[c0:c0 + tr, :] = rows_of(c0).T.astype(BF16)

    @pl.when(j < n_aligned)
    def _():
        emit(lambda c0: a_ref[:, c0:c0 + tr])

    @pl.when(j >= n_aligned)
    def _():
        emit(lambda c0: jnp.concatenate([a_ref[n_gate:tw, c0:c0 + tr], nb_ref[:, c0:c0 + tr]], axis=0))

    @pl.when(j == 0)
    def _():
        for c0 in range(0, k_rows, tr):
            rows = jnp.concatenate([if_ref[:, c0:c0 + tr], jnp.zeros((LANES - n_gate, tr), F32)], axis=0)
            wif_ref[c0:c0 + tr, :] = rows.T.astype(BF16)


def _prep_w_in(w_in_t, layer, n_aligned_cols, n_gate, n_shifted_cols):
    _, n_in, k_rows = w_in_t.shape
    tw = TW_PREP
    assert n_aligned_cols % tw == 0 and n_shifted_cols % tw == 0 and n_gate == SUBLANES
    assert k_rows % min(TR_PREP, k_rows) == 0 and n_aligned_cols + n_gate + n_shifted_cols == n_in
    n_aligned = n_aligned_cols // tw
    n_tiles = n_aligned + n_shifted_cols // tw
    per = tw // n_gate
    return pl.pallas_call(
        functools.partial(_prep_w_in_kernel, n_aligned=n_aligned, n_gate=n_gate),
        out_shape=(jax.ShapeDtypeStruct((k_rows, n_tiles * tw), BF16),
                   jax.ShapeDtypeStruct((k_rows, LANES), BF16)),
        grid=(n_tiles,),
        in_specs=[
            pl.BlockSpec((None, tw, k_rows), lambda j: (layer, j, 0)),
            pl.BlockSpec((None, n_gate, k_rows), lambda j: (layer, (j + 1) * per, 0)),
            pl.BlockSpec((None, n_gate, k_rows), lambda j: (layer, n_aligned * per, 0)),
        ],
        out_specs=(pl.BlockSpec((k_rows, tw), lambda j: (0, j)),
                   pl.BlockSpec((k_rows, LANES), lambda j: (0, 0))),
        compiler_params=_params(("arbitrary",)),
        name="prep_w_in",
    )(w_in_t, w_in_t, w_in_t)


def _conv_tile(ubuf_ref, w_ref, y_ref, u0, out0, rows, width, cs):
    y = None
    for r in range(min(SUBLANES, width)):
        q = None
        for a in range((width - 1 - r) // SUBLANES + 1):
            k = width - 1 - (SUBLANES * a + r)
            term = w_ref[k:k + 1, cs] * ubuf_ref[pl.ds(u0 - SUBLANES * (a + 1), rows + SUBLANES), cs]
            q = term if q is None else q + term
        part = q[SUBLANES - r:SUBLANES - r + rows]
        y = part if y is None else y + part
    y_ref[pl.ds(out0, rows), cs] = y


def _proj_kernel(x_ref, wqk_ref, wv_ref, wo_ref, wla_ref, wlb_ref, wga_ref, wgb_ref, wif_ref, cw_ref, cache_ref,
                 qk_ref, v_ref, ga_ref, gb_ref, yc_ref, if_ref, csp_ref, css_ref,
                 xb_ref, ubuf_ref, carry_ref, tbuf_ref, ybuf_ref, *, k_col0, k_scale, n_prompt_tiles, tiles_per_seq, width, t_dec):
    i = pl.program_id(0)
    j = pl.program_id(1)
    tm, tn = yc_ref.shape
    hist = width - 1

    @pl.when(j == 0)
    def _():
        xb0 = x_ref[...].astype(BF16)
        xb_ref[...] = xb0
        if_ref[...] = _dot(xb0, wif_ref[...])

    def glu(xb):
        return _dot(xb, wla_ref[...]) * _sigmoid(_dot(xb, wlb_ref[...]))

    def dense(xb):
        qk_ref[...] = (_dot(xb, wqk_ref[...]) * jnp.where(j * tn >= k_col0, k_scale, 1.0)).astype(BF16)
        v_ref[...] = _dot(xb, wv_ref[...]).astype(BF16)
        ga_ref[...] = (_sigmoid(_dot(xb, wo_ref[...])) * _sigmoid(_dot(xb, wga_ref[...]))).astype(BF16)
        gb_ref[...] = _sigmoid(_dot(xb, wgb_ref[...])).astype(BF16)

    @pl.when(i < n_prompt_tiles)
    def _():
        xb = xb_ref[...]
        u = glu(xb)
        first = (i % tiles_per_seq) == 0
        ubuf_ref[0:HALO, :] = jnp.where(first, 0.0, carry_ref[j])
        ubuf_ref[HALO:HALO + tm, :] = u
        carry_ref[j] = u[tm - HALO:tm]
        csp_ref[...] = u[tm - hist:tm]
        strip = min(CONV_ROWS, tm)
        for c0 in range(0, tn, LANES):
            for r0 in range(0, tm, strip):
                _conv_tile(ubuf_ref, cw_ref, yc_ref, HALO + r0, r0, strip, width, slice(c0, c0 + LANES))
        dense(xb)

    @pl.when(i >= n_prompt_tiles)
    def _():
        xb = xb_ref[...]
        nb = tm // t_dec
        u = glu(xb)
        for c in range(tn // LANES):
            cs = slice(c * LANES, (c + 1) * LANES)
            tbuf_ref[c] = u[:, cs]
            new = [tbuf_ref[c, pl.ds(t, nb, stride=t_dec), :] for t in range(t_dec)]
            row = lambda p: cache_ref[p, :, cs] if p < hist else new[p - hist]
            for t in range(t_dec):
                acc = cw_ref[0:1, cs] * row(t)
                for k in range(1, width):
                    acc = acc + cw_ref[k:k + 1, cs] * row(t + k)
                ybuf_ref[c, pl.ds(t, nb, stride=t_dec), :] = acc
            yc_ref[:, cs] = ybuf_ref[c]
            for p in range(hist):
                css_ref[p, :, cs] = row(p + t_dec)
        dense(xb)


def _proj(x1, w, wif, cw, cache_t, hqk, k_scale, batch, seq, t_dec):
    n, d = x1.shape
    hist, dec_batch, _ = cache_t.shape
    width = cw.shape[0]
    tm, tn = min(TM_PROJ, seq), min(TN_PROJ, d)
    n_prompt = batch * seq
    assert n_prompt % tm == 0 and (n - n_prompt) % tm == 0 and seq % tm == 0 and d % tn == 0
    assert hqk % tn == 0 and 2 * hqk == d and tn % LANES == 0
    assert hist == width - 1 and SUBLANES * (hist // SUBLANES + 1) <= HALO <= tm and tm % CONV_ROWS == 0
    assert n - n_prompt == dec_batch * t_dec and tm % t_dec == 0 and (tm // t_dec) % SUBLANES == 0 and t_dec <= hist
    npt, tps, nb = n_prompt // tm, seq // tm, tm // t_dec
    per = d // tn
    col = pl.BlockSpec((tm, tn), lambda i, j: (i, j))
    wspec = lambda g: pl.BlockSpec((d, tn), lambda i, j: (0, g * per + j))
    smp_i = lambda i: jnp.maximum(i - npt, 0)
    smp_j = lambda i, j: jnp.where(i >= npt, j, 0)
    prm_i = lambda i: jnp.minimum(i, npt - 1)
    prm_j = lambda i, j: jnp.where(i < npt, j, per - 1)
    return pl.pallas_call(
        functools.partial(_proj_kernel, k_col0=hqk, k_scale=k_scale, n_prompt_tiles=npt, tiles_per_seq=tps,
                          width=width, t_dec=t_dec),
        out_shape=(
            jax.ShapeDtypeStruct((n, d), BF16),
            jax.ShapeDtypeStruct((n, d), BF16),
            jax.ShapeDtypeStruct((n, d), BF16),
            jax.ShapeDtypeStruct((n, d), BF16),
            jax.ShapeDtypeStruct((n, d), F32),
            jax.ShapeDtypeStruct((n, LANES), F32),
            jax.ShapeDtypeStruct((npt, hist, d), F32),
            jax.ShapeDtypeStruct((hist, dec_batch, d), F32),
        ),
        grid=(n // tm, d // tn),
        in_specs=[pl.BlockSpec((tm, d), lambda i, j: (i, 0)),
                  wspec(0), wspec(1), wspec(2), wspec(3), wspec(4), wspec(5), wspec(6),
                  pl.BlockSpec((d, LANES), lambda i, j: (0, 0)),
                  pl.BlockSpec((width, tn), lambda i, j: (0, j)),
                  pl.BlockSpec((hist, nb, tn), lambda i, j: (0, smp_i(i), smp_j(i, j)))],
        out_specs=(col, col, col, col, col, pl.BlockSpec((tm, LANES), lambda i, j: (i, 0)),
                   pl.BlockSpec((None, hist, tn), lambda i, j: (prm_i(i), 0, prm_j(i, j))),
                   pl.BlockSpec((hist, nb, tn), lambda i, j: (0, smp_i(i), smp_j(i, j)))),
        scratch_shapes=[pltpu.VMEM((tm, d), BF16), pltpu.VMEM((HALO + tm, tn), F32),
                        pltpu.VMEM((per, HALO, tn), F32),
                        pltpu.VMEM((tn // LANES, tm, LANES), F32), pltpu.VMEM((tn // LANES, tm, LANES), F32)],
        compiler_params=_params(("arbitrary", "arbitrary")),
        name="proj",
    )(x1, w, w, w, w, w, w, w, wif, cw, cache_t)


def _mlstm_chunks(q, k, v, ig_c, lf_c, c0, n0, m0, last):
    G, L, _ = q.shape
    row = lax.broadcasted_iota(jnp.int32, (L, L), 0)
    colm = lax.broadcasted_iota(jnp.int32, (L, L), 1)
    causal = (colm <= row)[None]
    eye = (colm == row)[None]
    upper = (row <= colm)[None]
    b_r = jnp.sum(jnp.where(upper, lf_c, 0.0), axis=1, keepdims=True)
    b_c = jnp.sum(jnp.where(eye, b_r, 0.0), axis=2, keepdims=True)
    ig_r = jnp.sum(jnp.where(eye, ig_c, 0.0), axis=1, keepdims=True)

    log_d = jnp.where(causal, b_c - b_r + ig_r, -jnp.inf)
    inter = b_c + m0
    m_t = jnp.maximum(inter, jnp.max(log_d, axis=2, keepdims=True))
    d = jnp.exp(log_d - m_t)
    w_inter = jnp.exp(inter - m_t)
    per_group = lambda fn: jnp.stack([fn(g) for g in range(G)])
    nt = (((1,), (1,)), ((), ()))
    tn = (((0,), (0,)), ((), ()))
    s = per_group(lambda g: lax.dot_general(q[g], k[g], nt, preferred_element_type=F32)) * d
    sb = s.astype(BF16)
    cb = c0.astype(BF16)
    num = per_group(lambda g: _dot(sb[g], v[g])) + w_inter * per_group(lambda g: _dot(q[g], cb[g]))
    qn = jnp.sum(s, axis=2, keepdims=True) + w_inter * jnp.sum(q.astype(F32) * n0, axis=2, keepdims=True)
    den = jnp.maximum(jnp.abs(qn), jnp.exp(-m_t))
    h = num * (1.0 / den)

    m_new = m_t[:, last:last + 1, :]
    w_k = jnp.exp(b_c[:, last:last + 1, :] - b_c + ig_c - m_new)
    w_c = jnp.exp(inter[:, last:last + 1, :] - m_new)
    kw = k.astype(F32) * w_k
    kwb = kw.astype(BF16)
    c_new = w_c * c0 + per_group(lambda g: lax.dot_general(kwb[g], v[g], tn, preferred_element_type=F32))
    n_new = w_c * n0 + jnp.sum(kw, axis=1, keepdims=True)
    return h, c_new, n_new, m_new


def _head_norm(h, g):
    mu = jnp.mean(h, axis=-1, keepdims=True)
    hc = h - mu
    var = jnp.mean(hc * hc, axis=-1, keepdims=True)
    return hc * lax.rsqrt(var + LN_EPS) * g


def _log_sigmoid(x):
    return jnp.minimum(x, 0.0) - jnp.log1p(jnp.exp(-jnp.abs(x)))


def _mlstm_prompt_kernel(qk_ref, v_ref, if_ref, bias_ref, g_ref, hn_ref, c_ref, n_ref, m_ref,
                         *, n_heads, d_qk, d_v):
    c = pl.program_id(1)

    @pl.when(c == 0)
    def _():
        c_ref[...] = jnp.zeros_like(c_ref)
        n_ref[...] = jnp.zeros_like(n_ref)
        m_ref[...] = jnp.zeros_like(m_ref)

    gates = if_ref[...] + bias_ref[...]
    lf_all = _log_sigmoid(gates)
    L = gates.shape[0]
    heads = range(n_heads)
    q = jnp.stack([qk_ref[:, h * d_qk:(h + 1) * d_qk] for h in heads])
    k = jnp.stack([qk_ref[:, (n_heads + h) * d_qk:(n_heads + h + 1) * d_qk] for h in heads])
    v = jnp.stack([v_ref[:, h * d_v:(h + 1) * d_v] for h in heads])
    ig_c = jnp.stack([gates[:, h:h + 1] for h in heads])
    lf_c = jnp.stack([lf_all[:, n_heads + h:n_heads + h + 1] for h in heads])
    hh, c_new, n_new, m_new = _mlstm_chunks(q, k, v, ig_c, lf_c, c_ref[0], n_ref[0], m_ref[0], L - 1)
    c_ref[0] = c_new
    n_ref[0] = n_new
    m_ref[0] = m_new
    g = jnp.stack([g_ref[:, h * d_v:(h + 1) * d_v] for h in heads])
    hn = _head_norm(hh, g).astype(BF16)
    for h in heads:
        hn_ref[:, h * d_v:(h + 1) * d_v] = hn[h]


def _mlstm_prompt(qk, v, iff, bias, g, batch, seq, n_heads, d_qk, d_v):
    d = v.shape[1]
    L = min(CHUNK, seq)
    assert seq % L == 0
    nc = seq // L
    tok = lambda b, c: (b * nc + c, 0)
    return pl.pallas_call(
        functools.partial(_mlstm_prompt_kernel, n_heads=n_heads, d_qk=d_qk, d_v=d_v),
        out_shape=(
            jax.ShapeDtypeStruct((batch * seq, d), BF16),
            jax.ShapeDtypeStruct((batch, n_heads, d_qk, d_v), F32),
            jax.ShapeDtypeStruct((batch, n_heads, 1, d_qk), F32),
            jax.ShapeDtypeStruct((batch, n_heads, 1, 1), F32),
        ),
        grid=(batch, nc),
        in_specs=[
            pl.BlockSpec((L, qk.shape[1]), tok),
            pl.BlockSpec((L, d), tok),
            pl.BlockSpec((L, LANES), tok),
            pl.BlockSpec((1, LANES), lambda b, c: (0, 0)),
            pl.BlockSpec((1, d), lambda b, c: (0, 0)),
        ],
        out_specs=(
            pl.BlockSpec((L, d), tok),
            pl.BlockSpec((1, n_heads, d_qk, d_v), lambda b, c: (b, 0, 0, 0)),
            pl.BlockSpec((1, n_heads, 1, d_qk), lambda b, c: (b, 0, 0, 0)),
            pl.BlockSpec((1, n_heads, 1, 1), lambda b, c: (b, 0, 0, 0)),
        ),
        compiler_params=_params(("parallel", "arbitrary")),
        name="mlstm_prompt",
    )(qk, v, iff, bias, g)


def _mlstm_sample_kernel(qk_ref, v_ref, if_ref, bias_ref, g_ref, c0_ref, n0_ref, m0_ref,
                         hn_ref, c_ref, n_ref, m_ref, *, n_heads, d_qk, d_v, t_dec, bs):
    pad = (-t_dec) % (2 * SUBLANES)

    def zpad_rows(x):
        x = x.reshape(bs, t_dec, x.shape[1])
        return jnp.concatenate([x, jnp.zeros((bs, pad, x.shape[2]), x.dtype)], axis=1) if pad else x

    qk_all = zpad_rows(qk_ref[...].astype(F32))
    v_all = zpad_rows(v_ref[...].astype(F32))
    gates = zpad_rows(if_ref[...] + bias_ref[...])
    lf_all = _log_sigmoid(gates)
    groups = [(b, h) for b in range(bs) for h in range(n_heads)]
    q = jnp.stack([qk_all[b][:, h * d_qk:(h + 1) * d_qk] for b, h in groups]).astype(BF16)
    k = jnp.stack([qk_all[b][:, (n_heads + h) * d_qk:(n_heads + h + 1) * d_qk] for b, h in groups]).astype(BF16)
    v = jnp.stack([v_all[b][:, h * d_v:(h + 1) * d_v] for b, h in groups]).astype(BF16)
    ig_c = jnp.stack([gates[b][:, h:h + 1] for b, h in groups])
    lf_c = jnp.stack([lf_all[b][:, n_heads + h:n_heads + h + 1] for b, h in groups])
    flat = lambda r: r[...].reshape((bs * n_heads,) + r.shape[2:])
    hh, c_new, n_new, m_new = _mlstm_chunks(q, k, v, ig_c, lf_c, flat(c0_ref), flat(n0_ref), flat(m0_ref), t_dec - 1)
    c_ref[...] = c_new.reshape(c_ref.shape)
    n_ref[...] = n_new.reshape(n_ref.shape)
    m_ref[...] = m_new.reshape(m_ref.shape)
    g = jnp.stack([g_ref[:, h * d_v:(h + 1) * d_v] for _, h in groups])
    hn = _head_norm(hh[:, :t_dec], g)
    for i, (b, h) in enumerate(groups):
        hn_ref[b * t_dec:(b + 1) * t_dec, h * d_v:(h + 1) * d_v] = hn[i]


def _mlstm_sample(qk, v, iff, bias, g, c0, n0, m0, row0, t_dec):
    batch, n_heads, d_qk, d_v = c0.shape
    d = v.shape[1]
    bs = min(BS_MLSTM, batch)
    assert batch % bs == 0 and row0 % (bs * t_dec) == 0 and (bs * t_dec) % SUBLANES == 0
    blk0 = row0 // (bs * t_dec)
    tok = lambda i: (blk0 + i, 0)
    st4 = lambda i: (i, 0, 0, 0)
    return pl.pallas_call(
        functools.partial(_mlstm_sample_kernel, n_heads=n_heads, d_qk=d_qk, d_v=d_v, t_dec=t_dec, bs=bs),
        out_shape=(
            jax.ShapeDtypeStruct((batch * t_dec, d), F32),
            jax.ShapeDtypeStruct((batch, n_heads, d_qk, d_v), F32),
            jax.ShapeDtypeStruct((batch, n_heads, 1, d_qk), F32),
            jax.ShapeDtypeStruct((batch, n_heads, 1, 1), F32),
        ),
        grid=(batch // bs,),
        in_specs=[
            pl.BlockSpec((bs * t_dec, qk.shape[1]), tok),
            pl.BlockSpec((bs * t_dec, d), tok),
            pl.BlockSpec((bs * t_dec, LANES), tok),
            pl.BlockSpec((1, LANES), lambda i: (0, 0)),
            pl.BlockSpec((1, d), lambda i: (0, 0)),
            pl.BlockSpec((bs, n_heads, d_qk, d_v), st4),
            pl.BlockSpec((bs, n_heads, 1, d_qk), st4),
            pl.BlockSpec((bs, n_heads, 1, 1), st4),
        ],
        out_specs=(
            pl.BlockSpec((bs * t_dec, d), lambda i: (i, 0)),
            pl.BlockSpec((bs, n_heads, d_qk, d_v), st4),
            pl.BlockSpec((bs, n_heads, 1, d_qk), st4),
            pl.BlockSpec((bs, n_heads, 1, 1), st4),
        ),
        compiler_params=_params(("parallel",)),
        name="mlstm_sample",
    )(qk, v, iff, bias, g, c0, n0, m0)


def _mix_ln_kernel(yc_ref, ga_ref, gb_ref, hnp_ref, hns_ref, x1_ref, cb_ref, cg_ref, cbb_ref, w_ref, g_ref, b_ref,
                   o_ref, *, alpha, split):
    def body(hn_ref):
        z = _layer_norm(yc_ref[...] + cb_ref[...], cg_ref[...], cbb_ref[...])
        hb = z * _sigmoid(z)
        mixin = ga_ref[...].astype(F32) * hn_ref[...].astype(F32) + gb_ref[...].astype(F32) * hb
        mix = _dot(mixin.astype(BF16), w_ref[...])
        o_ref[...] = _layer_norm(alpha * x1_ref[...] + mix, g_ref[...], b_ref[...])

    _for_owner((hnp_ref, hns_ref), split, pl.program_id(0), body)


def _mix_ln(yc, ga, gb, hn_p, hn_s, x1, cb, cg, cbb, w, g, b, alpha):
    n, d = x1.shape
    tm = min(TM_MIX, hn_s.shape[0])
    assert hn_p.shape[0] % tm == 0 and hn_s.shape[0] % tm == 0 and hn_p.shape[0] + hn_s.shape[0] == n
    tok = pl.BlockSpec((tm, d), lambda i: (i, 0))
    vec = pl.BlockSpec((1, d), lambda i: (0, 0))
    hn_specs, split = _segment_specs([hn_p.shape[0], hn_s.shape[0]], tm, d)
    return pl.pallas_call(
        functools.partial(_mix_ln_kernel, alpha=alpha, split=split),
        out_shape=jax.ShapeDtypeStruct((n, d), F32),
        grid=(n // tm,),
        in_specs=[tok, tok, tok, *hn_specs, tok, vec, vec, vec, pl.BlockSpec((d, d), lambda i: (0, 0)), vec, vec],
        out_specs=tok,
        compiler_params=_params(("arbitrary",)),
        name="mix_ln",
    )(yc, ga, gb, hn_p, hn_s, x1, cb, cg, cbb, w, g, b)


def _layer(xs, layer, batch, seq, dec_batch, t_dec, c0, n0, m0, cache, w_in, p, alpha):
    (ffn1_w1, ffn1_w3, ffn1_w2, ln1_g, ln1_b, b_igate, b_fgate, mh_norm_g,
     conv_w, conv_b, conv_ln_g, conv_ln_b, w_out, ln2_g, ln2_b,
     ffn2_w1, ffn2_w3, ffn2_w2, ln3_g, ln3_b) = p
    d = xs[0].shape[1]
    n_heads, d_qk, d_v = c0.shape[1:]
    n_prompt, n_sample = batch * seq, dec_batch * t_dec
    hqk, dm = n_heads * d_qk, n_heads * d_v
    assert 2 * hqk == d and dm == d and conv_w.shape[1] == d
    vec = lambda a: a.reshape(1, -1).astype(F32)
    bf = lambda a: a.astype(BF16)

    (x1,) = _ffn_ln(xs, ffn1_w1, ffn1_w3, ffn1_w2, vec(ln1_g), vec(ln1_b), alpha, (n_prompt + n_sample,))
    w, wif = _prep_w_in(jnp.swapaxes(w_in, 1, 2), layer, 2 * hqk + 2 * dm, 2 * n_heads, 4 * d)
    qk, v, ga, gb, yc, iff, tile_tails, conv_s_t = _proj(x1, w, wif, conv_w.astype(F32), jnp.swapaxes(cache, 0, 1),
                                                         hqk, float(d_qk) ** -0.5, batch, seq, t_dec)

    gate_bias = jnp.pad(jnp.concatenate([b_igate, b_fgate]).astype(F32), (0, LANES - 2 * n_heads)).reshape(1, LANES)
    mh_g = vec(mh_norm_g)
    hn_p, c_p, n_p, m_p = _mlstm_prompt(qk, v, iff, gate_bias, mh_g, batch, seq, n_heads, d_qk, d_v)
    hn_s, c_s, n_s, m_s = _mlstm_sample(qk, v, iff, gate_bias, mh_g, c0,
                                        n0.reshape(dec_batch, n_heads, 1, d_qk),
                                        m0.reshape(dec_batch, n_heads, 1, 1), n_prompt, t_dec)

    x2 = _mix_ln(yc, ga, gb, hn_p, hn_s, x1, vec(conv_b), vec(conv_ln_g), vec(conv_ln_b), bf(w_out),
                 vec(ln2_g), vec(ln2_b), alpha)
    y_p, y_s = _ffn_ln((x2,), ffn2_w1, ffn2_w3, ffn2_w2, vec(ln3_g), vec(ln3_b), alpha, (n_prompt, n_sample))
    tiles_per_seq = tile_tails.shape[0] // batch
    conv_p = tile_tails[tiles_per_seq - 1::tiles_per_seq]
    states_p = (c_p, n_p.reshape(batch, n_heads, d_qk), m_p.reshape(batch, n_heads), conv_p)
    states_s = (c_s, n_s.reshape(dec_batch, n_heads, d_qk), m_s.reshape(dec_batch, n_heads),
                jnp.swapaxes(conv_s_t, 0, 1))
    return (y_p, y_s), states_p, states_s


def kernel(x_prompt, x_sample, state_C, state_n, state_m, cache_conv, ffn1_w1, ffn1_w3, ffn1_w2, ln1_g, ln1_b, w_in, b_igate, b_fgate, mh_norm_g, conv_w, conv_b, conv_ln_g, conv_ln_b, w_out, ln2_g, ln2_b, ffn2_w1, ffn2_w3, ffn2_w2, ln3_g, ln3_b):
    batch, seq, d = x_prompt.shape
    dec_batch, t_dec, _ = x_sample.shape
    depth = ffn1_w1.shape[0]
    alpha = (2.0 * depth) ** 0.25
    xs = (x_prompt.reshape(batch * seq, d), x_sample.reshape(dec_batch * t_dec, d))
    weights = (ffn1_w1, ffn1_w3, ffn1_w2, ln1_g, ln1_b, b_igate, b_fgate, mh_norm_g,
               conv_w, conv_b, conv_ln_g, conv_ln_b, w_out, ln2_g, ln2_b,
               ffn2_w1, ffn2_w3, ffn2_w2, ln3_g, ln3_b)
    outs_p, outs_s = [], []
    for l in range(depth):
        p = tuple(wt[l] for wt in weights)
        xs, st_p, st_s = _layer(xs, l, batch, seq, dec_batch, t_dec, state_C[l], state_n[l], state_m[l],
                                cache_conv[l], w_in, p, alpha)
        outs_p.append(st_p)
        outs_s.append(st_s)
    y_p = xs[0].reshape(batch, seq, d)
    y_s = xs[1].reshape(dec_batch, t_dec, d)
    stack = lambda outs, i: jnp.stack([o[i] for o in outs])
    return (y_p, y_s,
            stack(outs_p, 0), stack(outs_p, 1), stack(outs_p, 2), stack(outs_p, 3),
            stack(outs_s, 0), stack(outs_s, 1), stack(outs_s, 2), stack(outs_s, 3))
```

```python
import functools

import jax
import jax.numpy as jnp
from jax import lax
from jax.experimental import pallas as pl
from jax.experimental.pallas import tpu as pltpu

F32 = jnp.float32
BF16 = jnp.bfloat16
LN_EPS = 1e-5
LANES = 128
SUBLANES = 8
HALO = 32
VMEM_LIMIT = 56 * 1024 * 1024

TM_FFN = 512
TF_FFN = 512
TF_FFN_FIRST = 256
TM_PROJ = 512
TN_PROJ = 256
TW_PREP = 512
TR_PREP = 512
TM_MIX = 256
CONV_ROWS = 64
BS_MLSTM = 4
CHUNK = 128


def _params(sem):
    return pltpu.CompilerParams(dimension_semantics=sem, vmem_limit_bytes=VMEM_LIMIT)


def _sigmoid(x):
    return jax.nn.sigmoid(x)


def _layer_norm(z, g, b):
    mu = jnp.mean(z, axis=-1, keepdims=True)
    zc = z - mu
    var = jnp.mean(zc * zc, axis=-1, keepdims=True)
    return zc * lax.rsqrt(var + LN_EPS) * g + b


def _dot(a, b):
    return jnp.dot(a, b, preferred_element_type=F32)


def _for_owner(refs, split, i, fn):
    if len(refs) == 1:
        fn(refs[0])
        return
    pl.when(i < split)(lambda: fn(refs[0]))
    pl.when(i >= split)(lambda: fn(refs[1]))


def _segment_specs(arrays_rows, tm, d):
    if len(arrays_rows) == 1:
        return [pl.BlockSpec((tm, d), lambda i, *_: (i, 0))], 0
    split = arrays_rows[0] // tm
    last0 = split - 1
    return [pl.BlockSpec((tm, d), lambda i, *_: (jnp.minimum(i, last0), 0)),
            pl.BlockSpec((tm, d), lambda i, *_: (jnp.maximum(i - split, 0), 0))], split


def _ffn_ln_kernel(*refs, alpha, n_in, n_out, in_split, out_split, has_first, emit_w):
    x_refs = refs[:n_in]
    w1_ref, w3_ref, w2_ref, g_ref, b_ref = refs[n_in:n_in + 5]
    pos = n_in + 5
    first_ref = refs[pos] if has_first else None
    pos += has_first
    o_refs = refs[pos:pos + n_out]
    pos += n_out
    wb_refs = refs[pos:pos + 3] if emit_w else ()
    xb_ref, acc_ref = refs[pos + len(wb_refs):]
    i = pl.program_id(0)
    f = pl.program_id(1)
    last_f = pl.num_programs(1) - 1

    def tile():
        @pl.when(f == 0)
        def _():
            def cast(x_ref):
                xb_ref[...] = x_ref[...].astype(BF16)
            _for_owner(x_refs, in_split, i, cast)
            acc_ref[...] = jnp.zeros_like(acc_ref)

        w1, w3, w2 = (r[...].astype(BF16) for r in (w1_ref, w3_ref, w2_ref))
        for wb_ref, w in zip(wb_refs, (w1, w3, w2)):
            wb_ref[...] = w
        xb = xb_ref[...]
        a = _dot(xb, w1)
        c = _dot(xb, w3)
        h = (a * _sigmoid(a) * c).astype(BF16)
        acc_ref[...] += _dot(h, w2)

        @pl.when(f == last_f)
        def _():
            def finish(x_ref):
                z = alpha * x_ref[...] + 0.5 * acc_ref[...]
                acc_ref[...] = _layer_norm(z, g_ref[...], b_ref[...])
            _for_owner(x_refs, in_split, i, finish)

            def emit(o_ref):
                o_ref[...] = acc_ref[...]
            _for_owner(o_refs, out_split, i, emit)

    if not has_first:
        tile()
        return
    pl.when(i > 0)(tile)

    @pl.when(jnp.logical_and(i == 0, f == last_f))
    def _():
        o_refs[0][...] = first_ref[...]


def _ffn_ln(xs, w1, w3, w2, g, b, alpha, out_rows):
    d = xs[0].shape[1]
    n = sum(x.shape[0] for x in xs)
    dff = w1.shape[1]
    tm, tf, tf0 = min(TM_FFN, n), min(TF_FFN, dff), min(TF_FFN_FIRST, dff)
    assert sum(out_rows) == n and dff % tf == 0 and dff % tf0 == 0
    assert all(x.shape[0] % tm == 0 for x in xs) and all(r % tm == 0 for r in out_rows)
    vec = pl.BlockSpec((1, d), lambda i, f: (0, 0))
    tile0 = pl.BlockSpec((tm, d), lambda i, f: (0, 0))
    scratch = [pltpu.VMEM((tm, d), BF16), pltpu.VMEM((tm, d), F32)]
    kern = functools.partial(_ffn_ln_kernel, alpha=alpha)

    w_specs = [pl.BlockSpec((d, tf0), lambda i, f: (0, f)), pl.BlockSpec((d, tf0), lambda i, f: (0, f)),
               pl.BlockSpec((tf0, d), lambda i, f: (f, 0))]
    first, w1b, w3b, w2b = pl.pallas_call(
        functools.partial(kern, n_in=1, n_out=1, in_split=0, out_split=0, has_first=False, emit_w=True),
        out_shape=(jax.ShapeDtypeStruct((tm, d), F32),
                   *(jax.ShapeDtypeStruct(w.shape, BF16) for w in (w1, w3, w2))),
        grid=(1, dff // tf0),
        in_specs=[tile0] + w_specs + [vec, vec],
        out_specs=(tile0, *w_specs),
        scratch_shapes=scratch,
        compiler_params=_params(("arbitrary", "arbitrary")),
        name="ffn_ln_first",
    )(xs[0], w1, w3, w2, g, b)

    wf = lambda i, f: jnp.where(i == 0, 0, f)
    w_specs = [pl.BlockSpec((d, tf), lambda i, f: (0, wf(i, f))), pl.BlockSpec((d, tf), lambda i, f: (0, wf(i, f))),
               pl.BlockSpec((tf, d), lambda i, f: (wf(i, f), 0))]
    in_specs, in_split = _segment_specs([x.shape[0] for x in xs], tm, d)
    out_specs, out_split = _segment_specs(list(out_rows), tm, d)
    return pl.pallas_call(
        functools.partial(kern, n_in=len(xs), n_out=len(out_rows), in_split=in_split, out_split=out_split,
                          has_first=True, emit_w=False),
        out_shape=tuple(jax.ShapeDtypeStruct((r, d), F32) for r in out_rows),
        grid=(n // tm, dff // tf),
        in_specs=in_specs + w_specs + [vec, vec, tile0],
        out_specs=tuple(out_specs),
        scratch_shapes=scratch,
        compiler_params=_params(("arbitrary", "arbitrary")),
        name="ffn_ln",
    )(*xs, w1b, w3b, w2b, g, b, first)


def _prep_w_in_kernel(a_ref, nb_ref, if_ref, w_ref, wif_ref, *, n_aligned, n_gate):
    j = pl.program_id(0)
    tw, k_rows = a_ref.shape
    tr = min(TR_PREP, k_rows)

    def emit(rows_of):
        for c0 in range(0, k_rows, tr):
            w_ref[c0:c0 + tr, :] = rows_of(c0).T.astype(BF16)

    @pl.when(j < n_aligned)
    def _():
        emit(lambda c0: a_ref[:, c0:c0 + tr])

    @pl.when(j >= n_aligned)
    def _():
        emit(lambda c0: jnp.concatenate([a_ref[n_gate:tw, c0:c0 + tr], nb_ref[:, c0:c0 + tr]], axis=0))

    @pl.when(j == 0)
    def _():
        for c0 in range(0, k_rows, tr):
            rows = jnp.concatenate([if_ref[:, c0:c0 + tr], jnp.zeros((LANES - n_gate, tr), F32)], axis=0)
            wif_ref[c0:c0 + tr, :] = rows.T.astype(BF16)


def _prep_w_in(w_in_t, layer, n_aligned_cols, n_gate, n_shifted_cols):
    _, n_in, k_rows = w_in_t.shape
    tw = TW_PREP
    assert n_aligned_cols % tw == 0 and n_shifted_cols % tw == 0 and n_gate == SUBLANES
    assert k_rows % min(TR_PREP, k_rows) == 0 and n_aligned_cols + n_gate + n_shifted_cols == n_in
    n_aligned = n_aligned_cols // tw
    n_tiles = n_aligned + n_shifted_cols // tw
    per = tw // n_gate
    return pl.pallas_call(
        functools.partial(_prep_w_in_kernel, n_aligned=n_aligned, n_gate=n_gate),
        out_shape=(jax.ShapeDtypeStruct((k_rows, n_tiles * tw), BF16),
                   jax.ShapeDtypeStruct((k_rows, LANES), BF16)),
        grid=(n_tiles,),
        in_specs=[
            pl.BlockSpec((None, tw, k_rows), lambda j: (layer, j, 0)),
            pl.BlockSpec((None, n_gate, k_rows), lambda j: (layer, (j + 1) * per, 0)),
            pl.BlockSpec((None, n_gate, k_rows), lambda j: (layer, n_aligned * per, 0)),
        ],
        out_specs=(pl.BlockSpec((k_rows, tw), lambda j: (0, j)),
                   pl.BlockSpec((k_rows, LANES), lambda j: (0, 0))),
        compiler_params=_params(("arbitrary",)),
        name="prep_w_in",
    )(w_in_t, w_in_t, w_in_t)


def _conv_tile(ubuf_ref, w_ref, y_ref, u0, out0, rows, width, cs):
    y = None
    for r in range(min(SUBLANES, width)):
        q = None
        for a in range((width - 1 - r) // SUBLANES + 1):
            k = width - 1 - (SUBLANES * a + r)
            term = w_ref[k:k + 1, cs] * ubuf_ref[pl.ds(u0 - SUBLANES * (a + 1), rows + SUBLANES), cs]
            q = term if q is None else q + term
        part = q[SUBLANES - r:SUBLANES - r + rows]
        y = part if y is None else y + part
    y_ref[pl.ds(out0, rows), cs] = y


def _proj_kernel(x_ref, wqk_ref, wv_ref, wo_ref, wla_ref, wlb_ref, wga_ref, wgb_ref, wif_ref, cw_ref, cache_ref,
                 qk_ref, v_ref, ga_ref, gb_ref, yc_ref, if_ref, csp_ref, css_ref,
                 xb_ref, ubuf_ref, carry_ref, tbuf_ref, ybuf_ref, *, k_col0, k_scale, n_prompt_tiles, tiles_per_seq, width, t_dec):
    i = pl.program_id(0)
    j = pl.program_id(1)
    tm, tn = yc_ref.shape
    hist = width - 1

    @pl.when(j == 0)
    def _():
        xb0 = x_ref[...].astype(BF16)
        xb_ref[...] = xb0
        if_ref[...] = _dot(xb0, wif_ref[...])

    def glu(xb):
        return _dot(xb, wla_ref[...]) * _sigmoid(_dot(xb, wlb_ref[...]))

    def dense(xb):
        qk_ref[...] = (_dot(xb, wqk_ref[...]) * jnp.where(j * tn >= k_col0, k_scale, 1.0)).astype(BF16)
        v_ref[...] = _dot(xb, wv_ref[...]).astype(BF16)
        ga_ref[...] = (_sigmoid(_dot(xb, wo_ref[...])) * _sigmoid(_dot(xb, wga_ref[...]))).astype(BF16)
        gb_ref[...] = _sigmoid(_dot(xb, wgb_ref[...])).astype(BF16)

    @pl.when(i < n_prompt_tiles)
    def _():
        xb = xb_ref[...]
        u = glu(xb)
        first = (i % tiles_per_seq) == 0
        ubuf_ref[0:HALO, :] = jnp.where(first, 0.0, carry_ref[j])
        ubuf_ref[HALO:HALO + tm, :] = u
        carry_ref[j] = u[tm - HALO:tm]
        csp_ref[...] = u[tm - hist:tm]
        strip = min(CONV_ROWS, tm)
        for c0 in range(0, tn, LANES):
            for r0 in range(0, tm, strip):
                _conv_tile(ubuf_ref, cw_ref, yc_ref, HALO + r0, r0, strip, width, slice(c0, c0 + LANES))
        dense(xb)

    @pl.when(i >= n_prompt_tiles)
    def _():
        xb = xb_ref[...]
        nb = tm // t_dec
        u = glu(xb)
        for c in range(tn // LANES):
            cs = slice(c * LANES, (c + 1) * LANES)
            tbuf_ref[c] = u[:, cs]
            new = [tbuf_ref[c, pl.ds(t, nb, stride=t_dec), :] for t in range(t_dec)]
            row = lambda p: cache_ref[p, :, cs] if p < hist else new[p - hist]
            for t in range(t_dec):
                acc = cw_ref[0:1, cs] * row(t)
                for k in range(1, width):
                    acc = acc + cw_ref[k:k + 1, cs] * row(t + k)
                ybuf_ref[c, pl.ds(t, nb, stride=t_dec), :] = acc
            yc_ref[:, cs] = ybuf_ref[c]
            for p in range(hist):
                css_ref[p, :, cs] = row(p + t_dec)
        dense(xb)


def _proj(x1, w, wif, cw, cache_t, hqk, k_scale, batch, seq, t_dec):
    n, d = x1.shape
    hist, dec_batch, _ = cache_t.shape
    width = cw.shape[0]
    tm, tn = min(TM_PROJ, seq), min(TN_PROJ, d)
    n_prompt = batch * seq
    assert n_prompt % tm == 0 and (n - n_prompt) % tm == 0 and seq % tm == 0 and d % tn == 0
    assert hqk % tn == 0 and 2 * hqk == d and tn % LANES == 0
    assert hist == width - 1 and SUBLANES * (hist // SUBLANES + 1) <= HALO <= tm and tm % CONV_ROWS == 0
    assert n - n_prompt == dec_batch * t_dec and tm % t_dec == 0 and (tm // t_dec) % SUBLANES == 0 and t_dec <= hist
    npt, tps, nb = n_prompt // tm, seq // tm, tm // t_dec
    per = d // tn
    col = pl.BlockSpec((tm, tn), lambda i, j: (i, j))
    wspec = lambda g: pl.BlockSpec((d, tn), lambda i, j: (0, g * per + j))
    smp_i = lambda i: jnp.maximum(i - npt, 0)
    smp_j = lambda i, j: jnp.where(i >= npt, j, 0)
    prm_i = lambda i: jnp.minimum(i, npt - 1)
    prm_j = lambda i, j: jnp.where(i < npt, j, per - 1)
    return pl.pallas_call(
        functools.partial(_proj_kernel, k_col0=hqk, k_scale=k_scale, n_prompt_tiles=npt, tiles_per_seq=tps,
                          width=width, t_dec=t_dec),
        out_shape=(
            jax.ShapeDtypeStruct((n, d), BF16),
            jax.ShapeDtypeStruct((n, d), BF16),
            jax.ShapeDtypeStruct((n, d), BF16),
            jax.ShapeDtypeStruct((n, d), BF16),
            jax.ShapeDtypeStruct((n, d), F32),
            jax.ShapeDtypeStruct((n, LANES), F32),
            jax.ShapeDtypeStruct((npt, hist, d), F32),
            jax.ShapeDtypeStruct((hist, dec_batch, d), F32),
        ),
        grid=(n // tm, d // tn),
        in_specs=[pl.BlockSpec((tm, d), lambda i, j: (i, 0)),
                  wspec(0), wspec(1), wspec(2), wspec(3), wspec(4), wspec(5), wspec(6),
                  pl.BlockSpec((d, LANES), lambda i, j: (0, 0)),
                  pl.BlockSpec((width, tn), lambda i, j: (0, j)),
                  pl.BlockSpec((hist, nb, tn), lambda i, j: (0, smp_i(i), smp_j(i, j)))],
        out_specs=(col, col, col, col, col, pl.BlockSpec((tm, LANES), lambda i, j: (i, 0)),
                   pl.BlockSpec((None, hist, tn), lambda i, j: (prm_i(i), 0, prm_j(i, j))),
                   pl.BlockSpec((hist, nb, tn), lambda i, j: (0, smp_i(i), smp_j(i, j)))),
        scratch_shapes=[pltpu.VMEM((tm, d), BF16), pltpu.VMEM((HALO + tm, tn), F32),
                        pltpu.VMEM((per, HALO, tn), F32),
                        pltpu.VMEM((tn // LANES, tm, LANES), F32), pltpu.VMEM((tn // LANES, tm, LANES), F32)],
        compiler_params=_params(("arbitrary", "arbitrary")),
        name="proj",
    )(x1, w, w, w, w, w, w, w, wif, cw, cache_t)


def _mlstm_chunks(q, k, v, ig_c, lf_c, c0, n0, m0, last):
    G, L, _ = q.shape
    row = lax.broadcasted_iota(jnp.int32, (L, L), 0)
    colm = lax.broadcasted_iota(jnp.int32, (L, L), 1)
    causal = (colm <= row)[None]
    eye = (colm == row)[None]
    upper = (row <= colm)[None]
    b_r = jnp.sum(jnp.where(upper, lf_c, 0.0), axis=1, keepdims=True)
    b_c = jnp.sum(jnp.where(eye, b_r, 0.0), axis=2, keepdims=True)
    ig_r = jnp.sum(jnp.where(eye, ig_c, 0.0), axis=1, keepdims=True)

    log_d = jnp.where(causal, b_c - b_r + ig_r, -jnp.inf)
    inter = b_c + m0
    m_t = jnp.maximum(inter, jnp.max(log_d, axis=2, keepdims=True))
    d = jnp.exp(log_d - m_t)
    w_inter = jnp.exp(inter - m_t)
    per_group = lambda fn: jnp.stack([fn(g) for g in range(G)])
    nt = (((1,), (1,)), ((), ()))
    tn = (((0,), (0,)), ((), ()))
    s = per_group(lambda g: lax.dot_general(q[g], k[g], nt, preferred_element_type=F32)) * d
    sb = s.astype(BF16)
    cb = c0.astype(BF16)
    num = per_group(lambda g: _dot(sb[g], v[g])) + w_inter * per_group(lambda g: _dot(q[g], cb[g]))
    qn = jnp.sum(s, axis=2, keepdims=True) + w_inter * jnp.sum(q.astype(F32) * n0, axis=2, keepdims=True)
    den = jnp.maximum(jnp.abs(qn), jnp.exp(-m_t))
    h = num * (1.0 / den)

    m_new = m_t[:, last:last + 1, :]
    w_k = jnp.exp(b_c[:, last:last + 1, :] - b_c + ig_c - m_new)
    w_c = jnp.exp(inter[:, last:last + 1, :] - m_new)
    kw = k.astype(F32) * w_k
    kwb = kw.astype(BF16)
    c_new = w_c * c0 + per_group(lambda g: lax.dot_general(kwb[g], v[g], tn, preferred_element_type=F32))
    n_new = w_c * n0 + jnp.sum(kw, axis=1, keepdims=True)
    return h, c_new, n_new, m_new


def _head_norm(h, g):
    mu = jnp.mean(h, axis=-1, keepdims=True)
    hc = h - mu
    var = jnp.mean(hc * hc, axis=-1, keepdims=True)
    return hc * lax.rsqrt(var + LN_EPS) * g


def _log_sigmoid(x):
    return jnp.minimum(x, 0.0) - jnp.log1p(jnp.exp(-jnp.abs(x)))


def _mlstm_prompt_kernel(qk_ref, v_ref, if_ref, bias_ref, g_ref, hn_ref, c_ref, n_ref, m_ref,
                         *, n_heads, d_qk, d_v):
    c = pl.program_id(1)

    @pl.when(c == 0)
    def _():
        c_ref[...] = jnp.zeros_like(c_ref)
        n_ref[...] = jnp.zeros_like(n_ref)
        m_ref[...] = jnp.zeros_like(m_ref)

    gates = if_ref[...] + bias_ref[...]
    lf_all = _log_sigmoid(gates)
    L = gates.shape[0]
    heads = range(n_heads)
    q = jnp.stack([qk_ref[:, h * d_qk:(h + 1) * d_qk] for h in heads])
    k = jnp.stack([qk_ref[:, (n_heads + h) * d_qk:(n_heads + h + 1) * d_qk] for h in heads])
    v = jnp.stack([v_ref[:, h * d_v:(h + 1) * d_v] for h in heads])
    ig_c = jnp.stack([gates[:, h:h + 1] for h in heads])
    lf_c = jnp.stack([lf_all[:, n_heads + h:n_heads + h + 1] for h in heads])
    hh, c_new, n_new, m_new = _mlstm_chunks(q, k, v, ig_c, lf_c, c_ref[0], n_ref[0], m_ref[0], L - 1)
    c_ref[0] = c_new
    n_ref[0] = n_new
    m_ref[0] = m_new
    g = jnp.stack([g_ref[:, h * d_v:(h + 1) * d_v] for h in heads])
    hn = _head_norm(hh, g).astype(BF16)
    for h in heads:
        hn_ref[:, h * d_v:(h + 1) * d_v] = hn[h]


def _mlstm_prompt(qk, v, iff, bias, g, batch, seq, n_heads, d_qk, d_v):
    d = v.shape[1]
    L = min(CHUNK, seq)
    assert seq % L == 0
    nc = seq // L
    tok = lambda b, c: (b * nc + c, 0)
    return pl.pallas_call(
        functools.partial(_mlstm_prompt_kernel, n_heads=n_heads, d_qk=d_qk, d_v=d_v),
        out_shape=(
            jax.ShapeDtypeStruct((batch * seq, d), BF16),
            jax.ShapeDtypeStruct((batch, n_heads, d_qk, d_v), F32),
            jax.ShapeDtypeStruct((batch, n_heads, 1, d_qk), F32),
            jax.ShapeDtypeStruct((batch, n_heads, 1, 1), F32),
        ),
        grid=(batch, nc),
        in_specs=[
            pl.BlockSpec((L, qk.shape[1]), tok),
            pl.BlockSpec((L, d), tok),
            pl.BlockSpec((L, LANES), tok),
            pl.BlockSpec((1, LANES), lambda b, c: (0, 0)),
            pl.BlockSpec((1, d), lambda b, c: (0, 0)),
        ],
        out_specs=(
            pl.BlockSpec((L, d), tok),
            pl.BlockSpec((1, n_heads, d_qk, d_v), lambda b, c: (b, 0, 0, 0)),
            pl.BlockSpec((1, n_heads, 1, d_qk), lambda b, c: (b, 0, 0, 0)),
            pl.BlockSpec((1, n_heads, 1, 1), lambda b, c: (b, 0, 0, 0)),
        ),
        compiler_params=_params(("parallel", "arbitrary")),
        name="mlstm_prompt",
    )(qk, v, iff, bias, g)


def _mlstm_sample_kernel(qk_ref, v_ref, if_ref, bias_ref, g_ref, c0_ref, n0_ref, m0_ref,
                         hn_ref, c_ref, n_ref, m_ref, *, n_heads, d_qk, d_v, t_dec, bs):
    pad = (-t_dec) % (2 * SUBLANES)

    def zpad_rows(x):
        x = x.reshape(bs, t_dec, x.shape[1])
        return jnp.concatenate([x, jnp.zeros((bs, pad, x.shape[2]), x.dtype)], axis=1) if pad else x

    qk_all = zpad_rows(qk_ref[...].astype(F32))
    v_all = zpad_rows(v_ref[...].astype(F32))
    gates = zpad_rows(if_ref[...] + bias_ref[...])
    lf_all = _log_sigmoid(gates)
    groups = [(b, h) for b in range(bs) for h in range(n_heads)]
    q = jnp.stack([qk_all[b][:, h * d_qk:(h + 1) * d_qk] for b, h in groups]).astype(BF16)
    k = jnp.stack([qk_all[b][:, (n_heads + h) * d_qk:(n_heads + h + 1) * d_qk] for b, h in groups]).astype(BF16)
    v = jnp.stack([v_all[b][:, h * d_v:(h + 1) * d_v] for b, h in groups]).astype(BF16)
    ig_c = jnp.stack([gates[b][:, h:h + 1] for b, h in groups])
    lf_c = jnp.stack([lf_all[b][:, n_heads + h:n_heads + h + 1] for b, h in groups])
    flat = lambda r: r[...].reshape((bs * n_heads,) + r.shape[2:])
    hh, c_new, n_new, m_new = _mlstm_chunks(q, k, v, ig_c, lf_c, flat(c0_ref), flat(n0_ref), flat(m0_ref), t_dec - 1)
    c_ref[...] = c_new.reshape(c_ref.shape)
    n_ref[...] = n_new.reshape(n_ref.shape)
    m_ref[...] = m_new.reshape(m_ref.shape)
    g = jnp.stack([g_ref[:, h * d_v:(h + 1) * d_v] for _, h in groups])
    hn = _head_norm(hh[:, :t_dec], g)
    for i, (b, h) in enumerate(groups):
        hn_ref[b * t_dec:(b + 1) * t_dec, h * d_v:(h + 1) * d_v] = hn[i]


def _mlstm_sample(qk, v, iff, bias, g, c0, n0, m0, row0, t_dec):
    batch, n_heads, d_qk, d_v = c0.shape
    d = v.shape[1]
    bs = min(BS_MLSTM, batch)
    assert batch % bs == 0 and row0 % (bs * t_dec) == 0 and (bs * t_dec) % SUBLANES == 0
    blk0 = row0 // (bs * t_dec)
    tok = lambda i: (blk0 + i, 0)
    st4 = lambda i: (i, 0, 0, 0)
    return pl.pallas_call(
        functools.partial(_mlstm_sample_kernel, n_heads=n_heads, d_qk=d_qk, d_v=d_v, t_dec=t_dec, bs=bs),
        out_shape=(
            jax.ShapeDtypeStruct((batch * t_dec, d), F32),
            jax.ShapeDtypeStruct((batch, n_heads, d_qk, d_v), F32),
            jax.ShapeDtypeStruct((batch, n_heads, 1, d_qk), F32),
            jax.ShapeDtypeStruct((batch, n_heads, 1, 1), F32),
        ),
        grid=(batch // bs,),
        in_specs=[
            pl.BlockSpec((bs * t_dec, qk.shape[1]), tok),
            pl.BlockSpec((bs * t_dec, d), tok),
            pl.BlockSpec((bs * t_dec, LANES), tok),
            pl.BlockSpec((1, LANES), lambda i: (0, 0)),
            pl.BlockSpec((1, d), lambda i: (0, 0)),
            pl.BlockSpec((bs, n_heads, d_qk, d_v), st4),
            pl.BlockSpec((bs, n_heads, 1, d_qk), st4),
            pl.BlockSpec((bs, n_heads, 1, 1), st4),
        ],
        out_specs=(
            pl.BlockSpec((bs * t_dec, d), lambda i: (i, 0)),
            pl.BlockSpec((bs, n_heads, d_qk, d_v), st4),
            pl.BlockSpec((bs, n_heads, 1, d_qk), st4),
            pl.BlockSpec((bs, n_heads, 1, 1), st4),
        ),
        compiler_params=_params(("parallel",)),
        name="mlstm_sample",
    )(qk, v, iff, bias, g, c0, n0, m0)


def _mix_ln_kernel(yc_ref, ga_ref, gb_ref, hnp_ref, hns_ref, x1_ref, cb_ref, cg_ref, cbb_ref, w_ref, g_ref, b_ref,
                   o_ref, *, alpha, split):
    def body(hn_ref):
        z = _layer_norm(yc_ref[...] + cb_ref[...], cg_ref[...], cbb_ref[...])
        hb = z * _sigmoid(z)
        mixin = ga_ref[...].astype(F32) * hn_ref[...].astype(F32) + gb_ref[...].astype(F32) * hb
        mix = _dot(mixin.astype(BF16), w_ref[...])
        o_ref[...] = _layer_norm(alpha * x1_ref[...] + mix, g_ref[...], b_ref[...])

    _for_owner((hnp_ref, hns_ref), split, pl.program_id(0), body)


def _mix_ln(yc, ga, gb, hn_p, hn_s, x1, cb, cg, cbb, w, g, b, alpha):
    n, d = x1.shape
    tm = min(TM_MIX, hn_s.shape[0])
    assert hn_p.shape[0] % tm == 0 and hn_s.shape[0] % tm == 0 and hn_p.shape[0] + hn_s.shape[0] == n
    tok = pl.BlockSpec((tm, d), lambda i: (i, 0))
    vec = pl.BlockSpec((1, d), lambda i: (0, 0))
    hn_specs, split = _segment_specs([hn_p.shape[0], hn_s.shape[0]], tm, d)
    return pl.pallas_call(
        functools.partial(_mix_ln_kernel, alpha=alpha, split=split),
        out_shape=jax.ShapeDtypeStruct((n, d), F32),
        grid=(n // tm,),
        in_specs=[tok, tok, tok, *hn_specs, tok, vec, vec, vec, pl.BlockSpec((d, d), lambda i: (0, 0)), vec, vec],
        out_specs=tok,
        compiler_params=_params(("arbitrary",)),
        name="mix_ln",
    )(yc, ga, gb, hn_p, hn_s, x1, cb, cg, cbb, w, g, b)


def _layer(xs, layer, batch, seq, dec_batch, t_dec, c0, n0, m0, cache, w_in, p, alpha):
    (ffn1_w1, ffn1_w3, ffn1_w2, ln1_g, ln1_b, b_igate, b_fgate, mh_norm_g,
     conv_w, conv_b, conv_ln_g, conv_ln_b, w_out, ln2_g, ln2_b,
     ffn2_w1, ffn2_w3, ffn2_w2, ln3_g, ln3_b) = p
    d = xs[0].shape[1]
    n_heads, d_qk, d_v = c0.shape[1:]
    n_prompt, n_sample = batch * seq, dec_batch * t_dec
    hqk, dm = n_heads * d_qk, n_heads * d_v
    assert 2 * hqk == d and dm == d and conv_w.shape[1] == d
    vec = lambda a: a.reshape(1, -1).astype(F32)
    bf = lambda a: a.astype(BF16)

    (x1,) = _ffn_ln(xs, ffn1_w1, ffn1_w3, ffn1_w2, vec(ln1_g), vec(ln1_b), alpha, (n_prompt + n_sample,))
    w, wif = _prep_w_in(jnp.swapaxes(w_in, 1, 2), layer, 2 * hqk + 2 * dm, 2 * n_heads, 4 * d)
    qk, v, ga, gb, yc, iff, tile_tails, conv_s_t = _proj(x1, w, wif, conv_w.astype(F32), jnp.swapaxes(cache, 0, 1),
                                                         hqk, float(d_qk) ** -0.5, batch, seq, t_dec)

    gate_bias = jnp.pad(jnp.concatenate([b_igate, b_fgate]).astype(F32), (0, LANES - 2 * n_heads)).reshape(1, LANES)
    mh_g = vec(mh_norm_g)
    hn_p, c_p, n_p, m_p = _mlstm_prompt(qk, v, iff, gate_bias, mh_g, batch, seq, n_heads, d_qk, d_v)
    hn_s, c_s, n_s, m_s = _mlstm_sample(qk, v, iff, gate_bias, mh_g, c0,
                                        n0.reshape(dec_batch, n_heads, 1, d_qk),
                                        m0.reshape(dec_batch, n_heads, 1, 1), n_prompt, t_dec)

    x2 = _mix_ln(yc, ga, gb, hn_p, hn_s, x1, vec(conv_b), vec(conv_ln_g), vec(conv_ln_b), bf(w_out),
                 vec(ln2_g), vec(ln2_b), alpha)
    y_p, y_s = _ffn_ln((x2,), ffn2_w1, ffn2_w3, ffn2_w2, vec(ln3_g), vec(ln3_b), alpha, (n_prompt, n_sample))
    tiles_per_seq = tile_tails.shape[0] // batch
    conv_p = tile_tails[tiles_per_seq - 1::tiles_per_seq]
    states_p = (c_p, n_p.reshape(batch, n_heads, d_qk), m_p.reshape(batch, n_heads), conv_p)
    states_s = (c_s, n_s.reshape(dec_batch, n_heads, d_qk), m_s.reshape(dec_batch, n_heads),
                jnp.swapaxes(conv_s_t, 0, 1))
    return (y_p, y_s), states_p, states_s


def kernel(x_prompt, x_sample, state_C, state_n, state_m, cache_conv, ffn1_w1, ffn1_w3, ffn1_w2, ln1_g, ln1_b, w_in, b_igate, b_fgate, mh_norm_g, conv_w, conv_b, conv_ln_g, conv_ln_b, w_out, ln2_g, ln2_b, ffn2_w1, ffn2_w3, ffn2_w2, ln3_g, ln3_b):
    batch, seq, d = x_prompt.shape
    dec_batch, t_dec, _ = x_sample.shape
    depth = ffn1_w1.shape[0]
    alpha = (2.0 * depth) ** 0.25
    xs = (x_prompt.reshape(batch * seq, d), x_sample.reshape(dec_batch * t_dec, d))
    weights = (ffn1_w1, ffn1_w3, ffn1_w2, ln1_g, ln1_b, b_igate, b_fgate, mh_norm_g,
               conv_w, conv_b, conv_ln_g, conv_ln_b, w_out, ln2_g, ln2_b,
               ffn2_w1, ffn2_w3, ffn2_w2, ln3_g, ln3_b)
    outs_p, outs_s = [], []
    for l in range(depth):
        p = tuple(wt[l] for wt in weights)
        xs, st_p, st_s = _layer(xs, l, batch, seq, dec_batch, t_dec, state_C[l], state_n[l], state_m[l],
                                cache_conv[l], w_in, p, alpha)
        outs_p.append(st_p)
        outs_s.append(st_s)
    y_p = xs[0].reshape(batch, seq, d)
    y_s = xs[1].reshape(dec_batch, t_dec, d)
    stack = lambda outs, i: jnp.stack([o[i] for o in outs])
    return (y_p, y_s,
            stack(outs_p, 0), stack(outs_p, 1), stack(outs_p, 2), stack(outs_p, 3),
            stack(outs_s, 0), stack(outs_s, 1), stack(outs_s, 2), stack(outs_s, 3))
```

```python
import functools

import jax
import jax.numpy as jnp
from jax import lax
from jax.experimental import pallas as pl
from jax.experimental.pallas import tpu as pltpu

F32 = jnp.float32
BF16 = jnp.bfloat16
LN_EPS = 1e-5
LANES = 128
SUBLANES = 8
HALO = 32
VMEM_LIMIT = 56 * 1024 * 1024

TM_FFN = 512
TF_FFN = 512
TF_FFN_FIRST = 256
TM_PROJ = 512
TN_PROJ = 256
TW_PREP = 512
TR_PREP = 512
TM_MIX = 256
CONV_ROWS = 64
CHUNK = 128


def _params(sem):
    return pltpu.CompilerParams(dimension_semantics=sem, vmem_limit_bytes=VMEM_LIMIT)


def _sigmoid(x):
    return jax.nn.sigmoid(x)


def _layer_norm(z, g, b):
    mu = jnp.mean(z, axis=-1, keepdims=True)
    zc = z - mu
    var = jnp.mean(zc * zc, axis=-1, keepdims=True)
    return zc * lax.rsqrt(var + LN_EPS) * g + b


def _dot(a, b):
    return jnp.dot(a, b, preferred_element_type=F32)


def _for_owner(refs, split, i, fn):
    if len(refs) == 1:
        fn(refs[0])
        return
    pl.when(i < split)(lambda: fn(refs[0]))
    pl.when(i >= split)(lambda: fn(refs[1]))


def _segment_specs(arrays_rows, tm, d):
    if len(arrays_rows) == 1:
        return [pl.BlockSpec((tm, d), lambda i, *_: (i, 0))], 0
    split = arrays_rows[0] // tm
    last0 = split - 1
    return [pl.BlockSpec((tm, d), lambda i, *_: (jnp.minimum(i, last0), 0)),
            pl.BlockSpec((tm, d), lambda i, *_: (jnp.maximum(i - split, 0), 0))], split


def _ffn_ln_kernel(*refs, alpha, n_in, n_out, in_split, out_split, has_first, emit_w):
    x_refs = refs[:n_in]
    w1_ref, w3_ref, w2_ref, g_ref, b_ref = refs[n_in:n_in + 5]
    pos = n_in + 5
    first_ref = refs[pos] if has_first else None
    pos += has_first
    o_refs = refs[pos:pos + n_out]
    pos += n_out
    wb_refs = refs[pos:pos + 3] if emit_w else ()
    xb_ref, acc_ref = refs[pos + len(wb_refs):]
    i = pl.program_id(0)
    f = pl.program_id(1)
    last_f = pl.num_programs(1) - 1

    def tile():
        @pl.when(f == 0)
        def _():
            def cast(x_ref):
                xb_ref[...] = x_ref[...].astype(BF16)
            _for_owner(x_refs, in_split, i, cast)
            acc_ref[...] = jnp.zeros_like(acc_ref)

        w1, w3, w2 = (r[...].astype(BF16) for r in (w1_ref, w3_ref, w2_ref))
        for wb_ref, w in zip(wb_refs, (w1, w3, w2)):
            wb_ref[...] = w
        xb = xb_ref[...]
        a = _dot(xb, w1)
        c = _dot(xb, w3)
        h = (a * _sigmoid(a) * c).astype(BF16)
        acc_ref[...] += _dot(h, w2)

        @pl.when(f == last_f)
        def _():
            def finish(x_ref):
                z = alpha * x_ref[...] + 0.5 * acc_ref[...]
                acc_ref[...] = _layer_norm(z, g_ref[...], b_ref[...])
            _for_owner(x_refs, in_split, i, finish)

            def emit(o_ref):
                o_ref[...] = acc_ref[...]
            _for_owner(o_refs, out_split, i, emit)

    if not has_first:
        tile()
        return
    pl.when(i > 0)(tile)

    @pl.when(jnp.logical_and(i == 0, f == last_f))
    def _():
        o_refs[0][...] = first_ref[...]


def _ffn_ln(xs, w1, w3, w2, g, b, alpha, out_rows):
    d = xs[0].shape[1]
    n = sum(x.shape[0] for x in xs)
    dff = w1.shape[1]
    tm, tf, tf0 = min(TM_FFN, n), min(TF_FFN, dff), min(TF_FFN_FIRST, dff)
    assert sum(out_rows) == n and dff % tf == 0 and dff % tf0 == 0
    assert all(x.shape[0] % tm == 0 for x in xs) and all(r % tm == 0 for r in out_rows)
    vec = pl.BlockSpec((1, d), lambda i, f: (0, 0))
    tile0 = pl.BlockSpec((tm, d), lambda i, f: (0, 0))
    once = pl.BlockSpec((tm, d), lambda i, f: (0, 0), pipeline_mode=pl.Buffered(1))
    scratch = [pltpu.VMEM((tm, d), BF16), pltpu.VMEM((tm, d), F32)]
    kern = functools.partial(_ffn_ln_kernel, alpha=alpha)

    w_specs = [pl.BlockSpec((d, tf0), lambda i, f: (0, f)), pl.BlockSpec((d, tf0), lambda i, f: (0, f)),
               pl.BlockSpec((tf0, d), lambda i, f: (f, 0))]
    first, w1b, w3b, w2b = pl.pallas_call(
        functools.partial(kern, n_in=1, n_out=1, in_split=0, out_split=0, has_first=False, emit_w=True),
        out_shape=(jax.ShapeDtypeStruct((tm, d), F32),
                   *(jax.ShapeDtypeStruct(w.shape, BF16) for w in (w1, w3, w2))),
        grid=(1, dff // tf0),
        in_specs=[once] + w_specs + [vec, vec],
        out_specs=(once, *w_specs),
        scratch_shapes=scratch,
        compiler_params=_params(("arbitrary", "arbitrary")),
        name="ffn_ln_first",
    )(xs[0], w1, w3, w2, g, b)

    wf = lambda i, f: jnp.where(i == 0, 0, f)
    w_specs = [pl.BlockSpec((d, tf), lambda i, f: (0, wf(i, f))), pl.BlockSpec((d, tf), lambda i, f: (0, wf(i, f))),
               pl.BlockSpec((tf, d), lambda i, f: (wf(i, f), 0))]
    in_specs, in_split = _segment_specs([x.shape[0] for x in xs], tm, d)
    out_specs, out_split = _segment_specs(list(out_rows), tm, d)
    return pl.pallas_call(
        functools.partial(kern, n_in=len(xs), n_out=len(out_rows), in_split=in_split, out_split=out_split,
                          has_first=True, emit_w=False),
        out_shape=tuple(jax.ShapeDtypeStruct((r, d), F32) for r in out_rows),
        grid=(n // tm, dff // tf),
        in_specs=in_specs + w_specs + [vec, vec, tile0],
        out_specs=tuple(out_specs),
        scratch_shapes=scratch,
        compiler_params=_params(("arbitrary", "arbitrary")),
        name="ffn_ln",
    )(*xs, w1b, w3b, w2b, g, b, first)


def _prep_w_in_kernel(a_ref, nb_ref, if_ref, w_ref, wif_ref, *, n_aligned, n_gate):
    j = pl.program_id(0)
    tw, k_rows = a_ref.shape
    tr = min(TR_PREP, k_rows)

    def emit(rows_of):
        for c0 in range(0, k_rows, tr):
            w_ref[c0:c0 + tr, :] = rows_of(c0).T.astype(BF16)

    @pl.when(j < n_aligned)
    def _():
        emit(lambda c0: a_ref[:, c0:c0 + tr])

    @pl.when(j >= n_aligned)
    def _():
        emit(lambda c0: jnp.concatenate([a_ref[n_gate:tw, c0:c0 + tr], nb_ref[:, c0:c0 + tr]], axis=0))

    @pl.when(j == 0)
    def _():
        for c0 in range(0, k_rows, tr):
            rows = jnp.concatenate([if_ref[:, c0:c0 + tr], jnp.zeros((LANES - n_gate, tr), F32)], axis=0)
            wif_ref[c0:c0 + tr, :] = rows.T.astype(BF16)


def _prep_w_in(w_in_t, layer, n_aligned_cols, n_gate, n_shifted_cols):
    _, n_in, k_rows = w_in_t.shape
    tw = TW_PREP
    assert n_aligned_cols % tw == 0 and n_shifted_cols % tw == 0 and n_gate == SUBLANES
    assert k_rows % min(TR_PREP, k_rows) == 0 and n_aligned_cols + n_gate + n_shifted_cols == n_in
    n_aligned = n_aligned_cols // tw
    n_tiles = n_aligned + n_shifted_cols // tw
    per = tw // n_gate
    return pl.pallas_call(
        functools.partial(_prep_w_in_kernel, n_aligned=n_aligned, n_gate=n_gate),
        out_shape=(jax.ShapeDtypeStruct((k_rows, n_tiles * tw), BF16),
                   jax.ShapeDtypeStruct((k_rows, LANES), BF16)),
        grid=(n_tiles,),
        in_specs=[
            pl.BlockSpec((None, tw, k_rows), lambda j: (layer, j, 0)),
            pl.BlockSpec((None, n_gate, k_rows), lambda j: (layer, (j + 1) * per, 0)),
            pl.BlockSpec((None, n_gate, k_rows), lambda j: (layer, n_aligned * per, 0)),
        ],
        out_specs=(pl.BlockSpec((k_rows, tw), lambda j: (0, j)),
                   pl.BlockSpec((k_rows, LANES), lambda j: (0, 0))),
        compiler_params=_params(("arbitrary",)),
        name="prep_w_in",
    )(w_in_t, w_in_t, w_in_t)


def _conv_tile(ubuf_ref, w_ref, y_ref, u0, out0, rows, width, cs):
    y = None
    for r in range(min(SUBLANES, width)):
        q = None
        for a in range((width - 1 - r) // SUBLANES + 1):
            k = width - 1 - (SUBLANES * a + r)
            term = w_ref[k:k + 1, cs] * ubuf_ref[pl.ds(u0 - SUBLANES * (a + 1), rows + SUBLANES), cs]
            q = term if q is None else q + term
        part = q[SUBLANES - r:SUBLANES - r + rows]
        y = part if y is None else y + part
    y_ref[pl.ds(out0, rows), cs] = y


def _proj_kernel(x_ref, wqk_ref, wv_ref, wo_ref, wla_ref, wlb_ref, wga_ref, wgb_ref, wif_ref, cw_ref, cache_ref,
                 qk_ref, v_ref, ga_ref, gb_ref, yc_ref, if_ref, csp_ref, css_ref,
                 xb_ref, ubuf_ref, carry_ref, tbuf_ref, ybuf_ref, *, k_col0, k_scale, n_prompt_tiles, tiles_per_seq, width, t_dec):
    i = pl.program_id(0)
    j = pl.program_id(1)
    tm, tn = yc_ref.shape
    hist = width - 1

    @pl.when(j == 0)
    def _():
        xb0 = x_ref[...].astype(BF16)
        xb_ref[...] = xb0
        if_ref[...] = _dot(xb0, wif_ref[...])

    def glu(xb):
        return _dot(xb, wla_ref[...]) * _sigmoid(_dot(xb, wlb_ref[...]))

    def dense(xb):
        qk_ref[...] = (_dot(xb, wqk_ref[...]) * jnp.where(j * tn >= k_col0, k_scale, 1.0)).astype(BF16)
        v_ref[...] = _dot(xb, wv_ref[...]).astype(BF16)
        ga_ref[...] = (_sigmoid(_dot(xb, wo_ref[...])) * _sigmoid(_dot(xb, wga_ref[...]))).astype(BF16)
        gb_ref[...] = _sigmoid(_dot(xb, wgb_ref[...])).astype(BF16)

    @pl.when(i < n_prompt_tiles)
    def _():
        xb = xb_ref[...]
        u = glu(xb)
        first = (i % tiles_per_seq) == 0
        ubuf_ref[0:HALO, :] = jnp.where(first, 0.0, carry_ref[j])
        ubuf_ref[HALO:HALO + tm, :] = u
        carry_ref[j] = u[tm - HALO:tm]
        csp_ref[...] = u[tm - hist:tm]
        strip = min(CONV_ROWS, tm)
        for c0 in range(0, tn, LANES):
            for r0 in range(0, tm, strip):
                _conv_tile(ubuf_ref, cw_ref, yc_ref, HALO + r0, r0, strip, width, slice(c0, c0 + LANES))
        dense(xb)

    @pl.when(i >= n_prompt_tiles)
    def _():
        xb = xb_ref[...]
        nb = tm // t_dec
        u = glu(xb)
        for c in range(tn // LANES):
            cs = slice(c * LANES, (c + 1) * LANES)
            tbuf_ref[c] = u[:, cs]
            new = [tbuf_ref[c, pl.ds(t, nb, stride=t_dec), :] for t in range(t_dec)]
            row = lambda p: cache_ref[p, :, cs] if p < hist else new[p - hist]
            for t in range(t_dec):
                acc = cw_ref[0:1, cs] * row(t)
                for k in range(1, width):
                    acc = acc + cw_ref[k:k + 1, cs] * row(t + k)
                ybuf_ref[c, pl.ds(t, nb, stride=t_dec), :] = acc
            yc_ref[:, cs] = ybuf_ref[c]
            for p in range(hist):
                css_ref[p, :, cs] = row(p + t_dec)
        dense(xb)


def _proj(x1, w, wif, cw, cache_t, hqk, k_scale, batch, seq, t_dec):
    n, d = x1.shape
    hist, dec_batch, _ = cache_t.shape
    width = cw.shape[0]
    tm, tn = min(TM_PROJ, seq), min(TN_PROJ, d)
    n_prompt = batch * seq
    assert n_prompt % tm == 0 and (n - n_prompt) % tm == 0 and seq % tm == 0 and d % tn == 0
    assert hqk % tn == 0 and 2 * hqk == d and tn % LANES == 0
    assert hist == width - 1 and SUBLANES * (hist // SUBLANES + 1) <= HALO <= tm and tm % CONV_ROWS == 0
    assert n - n_prompt == dec_batch * t_dec and tm % t_dec == 0 and (tm // t_dec) % SUBLANES == 0 and t_dec <= hist
    npt, tps, nb = n_prompt // tm, seq // tm, tm // t_dec
    per = d // tn
    col = pl.BlockSpec((tm, tn), lambda i, j: (i, j))
    wspec = lambda g: pl.BlockSpec((d, tn), lambda i, j: (0, g * per + j))
    smp_i = lambda i: jnp.maximum(i - npt, 0)
    smp_j = lambda i, j: jnp.where(i >= npt, j, 0)
    prm_i = lambda i: jnp.minimum(i, npt - 1)
    prm_j = lambda i, j: jnp.where(i < npt, j, per - 1)
    return pl.pallas_call(
        functools.partial(_proj_kernel, k_col0=hqk, k_scale=k_scale, n_prompt_tiles=npt, tiles_per_seq=tps,
                          width=width, t_dec=t_dec),
        out_shape=(
            jax.ShapeDtypeStruct((n, d), BF16),
            jax.ShapeDtypeStruct((n, d), BF16),
            jax.ShapeDtypeStruct((n, d), BF16),
            jax.ShapeDtypeStruct((n, d), BF16),
            jax.ShapeDtypeStruct((n, d), F32),
            jax.ShapeDtypeStruct((n, LANES), F32),
            jax.ShapeDtypeStruct((npt, hist, d), F32),
            jax.ShapeDtypeStruct((hist, dec_batch, d), F32),
        ),
        grid=(n // tm, d // tn),
        in_specs=[pl.BlockSpec((tm, d), lambda i, j: (i, 0)),
                  wspec(0), wspec(1), wspec(2), wspec(3), wspec(4), wspec(5), wspec(6),
                  pl.BlockSpec((d, LANES), lambda i, j: (0, 0)),
                  pl.BlockSpec((width, tn), lambda i, j: (0, j)),
                  pl.BlockSpec((hist, nb, tn), lambda i, j: (0, smp_i(i), smp_j(i, j)))],
        out_specs=(col, col, col, col, col, pl.BlockSpec((tm, LANES), lambda i, j: (i, 0)),
                   pl.BlockSpec((None, hist, tn), lambda i, j: (prm_i(i), 0, prm_j(i, j))),
                   pl.BlockSpec((hist, nb, tn), lambda i, j: (0, smp_i(i), smp_j(i, j)))),
        scratch_shapes=[pltpu.VMEM((tm, d), BF16), pltpu.VMEM((HALO + tm, tn), F32),
                        pltpu.VMEM((per, HALO, tn), F32),
                        pltpu.VMEM((tn // LANES, tm, LANES), F32), pltpu.VMEM((tn // LANES, tm, LANES), F32)],
        compiler_params=_params(("arbitrary", "arbitrary")),
        name="proj",
    )(x1, w, w, w, w, w, w, w, wif, cw, cache_t)


def _mlstm_chunks(q, k, v, ig_c, lf_c, c0, n0, m0, last):
    G, L, _ = q.shape
    row = lax.broadcasted_iota(jnp.int32, (L, L), 0)
    colm = lax.broadcasted_iota(jnp.int32, (L, L), 1)
    causal = (colm <= row)[None]
    eye = (colm == row)[None]
    upper = (row <= colm)[None]
    b_r = jnp.sum(jnp.where(upper, lf_c, 0.0), axis=1, keepdims=True)
    b_c = jnp.sum(jnp.where(eye, b_r, 0.0), axis=2, keepdims=True)
    ig_r = jnp.sum(jnp.where(eye, ig_c, 0.0), axis=1, keepdims=True)

    log_d = jnp.where(causal, b_c - b_r + ig_r, -jnp.inf)
    inter = b_c + m0
    m_t = jnp.maximum(inter, jnp.max(log_d, axis=2, keepdims=True))
    d = jnp.exp(log_d - m_t)
    w_inter = jnp.exp(inter - m_t)
    per_group = lambda fn: jnp.stack([fn(g) for g in range(G)])
    nt = (((1,), (1,)), ((), ()))
    tn = (((0,), (0,)), ((), ()))
    s = per_group(lambda g: lax.dot_general(q[g], k[g], nt, preferred_element_type=F32)) * d
    sb = s.astype(BF16)
    cb = c0.astype(BF16)
    num = per_group(lambda g: _dot(sb[g], v[g])) + w_inter * per_group(lambda g: _dot(q[g], cb[g]))
    qn = jnp.sum(s, axis=2, keepdims=True) + w_inter * jnp.sum(q.astype(F32) * n0, axis=2, keepdims=True)
    den = jnp.maximum(jnp.abs(qn), jnp.exp(-m_t))
    h = num * (1.0 / den)

    m_new = m_t[:, last:last + 1, :]
    w_k = jnp.exp(b_c[:, last:last + 1, :] - b_c + ig_c - m_new)
    w_c = jnp.exp(inter[:, last:last + 1, :] - m_new)
    kw = k.astype(F32) * w_k
    kwb = kw.astype(BF16)
    c_new = w_c * c0 + per_group(lambda g: lax.dot_general(kwb[g], v[g], tn, preferred_element_type=F32))
    n_new = w_c * n0 + jnp.sum(kw, axis=1, keepdims=True)
    return h, c_new, n_new, m_new


def _head_norm(h, g):
    mu = jnp.mean(h, axis=-1, keepdims=True)
    hc = h - mu
    var = jnp.mean(hc * hc, axis=-1, keepdims=True)
    return hc * lax.rsqrt(var + LN_EPS) * g


def _log_sigmoid(x):
    return jnp.minimum(x, 0.0) - jnp.log1p(jnp.exp(-jnp.abs(x)))


def _mlstm_prompt_step(c, qk_ref, v_ref, if_ref, bias_ref, g_ref, hn_ref, c_ref, n_ref, m_ref, n_heads, d_qk, d_v):
    @pl.when(c == 0)
    def _():
        c_ref[...] = jnp.zeros_like(c_ref)
        n_ref[...] = jnp.zeros_like(n_ref)
        m_ref[...] = jnp.zeros_like(m_ref)

    gates = if_ref[...] + bias_ref[...]
    lf_all = _log_sigmoid(gates)
    L = gates.shape[0]
    heads = range(n_heads)
    q = jnp.stack([qk_ref[:, h * d_qk:(h + 1) * d_qk] for h in heads])
    k = jnp.stack([qk_ref[:, (n_heads + h) * d_qk:(n_heads + h + 1) * d_qk] for h in heads])
    v = jnp.stack([v_ref[:, h * d_v:(h + 1) * d_v] for h in heads])
    ig_c = jnp.stack([gates[:, h:h + 1] for h in heads])
    lf_c = jnp.stack([lf_all[:, n_heads + h:n_heads + h + 1] for h in heads])
    hh, c_new, n_new, m_new = _mlstm_chunks(q, k, v, ig_c, lf_c, c_ref[0], n_ref[0], m_ref[0], L - 1)
    c_ref[0] = c_new
    n_ref[0] = n_new
    m_ref[0] = m_new
    g = jnp.stack([g_ref[:, h * d_v:(h + 1) * d_v] for h in heads])
    hn = _head_norm(hh, g).astype(BF16)
    for h in heads:
        hn_ref[:, h * d_v:(h + 1) * d_v] = hn[h]


def _mlstm_sample_step(qk_ref, v_ref, if_ref, bias_ref, g_ref, c0_ref, n0_ref, m0_ref, hn_ref, c_ref, n_ref, m_ref,
                       n_heads, d_qk, d_v, t_dec, bs):
    pad = (-t_dec) % (2 * SUBLANES)

    def zpad_rows(x):
        x = x.reshape(bs, t_dec, x.shape[1])
        return jnp.concatenate([x, jnp.zeros((bs, pad, x.shape[2]), x.dtype)], axis=1) if pad else x

    qk_all = zpad_rows(qk_ref[...].astype(F32))
    v_all = zpad_rows(v_ref[...].astype(F32))
    gates = zpad_rows(if_ref[...] + bias_ref[...])
    lf_all = _log_sigmoid(gates)
    groups = [(b, h) for b in range(bs) for h in range(n_heads)]
    q = jnp.stack([qk_all[b][:, h * d_qk:(h + 1) * d_qk] for b, h in groups]).astype(BF16)
    k = jnp.stack([qk_all[b][:, (n_heads + h) * d_qk:(n_heads + h + 1) * d_qk] for b, h in groups]).astype(BF16)
    v = jnp.stack([v_all[b][:, h * d_v:(h + 1) * d_v] for b, h in groups]).astype(BF16)
    ig_c = jnp.stack([gates[b][:, h:h + 1] for b, h in groups])
    lf_c = jnp.stack([lf_all[b][:, n_heads + h:n_heads + h + 1] for b, h in groups])
    flat = lambda r: r[...].reshape((bs * n_heads,) + r.shape[2:])
    hh, c_new, n_new, m_new = _mlstm_chunks(q, k, v, ig_c, lf_c, flat(c0_ref), flat(n0_ref), flat(m0_ref), t_dec - 1)
    c_ref[...] = c_new.reshape(c_ref.shape)
    n_ref[...] = n_new.reshape(n_ref.shape)
    m_ref[...] = m_new.reshape(m_ref.shape)
    g = jnp.stack([g_ref[:, h * d_v:(h + 1) * d_v] for _, h in groups])
    hn = _head_norm(hh[:, :t_dec], g).astype(BF16)
    for i, (b, h) in enumerate(groups):
        hn_ref[b * t_dec:(b + 1) * t_dec, h * d_v:(h + 1) * d_v] = hn[i]


def _mlstm_kernel(qkp_ref, vp_ref, ifp_ref, qks_ref, vs_ref, ifs_ref, bias_ref, g_ref, c0_ref, n0_ref, m0_ref,
                  hnp_ref, cp_ref, np_ref, mp_ref, hns_ref, cs_ref, ns_ref, ms_ref,
                  *, n_chunks, n_heads, d_qk, d_v, t_dec, bs):
    _mlstm_prompt_step(pl.program_id(0) % n_chunks, qkp_ref, vp_ref, ifp_ref, bias_ref, g_ref,
                       hnp_ref, cp_ref, np_ref, mp_ref, n_heads, d_qk, d_v)
    _mlstm_sample_step(qks_ref, vs_ref, ifs_ref, bias_ref, g_ref, c0_ref, n0_ref, m0_ref,
                       hns_ref, cs_ref, ns_ref, ms_ref, n_heads, d_qk, d_v, t_dec, bs)


def _mlstm(qk, v, iff, bias, g, c0, n0, m0, batch, seq, t_dec):
    dec_batch, n_heads, d_qk, d_v = c0.shape
    d = v.shape[1]
    L = min(CHUNK, seq)
    assert seq % L == 0
    nc = seq // L
    steps = batch * nc
    assert dec_batch % steps == 0
    bs = dec_batch // steps
    rows = bs * t_dec
    n_prompt = batch * seq
    assert rows % (2 * SUBLANES) == 0 and n_prompt % rows == 0
    blk0 = n_prompt // rows
    ptok = lambda w: pl.BlockSpec((L, w), lambda i: (i, 0))
    stok = lambda w: pl.BlockSpec((rows, w), lambda i: (blk0 + i, 0))
    pst = lambda *tail: pl.BlockSpec((1, n_heads) + tail, lambda i: (i // nc, 0, 0, 0))
    sst = lambda *tail: pl.BlockSpec((bs, n_heads) + tail, lambda i: (i, 0, 0, 0))
    state_shapes = lambda nb: (jax.ShapeDtypeStruct((nb, n_heads, d_qk, d_v), F32),
                               jax.ShapeDtypeStruct((nb, n_heads, 1, d_qk), F32),
                               jax.ShapeDtypeStruct((nb, n_heads, 1, 1), F32))
    return pl.pallas_call(
        functools.partial(_mlstm_kernel, n_chunks=nc, n_heads=n_heads, d_qk=d_qk, d_v=d_v, t_dec=t_dec, bs=bs),
        out_shape=(jax.ShapeDtypeStruct((n_prompt, d), BF16), *state_shapes(batch),
                   jax.ShapeDtypeStruct((dec_batch * t_dec, d), BF16), *state_shapes(dec_batch)),
        grid=(steps,),
        in_specs=[ptok(qk.shape[1]), ptok(d), ptok(LANES), stok(qk.shape[1]), stok(d), stok(LANES),
                  pl.BlockSpec((1, LANES), lambda i: (0, 0)), pl.BlockSpec((1, d), lambda i: (0, 0)),
                  sst(d_qk, d_v), sst(1, d_qk), sst(1, 1)],
        out_specs=(ptok(d), pst(d_qk, d_v), pst(1, d_qk), pst(1, 1),
                   pl.BlockSpec((rows, d), lambda i: (i, 0)), sst(d_qk, d_v), sst(1, d_qk), sst(1, 1)),
        compiler_params=_params(("arbitrary",)),
        name="mlstm",
    )(qk, v, iff, qk, v, iff, bias, g, c0, n0, m0)


def _mix_ln_kernel(yc_ref, ga_ref, gb_ref, hnp_ref, hns_ref, x1_ref, cb_ref, cg_ref, cbb_ref, w_ref, g_ref, b_ref,
                   o_ref, *, alpha, split):
    def body(hn_ref):
        z = _layer_norm(yc_ref[...] + cb_ref[...], cg_ref[...], cbb_ref[...])
        hb = z * _sigmoid(z)
        mixin = ga_ref[...].astype(F32) * hn_ref[...].astype(F32) + gb_ref[...].astype(F32) * hb
        mix = _dot(mixin.astype(BF16), w_ref[...])
        o_ref[...] = _layer_norm(alpha * x1_ref[...] + mix, g_ref[...], b_ref[...])

    _for_owner((hnp_ref, hns_ref), split, pl.program_id(0), body)


def _mix_ln(yc, ga, gb, hn_p, hn_s, x1, cb, cg, cbb, w, g, b, alpha):
    n, d = x1.shape
    tm = min(TM_MIX, hn_s.shape[0])
    assert hn_p.shape[0] % tm == 0 and hn_s.shape[0] % tm == 0 and hn_p.shape[0] + hn_s.shape[0] == n
    tok = pl.BlockSpec((tm, d), lambda i: (i, 0))
    vec = pl.BlockSpec((1, d), lambda i: (0, 0))
    hn_specs, split = _segment_specs([hn_p.shape[0], hn_s.shape[0]], tm, d)
    return pl.pallas_call(
        functools.partial(_mix_ln_kernel, alpha=alpha, split=split),
        out_shape=jax.ShapeDtypeStruct((n, d), F32),
        grid=(n // tm,),
        in_specs=[tok, tok, tok, *hn_specs, tok, vec, vec, vec, pl.BlockSpec((d, d), lambda i: (0, 0)), vec, vec],
        out_specs=tok,
        compiler_params=_params(("arbitrary",)),
        name="mix_ln",
    )(yc, ga, gb, hn_p, hn_s, x1, cb, cg, cbb, w, g, b)


def _layer(xs, layer, batch, seq, dec_batch, t_dec, c0, n0, m0, cache, w_in, p, alpha):
    (ffn1_w1, ffn1_w3, ffn1_w2, ln1_g, ln1_b, b_igate, b_fgate, mh_norm_g,
     conv_w, conv_b, conv_ln_g, conv_ln_b, w_out, ln2_g, ln2_b,
     ffn2_w1, ffn2_w3, ffn2_w2, ln3_g, ln3_b) = p
    d = xs[0].shape[1]
    n_heads, d_qk, d_v = c0.shape[1:]
    n_prompt, n_sample = batch * seq, dec_batch * t_dec
    hqk, dm = n_heads * d_qk, n_heads * d_v
    assert 2 * hqk == d and dm == d and conv_w.shape[1] == d
    vec = lambda a: a.reshape(1, -1).astype(F32)
    bf = lambda a: a.astype(BF16)

    (x1,) = _ffn_ln(xs, ffn1_w1, ffn1_w3, ffn1_w2, vec(ln1_g), vec(ln1_b), alpha, (n_prompt + n_sample,))
    w, wif = _prep_w_in(jnp.swapaxes(w_in, 1, 2), layer, 2 * hqk + 2 * dm, 2 * n_heads, 4 * d)
    qk, v, ga, gb, yc, iff, tile_tails, conv_s_t = _proj(x1, w, wif, conv_w.astype(F32), jnp.swapaxes(cache, 0, 1),
                                                         hqk, float(d_qk) ** -0.5, batch, seq, t_dec)

    gate_bias = jnp.pad(jnp.concatenate([b_igate, b_fgate]).astype(F32), (0, LANES - 2 * n_heads)).reshape(1, LANES)
    mh_g = vec(mh_norm_g)
    hn_p, c_p, n_p, m_p, hn_s, c_s, n_s, m_s = _mlstm(qk, v, iff, gate_bias, mh_g, c0,
                                                      n0.reshape(dec_batch, n_heads, 1, d_qk),
                                                      m0.reshape(dec_batch, n_heads, 1, 1), batch, seq, t_dec)

    x2 = _mix_ln(yc, ga, gb, hn_p, hn_s, x1, vec(conv_b), vec(conv_ln_g), vec(conv_ln_b), bf(w_out),
                 vec(ln2_g), vec(ln2_b), alpha)
    y_p, y_s = _ffn_ln((x2,), ffn2_w1, ffn2_w3, ffn2_w2, vec(ln3_g), vec(ln3_b), alpha, (n_prompt, n_sample))
    tiles_per_seq = tile_tails.shape[0] // batch
    conv_p = tile_tails[tiles_per_seq - 1::tiles_per_seq]
    states_p = (c_p, n_p.reshape(batch, n_heads, d_qk), m_p.reshape(batch, n_heads), conv_p)
    states_s = (c_s, n_s.reshape(dec_batch, n_heads, d_qk), m_s.reshape(dec_batch, n_heads),
                jnp.swapaxes(conv_s_t, 0, 1))
    return (y_p, y_s), states_p, states_s


def kernel(x_prompt, x_sample, state_C, state_n, state_m, cache_conv, ffn1_w1, ffn1_w3, ffn1_w2, ln1_g, ln1_b, w_in, b_igate, b_fgate, mh_norm_g, conv_w, conv_b, conv_ln_g, conv_ln_b, w_out, ln2_g, ln2_b, ffn2_w1, ffn2_w3, ffn2_w2, ln3_g, ln3_b):
    batch, seq, d = x_prompt.shape
    dec_batch, t_dec, _ = x_sample.shape
    depth = ffn1_w1.shape[0]
    alpha = (2.0 * depth) ** 0.25
    xs = (x_prompt.reshape(batch * seq, d), x_sample.reshape(dec_batch * t_dec, d))
    weights = (ffn1_w1, ffn1_w3, ffn1_w2, ln1_g, ln1_b, b_igate, b_fgate, mh_norm_g,
               conv_w, conv_b, conv_ln_g, conv_ln_b, w_out, ln2_g, ln2_b,
               ffn2_w1, ffn2_w3, ffn2_w2, ln3_g, ln3_b)
    outs_p, outs_s = [], []
    for l in range(depth):
        p = tuple(wt[l] for wt in weights)
        xs, st_p, st_s = _layer(xs, l, batch, seq, dec_batch, t_dec, state_C[l], state_n[l], state_m[l],
                                cache_conv[l], w_in, p, alpha)
        outs_p.append(st_p)
        outs_s.append(st_s)
    y_p = xs[0].reshape(batch, seq, d)
    y_s = xs[1].reshape(dec_batch, t_dec, d)
    stack = lambda outs, i: jnp.stack([o[i] for o in outs])
    return (y_p, y_s,
            stack(outs_p, 0), stack(outs_p, 1), stack(outs_p, 2), stack(outs_p, 3),
            stack(outs_s, 0), stack(outs_s, 1), stack(outs_s, 2), stack(outs_s, 3))
```

```python
import functools

import jax
import jax.numpy as jnp
from jax import lax
from jax.experimental import pallas as pl
from jax.experimental.pallas import tpu as pltpu

F32 = jnp.float32
BF16 = jnp.bfloat16
LN_EPS = 1e-5
LANES = 128
SUBLANES = 8
HALO = 32
VMEM_LIMIT = 56 * 1024 * 1024

TM_FFN = 512
TF_FFN = 512
TF_FFN_FIRST = 256
TILES_FFN_FIRST = 2
TM_PROJ = 512
TN_PROJ = 256
TW_PREP = 512
TR_PREP = 512
TM_MIX = 256
CONV_ROWS = 64
CHUNK = 128


def _params(sem):
    return pltpu.CompilerParams(dimension_semantics=sem, vmem_limit_bytes=VMEM_LIMIT)


def _sigmoid(x):
    return jax.nn.sigmoid(x)


def _layer_norm(z, g, b):
    mu = jnp.mean(z, axis=-1, keepdims=True)
    zc = z - mu
    var = jnp.mean(zc * zc, axis=-1, keepdims=True)
    return zc * lax.rsqrt(var + LN_EPS) * g + b


def _dot(a, b):
    return jnp.dot(a, b, preferred_element_type=F32)


def _for_owner(refs, split, i, fn):
    if len(refs) == 1:
        fn(refs[0])
        return
    pl.when(i < split)(lambda: fn(refs[0]))
    pl.when(i >= split)(lambda: fn(refs[1]))


def _segment_specs(arrays_rows, tm, d):
    if len(arrays_rows) == 1:
        return [pl.BlockSpec((tm, d), lambda i, *_: (i, 0))], 0
    split = arrays_rows[0] // tm
    last0 = split - 1
    return [pl.BlockSpec((tm, d), lambda i, *_: (jnp.minimum(i, last0), 0)),
            pl.BlockSpec((tm, d), lambda i, *_: (jnp.maximum(i - split, 0), 0))], split


def _ffn_ln_kernel(*refs, alpha, n_in, n_out, in_split, out_split, n_first, emit_w):
    x_refs = refs[:n_in]
    w1_ref, w3_ref, w2_ref, g_ref, b_ref = refs[n_in:n_in + 5]
    pos = n_in + 5
    first_ref = refs[pos] if n_first else None
    pos += bool(n_first)
    o_refs = refs[pos:pos + n_out]
    pos += n_out
    wb_refs = refs[pos:pos + 3] if emit_w else ()
    scratch = refs[pos + len(wb_refs):]
    xb_ref, acc_ref = scratch if len(scratch) == 2 else (scratch[0], o_refs[0])
    i = pl.program_id(0)
    f = pl.program_id(1)
    last_f = pl.num_programs(1) - 1

    def tile():
        @pl.when(f == 0)
        def _():
            def cast(x_ref):
                xb_ref[...] = x_ref[...].astype(BF16)
            _for_owner(x_refs, in_split, i, cast)
            acc_ref[...] = jnp.zeros_like(acc_ref)

        w1, w3, w2 = (r[...].astype(BF16) for r in (w1_ref, w3_ref, w2_ref))
        for wb_ref, w in zip(wb_refs, (w1, w3, w2)):
            wb_ref[...] = w
        xb = xb_ref[...]
        a = _dot(xb, w1)
        c = _dot(xb, w3)
        h = (a * _sigmoid(a) * c).astype(BF16)
        acc_ref[...] += _dot(h, w2)

        @pl.when(f == last_f)
        def _():
            def finish(x_ref):
                z = alpha * x_ref[...] + 0.5 * acc_ref[...]
                acc_ref[...] = _layer_norm(z, g_ref[...], b_ref[...])
            _for_owner(x_refs, in_split, i, finish)

            def emit(o_ref):
                o_ref[...] = acc_ref[...]
            if len(scratch) == 2:
                _for_owner(o_refs, out_split, i, emit)

    if not n_first:
        tile()
        return
    pl.when(i >= n_first)(tile)

    @pl.when(jnp.logical_and(i < n_first, f == last_f))
    def _():
        o_refs[0][...] = first_ref[...]


def _ffn_ln(xs, w1, w3, w2, g, b, alpha, out_rows):
    d = xs[0].shape[1]
    n = sum(x.shape[0] for x in xs)
    dff = w1.shape[1]
    tm, tf, tf0 = min(TM_FFN, n), min(TF_FFN, dff), min(TF_FFN_FIRST, dff)
    n_first = TILES_FFN_FIRST
    tm0 = n_first * tm
    assert sum(out_rows) == n and dff % tf == 0 and dff % tf0 == 0 and xs[0].shape[0] >= tm0 <= out_rows[0]
    assert all(x.shape[0] % tm == 0 for x in xs) and all(r % tm == 0 for r in out_rows)
    vec = pl.BlockSpec((1, d), lambda i, f: (0, 0))
    once = pl.BlockSpec((tm0, d), lambda i, f: (0, 0), pipeline_mode=pl.Buffered(1))
    scratch = lambda t, own_acc: [pltpu.VMEM((t, d), BF16)] + ([pltpu.VMEM((t, d), F32)] if own_acc else [])
    kern = functools.partial(_ffn_ln_kernel, alpha=alpha)

    w_specs = [pl.BlockSpec((d, tf0), lambda i, f: (0, f)), pl.BlockSpec((d, tf0), lambda i, f: (0, f)),
               pl.BlockSpec((tf0, d), lambda i, f: (f, 0))]
    first, w1b, w3b, w2b = pl.pallas_call(
        functools.partial(kern, n_in=1, n_out=1, in_split=0, out_split=0, n_first=0, emit_w=True),
        out_shape=(jax.ShapeDtypeStruct((tm0, d), F32),
                   *(jax.ShapeDtypeStruct(w.shape, BF16) for w in (w1, w3, w2))),
        grid=(1, dff // tf0),
        in_specs=[once] + w_specs + [vec, vec],
        out_specs=(once, *w_specs),
        scratch_shapes=scratch(tm0, False),
        compiler_params=_params(("arbitrary", "arbitrary")),
        name="ffn_ln_first",
    )(xs[0], w1, w3, w2, g, b)

    wf = lambda i, f: jnp.where(i < n_first, 0, f)
    w_specs = [pl.BlockSpec((d, tf), lambda i, f: (0, wf(i, f))), pl.BlockSpec((d, tf), lambda i, f: (0, wf(i, f))),
               pl.BlockSpec((tf, d), lambda i, f: (wf(i, f), 0))]
    in_specs, in_split = _segment_specs([x.shape[0] for x in xs], tm, d)
    out_specs, out_split = _segment_specs(list(out_rows), tm, d)
    return pl.pallas_call(
        functools.partial(kern, n_in=len(xs), n_out=len(out_rows), in_split=in_split, out_split=out_split,
                          n_first=n_first, emit_w=False),
        out_shape=tuple(jax.ShapeDtypeStruct((r, d), F32) for r in out_rows),
        grid=(n // tm, dff // tf),
        in_specs=in_specs + w_specs + [vec, vec, pl.BlockSpec((tm, d), lambda i, f: (jnp.minimum(i, n_first - 1), 0),
                                                              pipeline_mode=pl.Buffered(1))],
        out_specs=tuple(out_specs),
        scratch_shapes=scratch(tm, len(out_rows) > 1),
        compiler_params=_params(("arbitrary", "arbitrary")),
        name="ffn_ln",
    )(*xs, w1b, w3b, w2b, g, b, first)


def _prep_w_in_kernel(a_ref, nb_ref, if_ref, w_ref, wif_ref, *, n_aligned, n_gate):
    j = pl.program_id(0)
    tw, k_rows = a_ref.shape
    tr = min(TR_PREP, k_rows)

    def emit(rows_of):
        for c0 in range(0, k_rows, tr):
            w_ref[c0:c0 + tr, :] = rows_of(c0).T.astype(BF16)

    @pl.when(j < n_aligned)
    def _():
        emit(lambda c0: a_ref[:, c0:c0 + tr])

    @pl.when(j >= n_aligned)
    def _():
        emit(lambda c0: jnp.concatenate([a_ref[n_gate:tw, c0:c0 + tr], nb_ref[:, c0:c0 + tr]], axis=0))

    @pl.when(j == 0)
    def _():
        for c0 in range(0, k_rows, tr):
            rows = jnp.concatenate([if_ref[:, c0:c0 + tr], jnp.zeros((LANES - n_gate, tr), F32)], axis=0)
            wif_ref[c0:c0 + tr, :] = rows.T.astype(BF16)


def _prep_w_in(w_in_t, layer, n_aligned_cols, n_gate, n_shifted_cols):
    _, n_in, k_rows = w_in_t.shape
    tw = TW_PREP
    assert n_aligned_cols % tw == 0 and n_shifted_cols % tw == 0 and n_gate == SUBLANES
    assert k_rows % min(TR_PREP, k_rows) == 0 and n_aligned_cols + n_gate + n_shifted_cols == n_in
    n_aligned = n_aligned_cols // tw
    n_tiles = n_aligned + n_shifted_cols // tw
    per = tw // n_gate
    return pl.pallas_call(
        functools.partial(_prep_w_in_kernel, n_aligned=n_aligned, n_gate=n_gate),
        out_shape=(jax.ShapeDtypeStruct((k_rows, n_tiles * tw), BF16),
                   jax.ShapeDtypeStruct((k_rows, LANES), BF16)),
        grid=(n_tiles,),
        in_specs=[
            pl.BlockSpec((None, tw, k_rows), lambda j: (layer, j, 0)),
            pl.BlockSpec((None, n_gate, k_rows), lambda j: (layer, (j + 1) * per, 0)),
            pl.BlockSpec((None, n_gate, k_rows), lambda j: (layer, n_aligned * per, 0)),
        ],
        out_specs=(pl.BlockSpec((k_rows, tw), lambda j: (0, j)),
                   pl.BlockSpec((k_rows, LANES), lambda j: (0, 0))),
        compiler_params=_params(("arbitrary",)),
        name="prep_w_in",
    )(w_in_t, w_in_t, w_in_t)


def _conv_tile(ubuf_ref, w_ref, y_ref, u0, out0, rows, width, cs):
    y = None
    for r in range(min(SUBLANES, width)):
        q = None
        for a in range((width - 1 - r) // SUBLANES + 1):
            k = width - 1 - (SUBLANES * a + r)
            term = w_ref[k:k + 1, cs] * ubuf_ref[pl.ds(u0 - SUBLANES * (a + 1), rows + SUBLANES), cs]
            q = term if q is None else q + term
        part = q[SUBLANES - r:SUBLANES - r + rows]
        y = part if y is None else y + part
    y_ref[pl.ds(out0, rows), cs] = y


def _proj_kernel(x_ref, wqk_ref, wv_ref, wo_ref, wla_ref, wlb_ref, wga_ref, wgb_ref, wif_ref, cw_ref, cache_ref,
                 qk_ref, v_ref, ga_ref, gb_ref, yc_ref, if_ref, csp_ref, css_ref,
                 xb_ref, ubuf_ref, carry_ref, tbuf_ref, ybuf_ref, *, k_col0, k_scale, n_prompt_tiles, tiles_per_seq, width, t_dec):
    i = pl.program_id(0)
    j = pl.program_id(1)
    tm, tn = yc_ref.shape
    hist = width - 1

    @pl.when(j == 0)
    def _():
        xb0 = x_ref[...].astype(BF16)
        xb_ref[...] = xb0
        if_ref[...] = _dot(xb0, wif_ref[...])

    def glu(xb):
        return _dot(xb, wla_ref[...]) * _sigmoid(_dot(xb, wlb_ref[...]))

    def dense(xb):
        qk_ref[...] = (_dot(xb, wqk_ref[...]) * jnp.where(j * tn >= k_col0, k_scale, 1.0)).astype(BF16)
        v_ref[...] = _dot(xb, wv_ref[...]).astype(BF16)
        ga_ref[...] = (_sigmoid(_dot(xb, wo_ref[...])) * _sigmoid(_dot(xb, wga_ref[...]))).astype(BF16)
        gb_ref[...] = _sigmoid(_dot(xb, wgb_ref[...])).astype(BF16)

    @pl.when(i < n_prompt_tiles)
    def _():
        xb = xb_ref[...]
        u = glu(xb)
        first = (i % tiles_per_seq) == 0
        ubuf_ref[0:HALO, :] = jnp.where(first, 0.0, carry_ref[j])
        ubuf_ref[HALO:HALO + tm, :] = u
        carry_ref[j] = u[tm - HALO:tm]
        csp_ref[...] = u[tm - hist:tm]
        strip = min(CONV_ROWS, tm)
        for c0 in range(0, tn, LANES):
            for r0 in range(0, tm, strip):
                _conv_tile(ubuf_ref, cw_ref, yc_ref, HALO + r0, r0, strip, width, slice(c0, c0 + LANES))
        dense(xb)

    @pl.when(i >= n_prompt_tiles)
    def _():
        xb = xb_ref[...]
        nb = tm // t_dec
        u = glu(xb)
        for c in range(tn // LANES):
            cs = slice(c * LANES, (c + 1) * LANES)
            tbuf_ref[c] = u[:, cs]
            new = [tbuf_ref[c, pl.ds(t, nb, stride=t_dec), :] for t in range(t_dec)]
            row = lambda p: cache_ref[p, :, cs] if p < hist else new[p - hist]
            for t in range(t_dec):
                acc = cw_ref[0:1, cs] * row(t)
                for k in range(1, width):
                    acc = acc + cw_ref[k:k + 1, cs] * row(t + k)
                ybuf_ref[c, pl.ds(t, nb, stride=t_dec), :] = acc
            yc_ref[:, cs] = ybuf_ref[c]
            for p in range(hist):
                css_ref[p, :, cs] = row(p + t_dec)
        dense(xb)


def _proj(x1, w, wif, cw, cache_t, hqk, k_scale, batch, seq, t_dec):
    n, d = x1.shape
    hist, dec_batch, _ = cache_t.shape
    width = cw.shape[0]
    tm, tn = min(TM_PROJ, seq), min(TN_PROJ, d)
    n_prompt = batch * seq
    assert n_prompt % tm == 0 and (n - n_prompt) % tm == 0 and seq % tm == 0 and d % tn == 0
    assert hqk % tn == 0 and 2 * hqk == d and tn % LANES == 0
    assert hist == width - 1 and SUBLANES * (hist // SUBLANES + 1) <= HALO <= tm and tm % CONV_ROWS == 0
    assert n - n_prompt == dec_batch * t_dec and tm % t_dec == 0 and (tm // t_dec) % SUBLANES == 0 and t_dec <= hist
    npt, tps, nb = n_prompt // tm, seq // tm, tm // t_dec
    per = d // tn
    col = pl.BlockSpec((tm, tn), lambda i, j: (i, j))
    wspec = lambda g: pl.BlockSpec((d, tn), lambda i, j: (0, g * per + j))
    smp_i = lambda i: jnp.maximum(i - npt, 0)
    smp_j = lambda i, j: jnp.where(i >= npt, j, 0)
    prm_i = lambda i: jnp.minimum(i, npt - 1)
    prm_j = lambda i, j: jnp.where(i < npt, j, per - 1)
    return pl.pallas_call(
        functools.partial(_proj_kernel, k_col0=hqk, k_scale=k_scale, n_prompt_tiles=npt, tiles_per_seq=tps,
                          width=width, t_dec=t_dec),
        out_shape=(
            jax.ShapeDtypeStruct((n, d), BF16),
            jax.ShapeDtypeStruct((n, d), BF16),
            jax.ShapeDtypeStruct((n, d), BF16),
            jax.ShapeDtypeStruct((n, d), BF16),
            jax.ShapeDtypeStruct((n, d), F32),
            jax.ShapeDtypeStruct((n, LANES), F32),
            jax.ShapeDtypeStruct((npt, hist, d), F32),
            jax.ShapeDtypeStruct((hist, dec_batch, d), F32),
        ),
        grid=(n // tm, d // tn),
        in_specs=[pl.BlockSpec((tm, d), lambda i, j: (i, 0)),
                  wspec(0), wspec(1), wspec(2), wspec(3), wspec(4), wspec(5), wspec(6),
                  pl.BlockSpec((d, LANES), lambda i, j: (0, 0)),
                  pl.BlockSpec((width, tn), lambda i, j: (0, j)),
                  pl.BlockSpec((hist, nb, tn), lambda i, j: (0, smp_i(i), smp_j(i, j)))],
        out_specs=(col, col, col, col, col, pl.BlockSpec((tm, LANES), lambda i, j: (i, 0)),
                   pl.BlockSpec((None, hist, tn), lambda i, j: (prm_i(i), 0, prm_j(i, j))),
                   pl.BlockSpec((hist, nb, tn), lambda i, j: (0, smp_i(i), smp_j(i, j)))),
        scratch_shapes=[pltpu.VMEM((tm, d), BF16), pltpu.VMEM((HALO + tm, tn), F32),
                        pltpu.VMEM((per, HALO, tn), F32),
                        pltpu.VMEM((tn // LANES, tm, LANES), F32), pltpu.VMEM((tn // LANES, tm, LANES), F32)],
        compiler_params=_params(("arbitrary", "arbitrary")),
        name="proj",
    )(x1, w, w, w, w, w, w, w, wif, cw, cache_t)


def _mlstm_chunks(q, k, v, ig_c, lf_c, c0, n0, m0, last):
    G, L, _ = q.shape
    row = lax.broadcasted_iota(jnp.int32, (L, L), 0)
    colm = lax.broadcasted_iota(jnp.int32, (L, L), 1)
    causal = (colm <= row)[None]
    eye = (colm == row)[None]
    upper = (row <= colm)[None]
    b_r = jnp.sum(jnp.where(upper, lf_c, 0.0), axis=1, keepdims=True)
    b_c = jnp.sum(jnp.where(eye, b_r, 0.0), axis=2, keepdims=True)
    ig_r = jnp.sum(jnp.where(eye, ig_c, 0.0), axis=1, keepdims=True)

    log_d = jnp.where(causal, b_c - b_r + ig_r, -jnp.inf)
    inter = b_c + m0
    m_t = jnp.maximum(inter, jnp.max(log_d, axis=2, keepdims=True))
    d = jnp.exp(log_d - m_t)
    w_inter = jnp.exp(inter - m_t)
    per_group = lambda fn: jnp.stack([fn(g) for g in range(G)])
    nt = (((1,), (1,)), ((), ()))
    tn = (((0,), (0,)), ((), ()))
    s = per_group(lambda g: lax.dot_general(q[g], k[g], nt, preferred_element_type=F32)) * d
    sb = s.astype(BF16)
    cb = c0.astype(BF16)
    num = per_group(lambda g: _dot(sb[g], v[g])) + w_inter * per_group(lambda g: _dot(q[g], cb[g]))
    qn = jnp.sum(s, axis=2, keepdims=True) + w_inter * jnp.sum(q.astype(F32) * n0, axis=2, keepdims=True)
    den = jnp.maximum(jnp.abs(qn), jnp.exp(-m_t))
    h = num * (1.0 / den)

    m_new = m_t[:, last:last + 1, :]
    w_k = jnp.exp(b_c[:, last:last + 1, :] - b_c + ig_c - m_new)
    w_c = jnp.exp(inter[:, last:last + 1, :] - m_new)
    kw = k.astype(F32) * w_k
    kwb = kw.astype(BF16)
    c_new = w_c * c0 + per_group(lambda g: lax.dot_general(kwb[g], v[g], tn, preferred_element_type=F32))
    n_new = w_c * n0 + jnp.sum(kw, axis=1, keepdims=True)
    return h, c_new, n_new, m_new


def _head_norm(h, g):
    mu = jnp.mean(h, axis=-1, keepdims=True)
    hc = h - mu
    var = jnp.mean(hc * hc, axis=-1, keepdims=True)
    return hc * lax.rsqrt(var + LN_EPS) * g


def _log_sigmoid(x):
    return jnp.minimum(x, 0.0) - jnp.log1p(jnp.exp(-jnp.abs(x)))


def _mlstm_prompt_step(c, qk_ref, v_ref, if_ref, bias_ref, g_ref, hn_ref, c_ref, n_ref, m_ref, n_heads, d_qk, d_v):
    @pl.when(c == 0)
    def _():
        c_ref[...] = jnp.zeros_like(c_ref)
        n_ref[...] = jnp.zeros_like(n_ref)
        m_ref[...] = jnp.zeros_like(m_ref)

    gates = if_ref[...] + bias_ref[...]
    lf_all = _log_sigmoid(gates)
    L = gates.shape[0]
    heads = range(n_heads)
    q = jnp.stack([qk_ref[:, h * d_qk:(h + 1) * d_qk] for h in heads])
    k = jnp.stack([qk_ref[:, (n_heads + h) * d_qk:(n_heads + h + 1) * d_qk] for h in heads])
    v = jnp.stack([v_ref[:, h * d_v:(h + 1) * d_v] for h in heads])
    ig_c = jnp.stack([gates[:, h:h + 1] for h in heads])
    lf_c = jnp.stack([lf_all[:, n_heads + h:n_heads + h + 1] for h in heads])
    hh, c_new, n_new, m_new = _mlstm_chunks(q, k, v, ig_c, lf_c, c_ref[0], n_ref[0], m_ref[0], L - 1)
    c_ref[0] = c_new
    n_ref[0] = n_new
    m_ref[0] = m_new
    g = jnp.stack([g_ref[:, h * d_v:(h + 1) * d_v] for h in heads])
    hn = _head_norm(hh, g).astype(BF16)
    for h in heads:
        hn_ref[:, h * d_v:(h + 1) * d_v] = hn[h]


def _mlstm_sample_step(qk_ref, v_ref, if_ref, bias_ref, g_ref, c0_ref, n0_ref, m0_ref, hn_ref, c_ref, n_ref, m_ref,
                       n_heads, d_qk, d_v, t_dec, bs):
    pad = (-t_dec) % (2 * SUBLANES)

    def zpad_rows(x):
        x = x.reshape(bs, t_dec, x.shape[1])
        return jnp.concatenate([x, jnp.zeros((bs, pad, x.shape[2]), x.dtype)], axis=1) if pad else x

    qk_all = zpad_rows(qk_ref[...].astype(F32))
    v_all = zpad_rows(v_ref[...].astype(F32))
    gates = zpad_rows(if_ref[...] + bias_ref[...])
    lf_all = _log_sigmoid(gates)
    groups = [(b, h) for b in range(bs) for h in range(n_heads)]
    q = jnp.stack([qk_all[b][:, h * d_qk:(h + 1) * d_qk] for b, h in groups]).astype(BF16)
    k = jnp.stack([qk_all[b][:, (n_heads + h) * d_qk:(n_heads + h + 1) * d_qk] for b, h in groups]).astype(BF16)
    v = jnp.stack([v_all[b][:, h * d_v:(h + 1) * d_v] for b, h in groups]).astype(BF16)
    ig_c = jnp.stack([gates[b][:, h:h + 1] for b, h in groups])
    lf_c = jnp.stack([lf_all[b][:, n_heads + h:n_heads + h + 1] for b, h in groups])
    flat = lambda r: r[...].reshape((bs * n_heads,) + r.shape[2:])
    hh, c_new, n_new, m_new = _mlstm_chunks(q, k, v, ig_c, lf_c, flat(c0_ref), flat(n0_ref), flat(m0_ref), t_dec - 1)
    c_ref[...] = c_new.reshape(c_ref.shape)
    n_ref[...] = n_new.reshape(n_ref.shape)
    m_ref[...] = m_new.reshape(m_ref.shape)
    g = jnp.stack([g_ref[:, h * d_v:(h + 1) * d_v] for _, h in groups])
    hn = _head_norm(hh[:, :t_dec], g).astype(BF16)
    for i, (b, h) in enumerate(groups):
        hn_ref[b * t_dec:(b + 1) * t_dec, h * d_v:(h + 1) * d_v] = hn[i]


def _mlstm_kernel(qkp_ref, vp_ref, ifp_ref, qks_ref, vs_ref, ifs_ref, bias_ref, g_ref, c0_ref, n0_ref, m0_ref,
                  hnp_ref, cp_ref, np_ref, mp_ref, hns_ref, cs_ref, ns_ref, ms_ref,
                  *, n_chunks, n_heads, d_qk, d_v, t_dec, bs):
    _mlstm_prompt_step(pl.program_id(0) % n_chunks, qkp_ref, vp_ref, ifp_ref, bias_ref, g_ref,
                       hnp_ref, cp_ref, np_ref, mp_ref, n_heads, d_qk, d_v)
    _mlstm_sample_step(qks_ref, vs_ref, ifs_ref, bias_ref, g_ref, c0_ref, n0_ref, m0_ref,
                       hns_ref, cs_ref, ns_ref, ms_ref, n_heads, d_qk, d_v, t_dec, bs)


def _mlstm(qk, v, iff, bias, g, c0, n0, m0, batch, seq, t_dec):
    dec_batch, n_heads, d_qk, d_v = c0.shape
    d = v.shape[1]
    L = min(CHUNK, seq)
    assert seq % L == 0
    nc = seq // L
    steps = batch * nc
    assert dec_batch % steps == 0
    bs = dec_batch // steps
    rows = bs * t_dec
    n_prompt = batch * seq
    assert rows % (2 * SUBLANES) == 0 and n_prompt % rows == 0
    blk0 = n_prompt // rows
    ptok = lambda w: pl.BlockSpec((L, w), lambda i: (i, 0))
    stok = lambda w: pl.BlockSpec((rows, w), lambda i: (blk0 + i, 0))
    pst = lambda *tail: pl.BlockSpec((1, n_heads) + tail, lambda i: (i // nc, 0, 0, 0))
    sst = lambda *tail: pl.BlockSpec((bs, n_heads) + tail, lambda i: (i, 0, 0, 0))
    state_shapes = lambda nb: (jax.ShapeDtypeStruct((nb, n_heads, d_qk, d_v), F32),
                               jax.ShapeDtypeStruct((nb, n_heads, 1, d_qk), F32),
                               jax.ShapeDtypeStruct((nb, n_heads, 1, 1), F32))
    return pl.pallas_call(
        functools.partial(_mlstm_kernel, n_chunks=nc, n_heads=n_heads, d_qk=d_qk, d_v=d_v, t_dec=t_dec, bs=bs),
        out_shape=(jax.ShapeDtypeStruct((n_prompt, d), BF16), *state_shapes(batch),
                   jax.ShapeDtypeStruct((dec_batch * t_dec, d), BF16), *state_shapes(dec_batch)),
        grid=(steps,),
        in_specs=[ptok(qk.shape[1]), ptok(d), ptok(LANES), stok(qk.shape[1]), stok(d), stok(LANES),
                  pl.BlockSpec((1, LANES), lambda i: (0, 0)), pl.BlockSpec((1, d), lambda i: (0, 0)),
                  sst(d_qk, d_v), sst(1, d_qk), sst(1, 1)],
        out_specs=(ptok(d), pst(d_qk, d_v), pst(1, d_qk), pst(1, 1),
                   pl.BlockSpec((rows, d), lambda i: (i, 0)), sst(d_qk, d_v), sst(1, d_qk), sst(1, 1)),
        compiler_params=_params(("arbitrary",)),
        name="mlstm",
    )(qk, v, iff, qk, v, iff, bias, g, c0, n0, m0)


def _mix_ln_kernel(yc_ref, ga_ref, gb_ref, hnp_ref, hns_ref, x1_ref, cb_ref, cg_ref, cbb_ref, w_ref, g_ref, b_ref,
                   o_ref, *, alpha, split):
    def body(hn_ref):
        z = _layer_norm(yc_ref[...] + cb_ref[...], cg_ref[...], cbb_ref[...])
        hb = z * _sigmoid(z)
        mixin = ga_ref[...].astype(F32) * hn_ref[...].astype(F32) + gb_ref[...].astype(F32) * hb
        mix = _dot(mixin.astype(BF16), w_ref[...])
        o_ref[...] = _layer_norm(alpha * x1_ref[...] + mix, g_ref[...], b_ref[...])

    _for_owner((hnp_ref, hns_ref), split, pl.program_id(0), body)


def _mix_ln(yc, ga, gb, hn_p, hn_s, x1, cb, cg, cbb, w, g, b, alpha):
    n, d = x1.shape
    tm = min(TM_MIX, hn_s.shape[0])
    assert hn_p.shape[0] % tm == 0 and hn_s.shape[0] % tm == 0 and hn_p.shape[0] + hn_s.shape[0] == n
    tok = pl.BlockSpec((tm, d), lambda i: (i, 0))
    vec = pl.BlockSpec((1, d), lambda i: (0, 0))
    hn_specs, split = _segment_specs([hn_p.shape[0], hn_s.shape[0]], tm, d)
    return pl.pallas_call(
        functools.partial(_mix_ln_kernel, alpha=alpha, split=split),
        out_shape=jax.ShapeDtypeStruct((n, d), F32),
        grid=(n // tm,),
        in_specs=[tok, tok, tok, *hn_specs, tok, vec, vec, vec, pl.BlockSpec((d, d), lambda i: (0, 0)), vec, vec],
        out_specs=tok,
        compiler_params=_params(("arbitrary",)),
        name="mix_ln",
    )(yc, ga, gb, hn_p, hn_s, x1, cb, cg, cbb, w, g, b)


def _layer(xs, layer, batch, seq, dec_batch, t_dec, c0, n0, m0, cache, w_in, p, alpha):
    (ffn1_w1, ffn1_w3, ffn1_w2, ln1_g, ln1_b, b_igate, b_fgate, mh_norm_g,
     conv_w, conv_b, conv_ln_g, conv_ln_b, w_out, ln2_g, ln2_b,
     ffn2_w1, ffn2_w3, ffn2_w2, ln3_g, ln3_b) = p
    d = xs[0].shape[1]
    n_heads, d_qk, d_v = c0.shape[1:]
    n_prompt, n_sample = batch * seq, dec_batch * t_dec
    hqk, dm = n_heads * d_qk, n_heads * d_v
    assert 2 * hqk == d and dm == d and conv_w.shape[1] == d
    vec = lambda a: a.reshape(1, -1).astype(F32)
    bf = lambda a: a.astype(BF16)

    (x1,) = _ffn_ln(xs, ffn1_w1, ffn1_w3, ffn1_w2, vec(ln1_g), vec(ln1_b), alpha, (n_prompt + n_sample,))
    w, wif = _prep_w_in(jnp.swapaxes(w_in, 1, 2), layer, 2 * hqk + 2 * dm, 2 * n_heads, 4 * d)
    qk, v, ga, gb, yc, iff, tile_tails, conv_s_t = _proj(x1, w, wif, conv_w.astype(F32), jnp.swapaxes(cache, 0, 1),
                                                         hqk, float(d_qk) ** -0.5, batch, seq, t_dec)

    gate_bias = jnp.pad(jnp.concatenate([b_igate, b_fgate]).astype(F32), (0, LANES - 2 * n_heads)).reshape(1, LANES)
    mh_g = vec(mh_norm_g)
    hn_p, c_p, n_p, m_p, hn_s, c_s, n_s, m_s = _mlstm(qk, v, iff, gate_bias, mh_g, c0,
                                                      n0.reshape(dec_batch, n_heads, 1, d_qk),
                                                      m0.reshape(dec_batch, n_heads, 1, 1), batch, seq, t_dec)

    x2 = _mix_ln(yc, ga, gb, hn_p, hn_s, x1, vec(conv_b), vec(conv_ln_g), vec(conv_ln_b), bf(w_out),
                 vec(ln2_g), vec(ln2_b), alpha)
    y_p, y_s = _ffn_ln((x2,), ffn2_w1, ffn2_w3, ffn2_w2, vec(ln3_g), vec(ln3_b), alpha, (n_prompt, n_sample))
    tiles_per_seq = tile_tails.shape[0] // batch
    conv_p = tile_tails[tiles_per_seq - 1::tiles_per_seq]
    states_p = (c_p, n_p.reshape(batch, n_heads, d_qk), m_p.reshape(batch, n_heads), conv_p)
    states_s = (c_s, n_s.reshape(dec_batch, n_heads, d_qk), m_s.reshape(dec_batch, n_heads),
                jnp.swapaxes(conv_s_t, 0, 1))
    return (y_p, y_s), states_p, states_s


def kernel(x_prompt, x_sample, state_C, state_n, state_m, cache_conv, ffn1_w1, ffn1_w3, ffn1_w2, ln1_g, ln1_b, w_in, b_igate, b_fgate, mh_norm_g, conv_w, conv_b, conv_ln_g, conv_ln_b, w_out, ln2_g, ln2_b, ffn2_w1, ffn2_w3, ffn2_w2, ln3_g, ln3_b):
    batch, seq, d = x_prompt.shape
    dec_batch, t_dec, _ = x_sample.shape
    depth = ffn1_w1.shape[0]
    alpha = (2.0 * depth) ** 0.25
    xs = (x_prompt.reshape(batch * seq, d), x_sample.reshape(dec_batch * t_dec, d))
    weights = (ffn1_w1, ffn1_w3, ffn1_w2, ln1_g, ln1_b, b_igate, b_fgate, mh_norm_g,
               conv_w, conv_b, conv_ln_g, conv_ln_b, w_out, ln2_g, ln2_b,
               ffn2_w1, ffn2_w3, ffn2_w2, ln3_g, ln3_b)
    outs_p, outs_s = [], []
    for l in range(depth):
        p = tuple(wt[l] for wt in weights)
        xs, st_p, st_s = _layer(xs, l, batch, seq, dec_batch, t_dec, state_C[l], state_n[l], state_m[l],
                                cache_conv[l], w_in, p, alpha)
        outs_p.append(st_p)
        outs_s.append(st_s)
    y_p = xs[0].reshape(batch, seq, d)
    y_s = xs[1].reshape(dec_batch, t_dec, d)
    stack = lambda outs, i: jnp.stack([o[i] for o in outs])
    return (y_p, y_s,
            stack(outs_p, 0), stack(outs_p, 1), stack(outs_p, 2), stack(outs_p, 3),
            stack(outs_s, 0), stack(outs_s, 1), stack(outs_s, 2), stack(outs_s, 3))
```

```python
import functools

import jax
import jax.numpy as jnp
from jax import lax
from jax.experimental import pallas as pl
from jax.experimental.pallas import tpu as pltpu

F32 = jnp.float32
BF16 = jnp.bfloat16
LN_EPS = 1e-5
LANES = 128
SUBLANES = 8
HALO = 32
VMEM_LIMIT = 56 * 1024 * 1024

TM_FFN = 512
TF_FFN = 512
TF_FFN_FIRST = 256
TILES_FFN_FIRST = 2
TM_PROJ = 512
TN_PROJ = 256
TW_PREP = 512
TR_PREP = 512
TM_MIX = 256
CONV_ROWS = 128
CHUNK = 128


def _params(sem):
    return pltpu.CompilerParams(dimension_semantics=sem, vmem_limit_bytes=VMEM_LIMIT)


def _sigmoid(x):
    return jax.nn.sigmoid(x)


def _layer_norm(z, g, b):
    mu = jnp.mean(z, axis=-1, keepdims=True)
    zc = z - mu
    var = jnp.mean(zc * zc, axis=-1, keepdims=True)
    return zc * lax.rsqrt(var + LN_EPS) * g + b


def _dot(a, b):
    return jnp.dot(a, b, preferred_element_type=F32)


def _for_owner(refs, split, i, fn):
    if len(refs) == 1:
        fn(refs[0])
        return
    pl.when(i < split)(lambda: fn(refs[0]))
    pl.when(i >= split)(lambda: fn(refs[1]))


def _segment_specs(arrays_rows, tm, d):
    if len(arrays_rows) == 1:
        return [pl.BlockSpec((tm, d), lambda i, *_: (i, 0))], 0
    split = arrays_rows[0] // tm
    last0 = split - 1
    return [pl.BlockSpec((tm, d), lambda i, *_: (jnp.minimum(i, last0), 0)),
            pl.BlockSpec((tm, d), lambda i, *_: (jnp.maximum(i - split, 0), 0))], split


def _ffn_ln_kernel(*refs, alpha, n_in, n_out, in_split, out_split, n_first, emit_w):
    x_refs = refs[:n_in]
    w1_ref, w3_ref, w2_ref, g_ref, b_ref = refs[n_in:n_in + 5]
    pos = n_in + 5
    first_ref = refs[pos] if n_first else None
    pos += bool(n_first)
    o_refs = refs[pos:pos + n_out]
    pos += n_out
    wb_refs = refs[pos:pos + 3] if emit_w else ()
    scratch = refs[pos + len(wb_refs):]
    xb_ref, acc_ref = scratch if len(scratch) == 2 else (scratch[0], o_refs[0])
    i = pl.program_id(0)
    f = pl.program_id(1)
    last_f = pl.num_programs(1) - 1

    def tile():
        @pl.when(f == 0)
        def _():
            def cast(x_ref):
                xb_ref[...] = x_ref[...].astype(BF16)
            _for_owner(x_refs, in_split, i, cast)
            acc_ref[...] = jnp.zeros_like(acc_ref)

        w1, w3, w2 = (r[...].astype(BF16) for r in (w1_ref, w3_ref, w2_ref))
        for wb_ref, w in zip(wb_refs, (w1, w3, w2)):
            wb_ref[...] = w
        xb = xb_ref[...]
        a = _dot(xb, w1)
        c = _dot(xb, w3)
        h = (a * _sigmoid(a) * c).astype(BF16)
        acc_ref[...] += _dot(h, w2)

        @pl.when(f == last_f)
        def _():
            def finish(x_ref):
                z = alpha * x_ref[...] + 0.5 * acc_ref[...]
                acc_ref[...] = _layer_norm(z, g_ref[...], b_ref[...])
            _for_owner(x_refs, in_split, i, finish)

            def emit(o_ref):
                o_ref[...] = acc_ref[...]
            if len(scratch) == 2:
                _for_owner(o_refs, out_split, i, emit)

    if not n_first:
        tile()
        return
    pl.when(i >= n_first)(tile)

    @pl.when(jnp.logical_and(i < n_first, f == last_f))
    def _():
        o_refs[0][...] = first_ref[...]


def _ffn_ln(xs, w1, w3, w2, g, b, alpha, out_rows):
    d = xs[0].shape[1]
    n = sum(x.shape[0] for x in xs)
    dff = w1.shape[1]
    tm, tf, tf0 = min(TM_FFN, n), min(TF_FFN, dff), min(TF_FFN_FIRST, dff)
    n_first = TILES_FFN_FIRST
    tm0 = n_first * tm
    assert sum(out_rows) == n and dff % tf == 0 and dff % tf0 == 0 and xs[0].shape[0] >= tm0 <= out_rows[0]
    assert all(x.shape[0] % tm == 0 for x in xs) and all(r % tm == 0 for r in out_rows)
    vec = pl.BlockSpec((1, d), lambda i, f: (0, 0))
    once = pl.BlockSpec((tm0, d), lambda i, f: (0, 0), pipeline_mode=pl.Buffered(1))
    scratch = lambda t, own_acc: [pltpu.VMEM((t, d), BF16)] + ([pltpu.VMEM((t, d), F32)] if own_acc else [])
    kern = functools.partial(_ffn_ln_kernel, alpha=alpha)

    w_specs = [pl.BlockSpec((d, tf0), lambda i, f: (0, f)), pl.BlockSpec((d, tf0), lambda i, f: (0, f)),
               pl.BlockSpec((tf0, d), lambda i, f: (f, 0))]
    first, w1b, w3b, w2b = pl.pallas_call(
        functools.partial(kern, n_in=1, n_out=1, in_split=0, out_split=0, n_first=0, emit_w=True),
        out_shape=(jax.ShapeDtypeStruct((tm0, d), F32),
                   *(jax.ShapeDtypeStruct(w.shape, BF16) for w in (w1, w3, w2))),
        grid=(1, dff // tf0),
        in_specs=[once] + w_specs + [vec, vec],
        out_specs=(once, *w_specs),
        scratch_shapes=scratch(tm0, False),
        compiler_params=_params(("arbitrary", "arbitrary")),
        name="ffn_ln_first",
    )(xs[0], w1, w3, w2, g, b)

    wf = lambda i, f: jnp.where(i < n_first, 0, f)
    w_specs = [pl.BlockSpec((d, tf), lambda i, f: (0, wf(i, f))), pl.BlockSpec((d, tf), lambda i, f: (0, wf(i, f))),
               pl.BlockSpec((tf, d), lambda i, f: (wf(i, f), 0))]
    in_specs, in_split = _segment_specs([x.shape[0] for x in xs], tm, d)
    out_specs, out_split = _segment_specs(list(out_rows), tm, d)
    return pl.pallas_call(
        functools.partial(kern, n_in=len(xs), n_out=len(out_rows), in_split=in_split, out_split=out_split,
                          n_first=n_first, emit_w=False),
        out_shape=tuple(jax.ShapeDtypeStruct((r, d), F32) for r in out_rows),
        grid=(n // tm, dff // tf),
        in_specs=in_specs + w_specs + [vec, vec, pl.BlockSpec((tm, d), lambda i, f: (jnp.minimum(i, n_first - 1), 0),
                                                              pipeline_mode=pl.Buffered(1))],
        out_specs=tuple(out_specs),
        scratch_shapes=scratch(tm, len(out_rows) > 1),
        compiler_params=_params(("arbitrary", "arbitrary")),
        name="ffn_ln",
    )(*xs, w1b, w3b, w2b, g, b, first)


def _prep_w_in_kernel(a_ref, nb_ref, if_ref, w_ref, wif_ref, *, n_aligned, n_gate):
    j = pl.program_id(0)
    tw, k_rows = a_ref.shape
    tr = min(TR_PREP, k_rows)

    def emit(rows_of):
        for c0 in range(0, k_rows, tr):
            w_ref[c0:c0 + tr, :] = rows_of(c0).T.astype(BF16)

    @pl.when(j < n_aligned)
    def _():
        emit(lambda c0: a_ref[:, c0:c0 + tr])

    @pl.when(j >= n_aligned)
    def _():
        emit(lambda c0: jnp.concatenate([a_ref[n_gate:tw, c0:c0 + tr], nb_ref[:, c0:c0 + tr]], axis=0))

    @pl.when(j == 0)
    def _():
        for c0 in range(0, k_rows, tr):
            rows = jnp.concatenate([if_ref[:, c0:c0 + tr], jnp.zeros((LANES - n_gate, tr), F32)], axis=0)
            wif_ref[c0:c0 + tr, :] = rows.T.astype(BF16)


def _prep_w_in(w_in_t, layer, n_aligned_cols, n_gate, n_shifted_cols):
    _, n_in, k_rows = w_in_t.shape
    tw = TW_PREP
    assert n_aligned_cols % tw == 0 and n_shifted_cols % tw == 0 and n_gate == SUBLANES
    assert k_rows % min(TR_PREP, k_rows) == 0 and n_aligned_cols + n_gate + n_shifted_cols == n_in
    n_aligned = n_aligned_cols // tw
    n_tiles = n_aligned + n_shifted_cols // tw
    per = tw // n_gate
    return pl.pallas_call(
        functools.partial(_prep_w_in_kernel, n_aligned=n_aligned, n_gate=n_gate),
        out_shape=(jax.ShapeDtypeStruct((k_rows, n_tiles * tw), BF16),
                   jax.ShapeDtypeStruct((k_rows, LANES), BF16)),
        grid=(n_tiles,),
        in_specs=[
            pl.BlockSpec((None, tw, k_rows), lambda j: (layer, j, 0)),
            pl.BlockSpec((None, n_gate, k_rows), lambda j: (layer, (j + 1) * per, 0)),
            pl.BlockSpec((None, n_gate, k_rows), lambda j: (layer, n_aligned * per, 0)),
        ],
        out_specs=(pl.BlockSpec((k_rows, tw), lambda j: (0, j)),
                   pl.BlockSpec((k_rows, LANES), lambda j: (0, 0))),
        compiler_params=_params(("arbitrary",)),
        name="prep_w_in",
    )(w_in_t, w_in_t, w_in_t)


def _conv_tile(ubuf_ref, w_ref, y_ref, u0, out0, rows, width, cs):
    y = None
    for r in range(min(SUBLANES, width)):
        q = None
        for a in range((width - 1 - r) // SUBLANES + 1):
            k = width - 1 - (SUBLANES * a + r)
            term = w_ref[k:k + 1, cs] * ubuf_ref[pl.ds(u0 - SUBLANES * (a + 1), rows + SUBLANES), cs]
            q = term if q is None else q + term
        part = q[SUBLANES - r:SUBLANES - r + rows]
        y = part if y is None else y + part
    y_ref[pl.ds(out0, rows), cs] = y


def _proj_kernel(x_ref, wqk_ref, wv_ref, wo_ref, wla_ref, wlb_ref, wga_ref, wgb_ref, wif_ref, cw_ref, cache_ref,
                 qk_ref, v_ref, ga_ref, gb_ref, yc_ref, if_ref, csp_ref, css_ref,
                 xb_ref, ubuf_ref, carry_ref, tbuf_ref, ybuf_ref, *, k_col0, k_scale, n_prompt_tiles, tiles_per_seq, width, t_dec):
    i = pl.program_id(0)
    j = pl.program_id(1)
    tm, tn = yc_ref.shape
    hist = width - 1

    @pl.when(j == 0)
    def _():
        xb0 = x_ref[...].astype(BF16)
        xb_ref[...] = xb0
        if_ref[...] = _dot(xb0, wif_ref[...])

    def glu(xb):
        return _dot(xb, wla_ref[...]) * _sigmoid(_dot(xb, wlb_ref[...]))

    def dense(xb):
        qk_ref[...] = (_dot(xb, wqk_ref[...]) * jnp.where(j * tn >= k_col0, k_scale, 1.0)).astype(BF16)
        v_ref[...] = _dot(xb, wv_ref[...]).astype(BF16)
        ga_ref[...] = (_sigmoid(_dot(xb, wo_ref[...])) * _sigmoid(_dot(xb, wga_ref[...]))).astype(BF16)
        gb_ref[...] = _sigmoid(_dot(xb, wgb_ref[...])).astype(BF16)

    @pl.when(i < n_prompt_tiles)
    def _():
        xb = xb_ref[...]
        u = glu(xb)
        first = (i % tiles_per_seq) == 0
        ubuf_ref[0:HALO, :] = jnp.where(first, 0.0, carry_ref[j])
        ubuf_ref[HALO:HALO + tm, :] = u
        carry_ref[j] = u[tm - HALO:tm]
        csp_ref[...] = u[tm - hist:tm]
        strip = min(CONV_ROWS, tm)
        for c0 in range(0, tn, LANES):
            for r0 in range(0, tm, strip):
                _conv_tile(ubuf_ref, cw_ref, yc_ref, HALO + r0, r0, strip, width, slice(c0, c0 + LANES))
        dense(xb)

    @pl.when(i >= n_prompt_tiles)
    def _():
        xb = xb_ref[...]
        nb = tm // t_dec
        u = glu(xb)
        for c in range(tn // LANES):
            cs = slice(c * LANES, (c + 1) * LANES)
            tbuf_ref[c] = u[:, cs]
            new = [tbuf_ref[c, pl.ds(t, nb, stride=t_dec), :] for t in range(t_dec)]
            row = lambda p: cache_ref[p, :, cs] if p < hist else new[p - hist]
            for t in range(t_dec):
                acc = cw_ref[0:1, cs] * row(t)
                for k in range(1, width):
                    acc = acc + cw_ref[k:k + 1, cs] * row(t + k)
                ybuf_ref[c, pl.ds(t, nb, stride=t_dec), :] = acc
            yc_ref[:, cs] = ybuf_ref[c]
            for p in range(hist):
                css_ref[p, :, cs] = row(p + t_dec)
        dense(xb)


def _proj(x1, w, wif, cw, cache_t, hqk, k_scale, batch, seq, t_dec):
    n, d = x1.shape
    hist, dec_batch, _ = cache_t.shape
    width = cw.shape[0]
    tm, tn = min(TM_PROJ, seq), min(TN_PROJ, d)
    n_prompt = batch * seq
    assert n_prompt % tm == 0 and (n - n_prompt) % tm == 0 and seq % tm == 0 and d % tn == 0
    assert hqk % tn == 0 and 2 * hqk == d and tn % LANES == 0
    assert hist == width - 1 and SUBLANES * (hist // SUBLANES + 1) <= HALO <= tm and tm % CONV_ROWS == 0
    assert n - n_prompt == dec_batch * t_dec and tm % t_dec == 0 and (tm // t_dec) % SUBLANES == 0 and t_dec <= hist
    npt, tps, nb = n_prompt // tm, seq // tm, tm // t_dec
    per = d // tn
    col = pl.BlockSpec((tm, tn), lambda i, j: (i, j))
    wspec = lambda g: pl.BlockSpec((d, tn), lambda i, j: (0, g * per + j))
    smp_i = lambda i: jnp.maximum(i - npt, 0)
    smp_j = lambda i, j: jnp.where(i >= npt, j, 0)
    prm_i = lambda i: jnp.minimum(i, npt - 1)
    prm_j = lambda i, j: jnp.where(i < npt, j, per - 1)
    return pl.pallas_call(
        functools.partial(_proj_kernel, k_col0=hqk, k_scale=k_scale, n_prompt_tiles=npt, tiles_per_seq=tps,
                          width=width, t_dec=t_dec),
        out_shape=(
            jax.ShapeDtypeStruct((n, d), BF16),
            jax.ShapeDtypeStruct((n, d), BF16),
            jax.ShapeDtypeStruct((n, d), BF16),
            jax.ShapeDtypeStruct((n, d), BF16),
            jax.ShapeDtypeStruct((n, d), F32),
            jax.ShapeDtypeStruct((n, LANES), F32),
            jax.ShapeDtypeStruct((npt, hist, d), F32),
            jax.ShapeDtypeStruct((hist, dec_batch, d), F32),
        ),
        grid=(n // tm, d // tn),
        in_specs=[pl.BlockSpec((tm, d), lambda i, j: (i, 0)),
                  wspec(0), wspec(1), wspec(2), wspec(3), wspec(4), wspec(5), wspec(6),
                  pl.BlockSpec((d, LANES), lambda i, j: (0, 0)),
                  pl.BlockSpec((width, tn), lambda i, j: (0, j)),
                  pl.BlockSpec((hist, nb, tn), lambda i, j: (0, smp_i(i), smp_j(i, j)))],
        out_specs=(col, col, col, col, col, pl.BlockSpec((tm, LANES), lambda i, j: (i, 0)),
                   pl.BlockSpec((None, hist, tn), lambda i, j: (prm_i(i), 0, prm_j(i, j))),
                   pl.BlockSpec((hist, nb, tn), lambda i, j: (0, smp_i(i), smp_j(i, j)))),
        scratch_shapes=[pltpu.VMEM((tm, d), BF16), pltpu.VMEM((HALO + tm, tn), F32),
                        pltpu.VMEM((per, HALO, tn), F32),
                        pltpu.VMEM((tn // LANES, tm, LANES), F32), pltpu.VMEM((tn // LANES, tm, LANES), F32)],
        compiler_params=_params(("arbitrary", "arbitrary")),
        name="proj",
    )(x1, w, w, w, w, w, w, w, wif, cw, cache_t)


def _mlstm_chunks(q, k, v, ig_c, lf_c, c0, n0, m0, gain, last):
    G, L, _ = q.shape
    row = lax.broadcasted_iota(jnp.int32, (L, L), 0)
    colm = lax.broadcasted_iota(jnp.int32, (L, L), 1)
    causal = (colm <= row)[None]
    eye = (colm == row)[None]
    upper = (row <= colm)[None]
    b_r = jnp.sum(jnp.where(upper, lf_c, 0.0), axis=1, keepdims=True)
    b_c = jnp.sum(jnp.where(eye, b_r, 0.0), axis=2, keepdims=True)
    ig_r = jnp.sum(jnp.where(eye, ig_c, 0.0), axis=1, keepdims=True)

    log_d = jnp.where(causal, b_c - b_r + ig_r, -jnp.inf)
    inter = b_c + m0
    m_t = jnp.maximum(inter, jnp.max(log_d, axis=2, keepdims=True))
    d = jnp.exp(log_d - m_t)
    w_inter = jnp.exp(inter - m_t)
    per_group = lambda fn: jnp.stack([fn(g) for g in range(G)])
    nt = (((1,), (1,)), ((), ()))
    tn = (((0,), (0,)), ((), ()))
    s = per_group(lambda g: lax.dot_general(q[g], k[g], nt, preferred_element_type=F32)) * d
    sb = s.astype(BF16)
    cb = c0.astype(BF16)
    num = per_group(lambda g: _dot(sb[g], v[g])) + w_inter * per_group(lambda g: _dot(q[g], cb[g]))
    qn = jnp.sum(s, axis=2, keepdims=True) + w_inter * jnp.sum(q.astype(F32) * n0, axis=2, keepdims=True)
    den = jnp.maximum(jnp.abs(qn), jnp.exp(-m_t))
    rden = 1.0 / den
    mu = jnp.mean(num, axis=2, keepdims=True)
    cen = num - mu
    var = jnp.mean(cen * cen, axis=2, keepdims=True)
    h = cen * (rden * lax.rsqrt(rden * rden * var + LN_EPS)) * gain

    m_new = m_t[:, last:last + 1, :]
    w_k = jnp.exp(b_c[:, last:last + 1, :] - b_c + ig_c - m_new)
    w_c = jnp.exp(inter[:, last:last + 1, :] - m_new)
    kw = k.astype(F32) * w_k
    kwb = kw.astype(BF16)
    c_new = w_c * c0 + per_group(lambda g: lax.dot_general(kwb[g], v[g], tn, preferred_element_type=F32))
    n_new = w_c * n0 + jnp.sum(kw, axis=1, keepdims=True)
    return h, c_new, n_new, m_new


def _log_sigmoid(x):
    return jnp.minimum(x, 0.0) - jnp.log1p(jnp.exp(-jnp.abs(x)))


def _mlstm_prompt_step(c, qk_ref, v_ref, if_ref, bias_ref, g_ref, hn_ref, c_ref, n_ref, m_ref, n_heads, d_qk, d_v):
    @pl.when(c == 0)
    def _():
        c_ref[...] = jnp.zeros_like(c_ref)
        n_ref[...] = jnp.zeros_like(n_ref)
        m_ref[...] = jnp.zeros_like(m_ref)

    gates = if_ref[...] + bias_ref[...]
    lf_all = _log_sigmoid(gates)
    L = gates.shape[0]
    heads = range(n_heads)
    q = jnp.stack([qk_ref[:, h * d_qk:(h + 1) * d_qk] for h in heads])
    k = jnp.stack([qk_ref[:, (n_heads + h) * d_qk:(n_heads + h + 1) * d_qk] for h in heads])
    v = jnp.stack([v_ref[:, h * d_v:(h + 1) * d_v] for h in heads])
    ig_c = jnp.stack([gates[:, h:h + 1] for h in heads])
    lf_c = jnp.stack([lf_all[:, n_heads + h:n_heads + h + 1] for h in heads])
    g = jnp.stack([g_ref[:, h * d_v:(h + 1) * d_v] for h in heads])
    hn, c_new, n_new, m_new = _mlstm_chunks(q, k, v, ig_c, lf_c, c_ref[0], n_ref[0], m_ref[0], g, L - 1)
    c_ref[0] = c_new
    n_ref[0] = n_new
    m_ref[0] = m_new
    hn = hn.astype(BF16)
    for h in heads:
        hn_ref[:, h * d_v:(h + 1) * d_v] = hn[h]


def _mlstm_sample_step(qk_ref, v_ref, if_ref, bias_ref, g_ref, c0_ref, n0_ref, m0_ref, hn_ref, c_ref, n_ref, m_ref,
                       n_heads, d_qk, d_v, t_dec, bs):
    pad = (-t_dec) % (2 * SUBLANES)

    def zpad_rows(x):
        x = x.reshape(bs, t_dec, x.shape[1])
        return jnp.concatenate([x, jnp.zeros((bs, pad, x.shape[2]), x.dtype)], axis=1) if pad else x

    qk_all = zpad_rows(qk_ref[...].astype(F32))
    v_all = zpad_rows(v_ref[...].astype(F32))
    gates = zpad_rows(if_ref[...] + bias_ref[...])
    lf_all = _log_sigmoid(gates)
    groups = [(b, h) for b in range(bs) for h in range(n_heads)]
    q = jnp.stack([qk_all[b][:, h * d_qk:(h + 1) * d_qk] for b, h in groups]).astype(BF16)
    k = jnp.stack([qk_all[b][:, (n_heads + h) * d_qk:(n_heads + h + 1) * d_qk] for b, h in groups]).astype(BF16)
    v = jnp.stack([v_all[b][:, h * d_v:(h + 1) * d_v] for b, h in groups]).astype(BF16)
    ig_c = jnp.stack([gates[b][:, h:h + 1] for b, h in groups])
    lf_c = jnp.stack([lf_all[b][:, n_heads + h:n_heads + h + 1] for b, h in groups])
    flat = lambda r: r[...].reshape((bs * n_heads,) + r.shape[2:])
    g = jnp.stack([g_ref[:, h * d_v:(h + 1) * d_v] for _, h in groups])
    hn, c_new, n_new, m_new = _mlstm_chunks(q, k, v, ig_c, lf_c, flat(c0_ref), flat(n0_ref), flat(m0_ref), g,
                                            t_dec - 1)
    c_ref[...] = c_new.reshape(c_ref.shape)
    n_ref[...] = n_new.reshape(n_ref.shape)
    m_ref[...] = m_new.reshape(m_ref.shape)
    hn = hn[:, :t_dec].astype(BF16)
    for i, (b, h) in enumerate(groups):
        hn_ref[b * t_dec:(b + 1) * t_dec, h * d_v:(h + 1) * d_v] = hn[i]


def _mlstm_kernel(qkp_ref, vp_ref, ifp_ref, qks_ref, vs_ref, ifs_ref, bias_ref, g_ref, c0_ref, n0_ref, m0_ref,
                  hnp_ref, cp_ref, np_ref, mp_ref, hns_ref, cs_ref, ns_ref, ms_ref,
                  *, n_chunks, n_heads, d_qk, d_v, t_dec, bs):
    _mlstm_prompt_step(pl.program_id(0) % n_chunks, qkp_ref, vp_ref, ifp_ref, bias_ref, g_ref,
                       hnp_ref, cp_ref, np_ref, mp_ref, n_heads, d_qk, d_v)
    _mlstm_sample_step(qks_ref, vs_ref, ifs_ref, bias_ref, g_ref, c0_ref, n0_ref, m0_ref,
                       hns_ref, cs_ref, ns_ref, ms_ref, n_heads, d_qk, d_v, t_dec, bs)


def _mlstm(qk, v, iff, bias, g, c0, n0, m0, batch, seq, t_dec):
    dec_batch, n_heads, d_qk, d_v = c0.shape
    d = v.shape[1]
    L = min(CHUNK, seq)
    assert seq % L == 0
    nc = seq // L
    steps = batch * nc
    assert dec_batch % steps == 0
    bs = dec_batch // steps
    rows = bs * t_dec
    n_prompt = batch * seq
    assert rows % (2 * SUBLANES) == 0 and n_prompt % rows == 0
    blk0 = n_prompt // rows
    ptok = lambda w: pl.BlockSpec((L, w), lambda i: (i, 0))
    stok = lambda w: pl.BlockSpec((rows, w), lambda i: (blk0 + i, 0))
    pst = lambda *tail: pl.BlockSpec((1, n_heads) + tail, lambda i: (i // nc, 0, 0, 0))
    sst = lambda *tail: pl.BlockSpec((bs, n_heads) + tail, lambda i: (i, 0, 0, 0))
    state_shapes = lambda nb: (jax.ShapeDtypeStruct((nb, n_heads, d_qk, d_v), F32),
                               jax.ShapeDtypeStruct((nb, n_heads, 1, d_qk), F32),
                               jax.ShapeDtypeStruct((nb, n_heads, 1, 1), F32))
    return pl.pallas_call(
        functools.partial(_mlstm_kernel, n_chunks=nc, n_heads=n_heads, d_qk=d_qk, d_v=d_v, t_dec=t_dec, bs=bs),
        out_shape=(jax.ShapeDtypeStruct((n_prompt, d), BF16), *state_shapes(batch),
                   jax.ShapeDtypeStruct((dec_batch * t_dec, d), BF16), *state_shapes(dec_batch)),
        grid=(steps,),
        in_specs=[ptok(qk.shape[1]), ptok(d), ptok(LANES), stok(qk.shape[1]), stok(d), stok(LANES),
                  pl.BlockSpec((1, LANES), lambda i: (0, 0)), pl.BlockSpec((1, d), lambda i: (0, 0)),
                  sst(d_qk, d_v), sst(1, d_qk), sst(1, 1)],
        out_specs=(ptok(d), pst(d_qk, d_v), pst(1, d_qk), pst(1, 1),
                   pl.BlockSpec((rows, d), lambda i: (i, 0)), sst(d_qk, d_v), sst(1, d_qk), sst(1, 1)),
        compiler_params=_params(("arbitrary",)),
        name="mlstm",
    )(qk, v, iff, qk, v, iff, bias, g, c0, n0, m0)


def _mix_ln_kernel(yc_ref, ga_ref, gb_ref, hnp_ref, hns_ref, x1_ref, cb_ref, cg_ref, cbb_ref, w_ref, g_ref, b_ref,
                   o_ref, *, alpha, split):
    def body(hn_ref):
        z = _layer_norm(yc_ref[...] + cb_ref[...], cg_ref[...], cbb_ref[...])
        hb = (z * _sigmoid(z)).astype(BF16)
        mixin = ga_ref[...] * hn_ref[...] + gb_ref[...] * hb
        mix = _dot(mixin, w_ref[...])
        o_ref[...] = _layer_norm(alpha * x1_ref[...] + mix, g_ref[...], b_ref[...])

    _for_owner((hnp_ref, hns_ref), split, pl.program_id(0), body)


def _mix_ln(yc, ga, gb, hn_p, hn_s, x1, cb, cg, cbb, w, g, b, alpha):
    n, d = x1.shape
    tm = min(TM_MIX, hn_s.shape[0])
    assert hn_p.shape[0] % tm == 0 and hn_s.shape[0] % tm == 0 and hn_p.shape[0] + hn_s.shape[0] == n
    tok = pl.BlockSpec((tm, d), lambda i: (i, 0))
    vec = pl.BlockSpec((1, d), lambda i: (0, 0))
    hn_specs, split = _segment_specs([hn_p.shape[0], hn_s.shape[0]], tm, d)
    return pl.pallas_call(
        functools.partial(_mix_ln_kernel, alpha=alpha, split=split),
        out_shape=jax.ShapeDtypeStruct((n, d), F32),
        grid=(n // tm,),
        in_specs=[tok, tok, tok, *hn_specs, tok, vec, vec, vec, pl.BlockSpec((d, d), lambda i: (0, 0)), vec, vec],
        out_specs=tok,
        compiler_params=_params(("arbitrary",)),
        name="mix_ln",
    )(yc, ga, gb, hn_p, hn_s, x1, cb, cg, cbb, w, g, b)


def _layer(xs, layer, batch, seq, dec_batch, t_dec, c0, n0, m0, cache, w_in, p, alpha):
    (ffn1_w1, ffn1_w3, ffn1_w2, ln1_g, ln1_b, b_igate, b_fgate, mh_norm_g,
     conv_w, conv_b, conv_ln_g, conv_ln_b, w_out, ln2_g, ln2_b,
     ffn2_w1, ffn2_w3, ffn2_w2, ln3_g, ln3_b) = p
    d = xs[0].shape[1]
    n_heads, d_qk, d_v = c0.shape[1:]
    n_prompt, n_sample = batch * seq, dec_batch * t_dec
    hqk, dm = n_heads * d_qk, n_heads * d_v
    assert 2 * hqk == d and dm == d and conv_w.shape[1] == d
    vec = lambda a: a.reshape(1, -1).astype(F32)
    bf = lambda a: a.astype(BF16)

    (x1,) = _ffn_ln(xs, ffn1_w1, ffn1_w3, ffn1_w2, vec(ln1_g), vec(ln1_b), alpha, (n_prompt + n_sample,))
    w, wif = _prep_w_in(jnp.swapaxes(w_in, 1, 2), layer, 2 * hqk + 2 * dm, 2 * n_heads, 4 * d)
    qk, v, ga, gb, yc, iff, tile_tails, conv_s_t = _proj(x1, w, wif, conv_w.astype(F32), jnp.swapaxes(cache, 0, 1),
                                                         hqk, float(d_qk) ** -0.5, batch, seq, t_dec)

    gate_bias = jnp.pad(jnp.concatenate([b_igate, b_fgate]).astype(F32), (0, LANES - 2 * n_heads)).reshape(1, LANES)
    mh_g = vec(mh_norm_g)
    hn_p, c_p, n_p, m_p, hn_s, c_s, n_s, m_s = _mlstm(qk, v, iff, gate_bias, mh_g, c0,
                                                      n0.reshape(dec_batch, n_heads, 1, d_qk),
                                                      m0.reshape(dec_batch, n_heads, 1, 1), batch, seq, t_dec)

    x2 = _mix_ln(yc, ga, gb, hn_p, hn_s, x1, vec(conv_b), vec(conv_ln_g), vec(conv_ln_b), bf(w_out),
                 vec(ln2_g), vec(ln2_b), alpha)
    y_p, y_s = _ffn_ln((x2,), ffn2_w1, ffn2_w3, ffn2_w2, vec(ln3_g), vec(ln3_b), alpha, (n_prompt, n_sample))
    tiles_per_seq = tile_tails.shape[0] // batch
    conv_p = tile_tails[tiles_per_seq - 1::tiles_per_seq]
    states_p = (c_p, n_p.reshape(batch, n_heads, d_qk), m_p.reshape(batch, n_heads), conv_p)
    states_s = (c_s, n_s.reshape(dec_batch, n_heads, d_qk), m_s.reshape(dec_batch, n_heads),
                jnp.swapaxes(conv_s_t, 0, 1))
    return (y_p, y_s), states_p, states_s


def kernel(x_prompt, x_sample, state_C, state_n, state_m, cache_conv, ffn1_w1, ffn1_w3, ffn1_w2, ln1_g, ln1_b, w_in, b_igate, b_fgate, mh_norm_g, conv_w, conv_b, conv_ln_g, conv_ln_b, w_out, ln2_g, ln2_b, ffn2_w1, ffn2_w3, ffn2_w2, ln3_g, ln3_b):
    batch, seq, d = x_prompt.shape
    dec_batch, t_dec, _ = x_sample.shape
    depth = ffn1_w1.shape[0]
    alpha = (2.0 * depth) ** 0.25
    xs = (x_prompt.reshape(batch * seq, d), x_sample.reshape(dec_batch * t_dec, d))
    weights = (ffn1_w1, ffn1_w3, ffn1_w2, ln1_g, ln1_b, b_igate, b_fgate, mh_norm_g,
               conv_w, conv_b, conv_ln_g, conv_ln_b, w_out, ln2_g, ln2_b,
               ffn2_w1, ffn2_w3, ffn2_w2, ln3_g, ln3_b)
    outs_p, outs_s = [], []
    for l in range(depth):
        p = tuple(wt[l] for wt in weights)
        xs, st_p, st_s = _layer(xs, l, batch, seq, dec_batch, t_dec, state_C[l], state_n[l], state_m[l],
                                cache_conv[l], w_in, p, alpha)
        outs_p.append(st_p)
        outs_s.append(st_s)
    y_p = xs[0].reshape(batch, seq, d)
    y_s = xs[1].reshape(dec_batch, t_dec, d)
    stack = lambda outs, i: jnp.stack([o[i] for o in outs])
    return (y_p, y_s,
            stack(outs_p, 0), stack(outs_p, 1), stack(outs_p, 2), stack(outs_p, 3),
            stack(outs_s, 0), stack(outs_s, 1), stack(outs_s, 2), stack(outs_s, 3))
```

```python
import functools

import jax
import jax.numpy as jnp
from jax import lax
from jax.experimental import pallas as pl
from jax.experimental.pallas import tpu as pltpu

F32 = jnp.float32
BF16 = jnp.bfloat16
LN_EPS = 1e-5
LANES = 128
SUBLANES = 8
HALO = 32
VMEM_LIMIT = 56 * 1024 * 1024

TM_FFN = 512
TF_FFN = 512
TF_FFN_FIRST = 256
TILES_FFN_FIRST = 2
TM_PROJ = 512
TN_PROJ = 256
TW_PREP = 512
TR_PREP = 512
TM_MIX = 256
CONV_ROWS = 128
CHUNK = 128


def _params(sem):
    return pltpu.CompilerParams(dimension_semantics=sem, vmem_limit_bytes=VMEM_LIMIT)


def _sigmoid(x):
    return 0.5 * jnp.tanh(0.5 * x) + 0.5


def _layer_norm(z, g, b):
    mu = jnp.mean(z, axis=-1, keepdims=True)
    zc = z - mu
    var = jnp.mean(zc * zc, axis=-1, keepdims=True)
    return zc * lax.rsqrt(var + LN_EPS) * g + b


def _dot(a, b):
    return jnp.dot(a, b, preferred_element_type=F32)


def _for_owner(refs, split, i, fn):
    if len(refs) == 1:
        fn(refs[0])
        return
    pl.when(i < split)(lambda: fn(refs[0]))
    pl.when(i >= split)(lambda: fn(refs[1]))


def _segment_specs(arrays_rows, tm, d):
    if len(arrays_rows) == 1:
        return [pl.BlockSpec((tm, d), lambda i, *_: (i, 0))], 0
    split = arrays_rows[0] // tm
    last0 = split - 1
    return [pl.BlockSpec((tm, d), lambda i, *_: (jnp.minimum(i, last0), 0)),
            pl.BlockSpec((tm, d), lambda i, *_: (jnp.maximum(i - split, 0), 0))], split


def _ffn_ln_kernel(*refs, alpha, n_in, n_out, in_split, out_split, n_first, emit_w):
    x_refs = refs[:n_in]
    w1_ref, w3_ref, w2_ref, g_ref, b_ref = refs[n_in:n_in + 5]
    pos = n_in + 5
    first_ref = refs[pos] if n_first else None
    pos += bool(n_first)
    o_refs = refs[pos:pos + n_out]
    pos += n_out
    wb_refs = refs[pos:pos + 3] if emit_w else ()
    scratch = refs[pos + len(wb_refs):]
    xb_ref, acc_ref = scratch if len(scratch) == 2 else (scratch[0], o_refs[0])
    i = pl.program_id(0)
    f = pl.program_id(1)
    last_f = pl.num_programs(1) - 1

    def tile():
        @pl.when(f == 0)
        def _():
            def cast(x_ref):
                xb_ref[...] = x_ref[...].astype(BF16)
            _for_owner(x_refs, in_split, i, cast)
            acc_ref[...] = jnp.zeros_like(acc_ref)

        w1, w3, w2 = (r[...].astype(BF16) for r in (w1_ref, w3_ref, w2_ref))
        for wb_ref, w in zip(wb_refs, (w1, w3, w2)):
            wb_ref[...] = w
        xb = xb_ref[...]
        a = _dot(xb, w1)
        c = _dot(xb, w3)
        h = (a * _sigmoid(a) * c).astype(BF16)
        acc_ref[...] += _dot(h, w2)

        @pl.when(f == last_f)
        def _():
            def finish(x_ref):
                z = alpha * x_ref[...] + 0.5 * acc_ref[...]
                acc_ref[...] = _layer_norm(z, g_ref[...], b_ref[...])
            _for_owner(x_refs, in_split, i, finish)

            def emit(o_ref):
                o_ref[...] = acc_ref[...]
            if len(scratch) == 2:
                _for_owner(o_refs, out_split, i, emit)

    if not n_first:
        tile()
        return
    pl.when(i >= n_first)(tile)

    @pl.when(jnp.logical_and(i < n_first, f == last_f))
    def _():
        o_refs[0][...] = first_ref[...]


def _ffn_ln(xs, w1, w3, w2, g, b, alpha, out_rows):
    d = xs[0].shape[1]
    n = sum(x.shape[0] for x in xs)
    dff = w1.shape[1]
    tm, tf, tf0 = min(TM_FFN, n), min(TF_FFN, dff), min(TF_FFN_FIRST, dff)
    n_first = TILES_FFN_FIRST
    tm0 = n_first * tm
    assert sum(out_rows) == n and dff % tf == 0 and dff % tf0 == 0 and xs[0].shape[0] >= tm0 <= out_rows[0]
    assert all(x.shape[0] % tm == 0 for x in xs) and all(r % tm == 0 for r in out_rows)
    vec = pl.BlockSpec((1, d), lambda i, f: (0, 0))
    once = pl.BlockSpec((tm0, d), lambda i, f: (0, 0), pipeline_mode=pl.Buffered(1))
    scratch = lambda t, own_acc: [pltpu.VMEM((t, d), BF16)] + ([pltpu.VMEM((t, d), F32)] if own_acc else [])
    kern = functools.partial(_ffn_ln_kernel, alpha=alpha)

    w_specs = [pl.BlockSpec((d, tf0), lambda i, f: (0, f)), pl.BlockSpec((d, tf0), lambda i, f: (0, f)),
               pl.BlockSpec((tf0, d), lambda i, f: (f, 0))]
    first, w1b, w3b, w2b = pl.pallas_call(
        functools.partial(kern, n_in=1, n_out=1, in_split=0, out_split=0, n_first=0, emit_w=True),
        out_shape=(jax.ShapeDtypeStruct((tm0, d), F32),
                   *(jax.ShapeDtypeStruct(w.shape, BF16) for w in (w1, w3, w2))),
        grid=(1, dff // tf0),
        in_specs=[once] + w_specs + [vec, vec],
        out_specs=(once, *w_specs),
        scratch_shapes=scratch(tm0, False),
        compiler_params=_params(("arbitrary", "arbitrary")),
        name="ffn_ln_first",
    )(xs[0], w1, w3, w2, g, b)

    wf = lambda i, f: jnp.where(i < n_first, 0, f)
    w_specs = [pl.BlockSpec((d, tf), lambda i, f: (0, wf(i, f))), pl.BlockSpec((d, tf), lambda i, f: (0, wf(i, f))),
               pl.BlockSpec((tf, d), lambda i, f: (wf(i, f), 0))]
    in_specs, in_split = _segment_specs([x.shape[0] for x in xs], tm, d)
    out_specs, out_split = _segment_specs(list(out_rows), tm, d)
    return pl.pallas_call(
        functools.partial(kern, n_in=len(xs), n_out=len(out_rows), in_split=in_split, out_split=out_split,
                          n_first=n_first, emit_w=False),
        out_shape=tuple(jax.ShapeDtypeStruct((r, d), F32) for r in out_rows),
        grid=(n // tm, dff // tf),
        in_specs=in_specs + w_specs + [vec, vec, pl.BlockSpec((tm, d), lambda i, f: (jnp.minimum(i, n_first - 1), 0),
                                                              pipeline_mode=pl.Buffered(1))],
        out_specs=tuple(out_specs),
        scratch_shapes=scratch(tm, len(out_rows) > 1),
        compiler_params=_params(("arbitrary", "arbitrary")),
        name="ffn_ln",
    )(*xs, w1b, w3b, w2b, g, b, first)


def _prep_w_in_kernel(a_ref, nb_ref, if_ref, w_ref, wif_ref, *, n_aligned, n_gate):
    j = pl.program_id(0)
    tw, k_rows = a_ref.shape
    tr = min(TR_PREP, k_rows)

    def emit(rows_of):
        for c0 in range(0, k_rows, tr):
            w_ref[c0:c0 + tr, :] = rows_of(c0).T.astype(BF16)

    @pl.when(j < n_aligned)
    def _():
        emit(lambda c0: a_ref[:, c0:c0 + tr])

    @pl.when(j >= n_aligned)
    def _():
        emit(lambda c0: jnp.concatenate([a_ref[n_gate:tw, c0:c0 + tr], nb_ref[:, c0:c0 + tr]], axis=0))

    @pl.when(j == 0)
    def _():
        for c0 in range(0, k_rows, tr):
            rows = jnp.concatenate([if_ref[:, c0:c0 + tr], jnp.zeros((LANES - n_gate, tr), F32)], axis=0)
            wif_ref[c0:c0 + tr, :] = rows.T.astype(BF16)


def _prep_w_in(w_in_t, layer, n_aligned_cols, n_gate, n_shifted_cols):
    _, n_in, k_rows = w_in_t.shape
    tw = TW_PREP
    assert n_aligned_cols % tw == 0 and n_shifted_cols % tw == 0 and n_gate == SUBLANES
    assert k_rows % min(TR_PREP, k_rows) == 0 and n_aligned_cols + n_gate + n_shifted_cols == n_in
    n_aligned = n_aligned_cols // tw
    n_tiles = n_aligned + n_shifted_cols // tw
    per = tw // n_gate
    return pl.pallas_call(
        functools.partial(_prep_w_in_kernel, n_aligned=n_aligned, n_gate=n_gate),
        out_shape=(jax.ShapeDtypeStruct((k_rows, n_tiles * tw), BF16),
                   jax.ShapeDtypeStruct((k_rows, LANES), BF16)),
        grid=(n_tiles,),
        in_specs=[
            pl.BlockSpec((None, tw, k_rows), lambda j: (layer, j, 0)),
            pl.BlockSpec((None, n_gate, k_rows), lambda j: (layer, (j + 1) * per, 0)),
            pl.BlockSpec((None, n_gate, k_rows), lambda j: (layer, n_aligned * per, 0)),
        ],
        out_specs=(pl.BlockSpec((k_rows, tw), lambda j: (0, j)),
                   pl.BlockSpec((k_rows, LANES), lambda j: (0, 0))),
        compiler_params=_params(("arbitrary",)),
        name="prep_w_in",
    )(w_in_t, w_in_t, w_in_t)


def _conv_tile(ubuf_ref, w_ref, y_ref, u0, out0, rows, width, cs):
    y = None
    for r in range(min(SUBLANES, width)):
        q = None
        for a in range((width - 1 - r) // SUBLANES + 1):
            k = width - 1 - (SUBLANES * a + r)
            term = w_ref[k:k + 1, cs] * ubuf_ref[pl.ds(u0 - SUBLANES * (a + 1), rows + SUBLANES), cs]
            q = term if q is None else q + term
        part = q[SUBLANES - r:SUBLANES - r + rows]
        y = part if y is None else y + part
    y_ref[pl.ds(out0, rows), cs] = y


def _proj_kernel(x_ref, wqk_ref, wv_ref, wo_ref, wla_ref, wlb_ref, wga_ref, wgb_ref, wif_ref, cw_ref, cache_ref,
                 qk_ref, v_ref, ga_ref, gb_ref, yc_ref, if_ref, csp_ref, css_ref,
                 xb_ref, ubuf_ref, carry_ref, tbuf_ref, ybuf_ref, *, k_col0, k_scale, n_prompt_tiles, tiles_per_seq, width, t_dec):
    i = pl.program_id(0)
    j = pl.program_id(1)
    tm, tn = yc_ref.shape
    hist = width - 1

    @pl.when(j == 0)
    def _():
        xb0 = x_ref[...].astype(BF16)
        xb_ref[...] = xb0
        if_ref[...] = _dot(xb0, wif_ref[...])

    def glu(xb):
        return _dot(xb, wla_ref[...]) * _sigmoid(_dot(xb, wlb_ref[...]))

    def dense(xb):
        qk_ref[...] = (_dot(xb, wqk_ref[...]) * jnp.where(j * tn >= k_col0, k_scale, 1.0)).astype(BF16)
        v_ref[...] = _dot(xb, wv_ref[...]).astype(BF16)
        ga_ref[...] = (_sigmoid(_dot(xb, wo_ref[...])) * _sigmoid(_dot(xb, wga_ref[...]))).astype(BF16)
        gb_ref[...] = _sigmoid(_dot(xb, wgb_ref[...])).astype(BF16)

    @pl.when(i < n_prompt_tiles)
    def _():
        xb = xb_ref[...]
        u = glu(xb)
        first = (i % tiles_per_seq) == 0
        ubuf_ref[0:HALO, :] = jnp.where(first, 0.0, carry_ref[j])
        ubuf_ref[HALO:HALO + tm, :] = u
        carry_ref[j] = u[tm - HALO:tm]
        csp_ref[...] = u[tm - hist:tm]
        strip = min(CONV_ROWS, tm)
        for c0 in range(0, tn, LANES):
            for r0 in range(0, tm, strip):
                _conv_tile(ubuf_ref, cw_ref, yc_ref, HALO + r0, r0, strip, width, slice(c0, c0 + LANES))
        dense(xb)

    @pl.when(i >= n_prompt_tiles)
    def _():
        xb = xb_ref[...]
        nb = tm // t_dec
        u = glu(xb)
        for c in range(tn // LANES):
            cs = slice(c * LANES, (c + 1) * LANES)
            tbuf_ref[c] = u[:, cs]
            new = [tbuf_ref[c, pl.ds(t, nb, stride=t_dec), :] for t in range(t_dec)]
            row = lambda p: cache_ref[p, :, cs] if p < hist else new[p - hist]
            for t in range(t_dec):
                acc = cw_ref[0:1, cs] * row(t)
                for k in range(1, width):
                    acc = acc + cw_ref[k:k + 1, cs] * row(t + k)
                ybuf_ref[c, pl.ds(t, nb, stride=t_dec), :] = acc
            yc_ref[:, cs] = ybuf_ref[c]
            for p in range(hist):
                css_ref[p, :, cs] = row(p + t_dec)
        dense(xb)


def _proj(x1, w, wif, cw, cache_t, hqk, k_scale, batch, seq, t_dec):
    n, d = x1.shape
    hist, dec_batch, _ = cache_t.shape
    width = cw.shape[0]
    tm, tn = min(TM_PROJ, seq), min(TN_PROJ, d)
    n_prompt = batch * seq
    assert n_prompt % tm == 0 and (n - n_prompt) % tm == 0 and seq % tm == 0 and d % tn == 0
    assert hqk % tn == 0 and 2 * hqk == d and tn % LANES == 0
    assert hist == width - 1 and SUBLANES * (hist // SUBLANES + 1) <= HALO <= tm and tm % CONV_ROWS == 0
    assert n - n_prompt == dec_batch * t_dec and tm % t_dec == 0 and (tm // t_dec) % SUBLANES == 0 and t_dec <= hist
    npt, tps, nb = n_prompt // tm, seq // tm, tm // t_dec
    per = d // tn
    col = pl.BlockSpec((tm, tn), lambda i, j: (i, j))
    wspec = lambda g: pl.BlockSpec((d, tn), lambda i, j: (0, g * per + j))
    smp_i = lambda i: jnp.maximum(i - npt, 0)
    smp_j = lambda i, j: jnp.where(i >= npt, j, 0)
    prm_i = lambda i: jnp.minimum(i, npt - 1)
    prm_j = lambda i, j: jnp.where(i < npt, j, per - 1)
    return pl.pallas_call(
        functools.partial(_proj_kernel, k_col0=hqk, k_scale=k_scale, n_prompt_tiles=npt, tiles_per_seq=tps,
                          width=width, t_dec=t_dec),
        out_shape=(
            jax.ShapeDtypeStruct((n, d), BF16),
            jax.ShapeDtypeStruct((n, d), BF16),
            jax.ShapeDtypeStruct((n, d), BF16),
            jax.ShapeDtypeStruct((n, d), BF16),
            jax.ShapeDtypeStruct((n, d), F32),
            jax.ShapeDtypeStruct((n, LANES), F32),
            jax.ShapeDtypeStruct((npt, hist, d), F32),
            jax.ShapeDtypeStruct((hist, dec_batch, d), F32),
        ),
        grid=(n // tm, d // tn),
        in_specs=[pl.BlockSpec((tm, d), lambda i, j: (i, 0)),
                  wspec(0), wspec(1), wspec(2), wspec(3), wspec(4), wspec(5), wspec(6),
                  pl.BlockSpec((d, LANES), lambda i, j: (0, 0)),
                  pl.BlockSpec((width, tn), lambda i, j: (0, j)),
                  pl.BlockSpec((hist, nb, tn), lambda i, j: (0, smp_i(i), smp_j(i, j)))],
        out_specs=(col, col, col, col, col, pl.BlockSpec((tm, LANES), lambda i, j: (i, 0)),
                   pl.BlockSpec((None, hist, tn), lambda i, j: (prm_i(i), 0, prm_j(i, j))),
                   pl.BlockSpec((hist, nb, tn), lambda i, j: (0, smp_i(i), smp_j(i, j)))),
        scratch_shapes=[pltpu.VMEM((tm, d), BF16), pltpu.VMEM((HALO + tm, tn), F32),
                        pltpu.VMEM((per, HALO, tn), F32),
                        pltpu.VMEM((tn // LANES, tm, LANES), F32), pltpu.VMEM((tn // LANES, tm, LANES), F32)],
        compiler_params=_params(("arbitrary", "arbitrary")),
        name="proj",
    )(x1, w, w, w, w, w, w, w, wif, cw, cache_t)


def _mlstm_chunks(q, k, v, ig_c, lf_c, c0, n0, m0, gain, last):
    G, L, _ = q.shape
    row = lax.broadcasted_iota(jnp.int32, (L, L), 0)
    colm = lax.broadcasted_iota(jnp.int32, (L, L), 1)
    causal = (colm <= row)[None]
    eye = (colm == row)[None]
    upper = (row <= colm)[None]
    b_r = jnp.sum(jnp.where(upper, lf_c, 0.0), axis=1, keepdims=True)
    b_c = jnp.sum(jnp.where(eye, b_r, 0.0), axis=2, keepdims=True)
    ig_r = jnp.sum(jnp.where(eye, ig_c, 0.0), axis=1, keepdims=True)

    log_d = jnp.where(causal, b_c - b_r + ig_r, -jnp.inf)
    inter = b_c + m0
    m_t = jnp.maximum(inter, jnp.max(log_d, axis=2, keepdims=True))
    d = jnp.exp(log_d - m_t)
    w_inter = jnp.exp(inter - m_t)
    per_group = lambda fn: jnp.stack([fn(g) for g in range(G)])
    nt = (((1,), (1,)), ((), ()))
    tn = (((0,), (0,)), ((), ()))
    s = per_group(lambda g: lax.dot_general(q[g], k[g], nt, preferred_element_type=F32)) * d
    sb = s.astype(BF16)
    cb = c0.astype(BF16)
    num = per_group(lambda g: _dot(sb[g], v[g])) + w_inter * per_group(lambda g: _dot(q[g], cb[g]))
    qn = jnp.sum(s, axis=2, keepdims=True) + w_inter * jnp.sum(q.astype(F32) * n0, axis=2, keepdims=True)
    den = jnp.maximum(jnp.abs(qn), jnp.exp(-m_t))
    rden = 1.0 / den
    mu = jnp.mean(num, axis=2, keepdims=True)
    cen = num - mu
    var = jnp.mean(cen * cen, axis=2, keepdims=True)
    h = cen * (rden * lax.rsqrt(rden * rden * var + LN_EPS)) * gain

    m_new = m_t[:, last:last + 1, :]
    w_k = jnp.exp(b_c[:, last:last + 1, :] - b_c + ig_c - m_new)
    w_c = jnp.exp(inter[:, last:last + 1, :] - m_new)
    kw = k.astype(F32) * w_k
    kwb = kw.astype(BF16)
    c_new = w_c * c0 + per_group(lambda g: lax.dot_general(kwb[g], v[g], tn, preferred_element_type=F32))
    n_new = w_c * n0 + jnp.sum(kw, axis=1, keepdims=True)
    return h, c_new, n_new, m_new


def _log_sigmoid(x):
    return jnp.minimum(x, 0.0) - jnp.log1p(jnp.exp(-jnp.abs(x)))


def _mlstm_prompt_step(c, qk_ref, v_ref, if_ref, bias_ref, g_ref, hn_ref, c_ref, n_ref, m_ref, n_heads, d_qk, d_v):
    @pl.when(c == 0)
    def _():
        c_ref[...] = jnp.zeros_like(c_ref)
        n_ref[...] = jnp.zeros_like(n_ref)
        m_ref[...] = jnp.zeros_like(m_ref)

    gates = if_ref[...] + bias_ref[...]
    lf_all = _log_sigmoid(gates)
    L = gates.shape[0]
    heads = range(n_heads)
    q = jnp.stack([qk_ref[:, h * d_qk:(h + 1) * d_qk] for h in heads])
    k = jnp.stack([qk_ref[:, (n_heads + h) * d_qk:(n_heads + h + 1) * d_qk] for h in heads])
    v = jnp.stack([v_ref[:, h * d_v:(h + 1) * d_v] for h in heads])
    ig_c = jnp.stack([gates[:, h:h + 1] for h in heads])
    lf_c = jnp.stack([lf_all[:, n_heads + h:n_heads + h + 1] for h in heads])
    g = jnp.stack([g_ref[:, h * d_v:(h + 1) * d_v] for h in heads])
    hn, c_new, n_new, m_new = _mlstm_chunks(q, k, v, ig_c, lf_c, c_ref[0], n_ref[0], m_ref[0], g, L - 1)
    c_ref[0] = c_new
    n_ref[0] = n_new
    m_ref[0] = m_new
    hn = hn.astype(BF16)
    for h in heads:
        hn_ref[:, h * d_v:(h + 1) * d_v] = hn[h]


def _mlstm_sample_step(qk_ref, v_ref, if_ref, bias_ref, g_ref, c0_ref, n0_ref, m0_ref, hn_ref, c_ref, n_ref, m_ref,
                       n_heads, d_qk, d_v, t_dec, bs):
    pad = (-t_dec) % (2 * SUBLANES)

    def zpad_rows(x):
        x = x.reshape(bs, t_dec, x.shape[1])
        return jnp.concatenate([x, jnp.zeros((bs, pad, x.shape[2]), x.dtype)], axis=1) if pad else x

    qk_all = zpad_rows(qk_ref[...].astype(F32))
    v_all = zpad_rows(v_ref[...].astype(F32))
    gates = zpad_rows(if_ref[...] + bias_ref[...])
    lf_all = _log_sigmoid(gates)
    groups = [(b, h) for b in range(bs) for h in range(n_heads)]
    q = jnp.stack([qk_all[b][:, h * d_qk:(h + 1) * d_qk] for b, h in groups]).astype(BF16)
    k = jnp.stack([qk_all[b][:, (n_heads + h) * d_qk:(n_heads + h + 1) * d_qk] for b, h in groups]).astype(BF16)
    v = jnp.stack([v_all[b][:, h * d_v:(h + 1) * d_v] for b, h in groups]).astype(BF16)
    ig_c = jnp.stack([gates[b][:, h:h + 1] for b, h in groups])
    lf_c = jnp.stack([lf_all[b][:, n_heads + h:n_heads + h + 1] for b, h in groups])
    flat = lambda r: r[...].reshape((bs * n_heads,) + r.shape[2:])
    g = jnp.stack([g_ref[:, h * d_v:(h + 1) * d_v] for _, h in groups])
    hn, c_new, n_new, m_new = _mlstm_chunks(q, k, v, ig_c, lf_c, flat(c0_ref), flat(n0_ref), flat(m0_ref), g,
                                            t_dec - 1)
    c_ref[...] = c_new.reshape(c_ref.shape)
    n_ref[...] = n_new.reshape(n_ref.shape)
    m_ref[...] = m_new.reshape(m_ref.shape)
    hn = hn[:, :t_dec].astype(BF16)
    for i, (b, h) in enumerate(groups):
        hn_ref[b * t_dec:(b + 1) * t_dec, h * d_v:(h + 1) * d_v] = hn[i]


def _mlstm_kernel(qkp_ref, vp_ref, ifp_ref, qks_ref, vs_ref, ifs_ref, bias_ref, g_ref, c0_ref, n0_ref, m0_ref,
                  hnp_ref, cp_ref, np_ref, mp_ref, hns_ref, cs_ref, ns_ref, ms_ref,
                  *, n_chunks, n_heads, d_qk, d_v, t_dec, bs):
    _mlstm_prompt_step(pl.program_id(0) % n_chunks, qkp_ref, vp_ref, ifp_ref, bias_ref, g_ref,
                       hnp_ref, cp_ref, np_ref, mp_ref, n_heads, d_qk, d_v)
    _mlstm_sample_step(qks_ref, vs_ref, ifs_ref, bias_ref, g_ref, c0_ref, n0_ref, m0_ref,
                       hns_ref, cs_ref, ns_ref, ms_ref, n_heads, d_qk, d_v, t_dec, bs)


def _mlstm(qk, v, iff, bias, g, c0, n0, m0, batch, seq, t_dec):
    dec_batch, n_heads, d_qk, d_v = c0.shape
    d = v.shape[1]
    L = min(CHUNK, seq)
    assert seq % L == 0
    nc = seq // L
    steps = batch * nc
    assert dec_batch % steps == 0
    bs = dec_batch // steps
    rows = bs * t_dec
    n_prompt = batch * seq
    assert rows % (2 * SUBLANES) == 0 and n_prompt % rows == 0
    blk0 = n_prompt // rows
    ptok = lambda w: pl.BlockSpec((L, w), lambda i: (i, 0))
    stok = lambda w: pl.BlockSpec((rows, w), lambda i: (blk0 + i, 0))
    pst = lambda *tail: pl.BlockSpec((1, n_heads) + tail, lambda i: (i // nc, 0, 0, 0))
    sst = lambda *tail: pl.BlockSpec((bs, n_heads) + tail, lambda i: (i, 0, 0, 0))
    state_shapes = lambda nb: (jax.ShapeDtypeStruct((nb, n_heads, d_qk, d_v), F32),
                               jax.ShapeDtypeStruct((nb, n_heads, 1, d_qk), F32),
                               jax.ShapeDtypeStruct((nb, n_heads, 1, 1), F32))
    return pl.pallas_call(
        functools.partial(_mlstm_kernel, n_chunks=nc, n_heads=n_heads, d_qk=d_qk, d_v=d_v, t_dec=t_dec, bs=bs),
        out_shape=(jax.ShapeDtypeStruct((n_prompt, d), BF16), *state_shapes(batch),
                   jax.ShapeDtypeStruct((dec_batch * t_dec, d), BF16), *state_shapes(dec_batch)),
        grid=(steps,),
        in_specs=[ptok(qk.shape[1]), ptok(d), ptok(LANES), stok(qk.shape[1]), stok(d), stok(LANES),
                  pl.BlockSpec((1, LANES), lambda i: (0, 0)), pl.BlockSpec((1, d), lambda i: (0, 0)),
                  sst(d_qk, d_v), sst(1, d_qk), sst(1, 1)],
        out_specs=(ptok(d), pst(d_qk, d_v), pst(1, d_qk), pst(1, 1),
                   pl.BlockSpec((rows, d), lambda i: (i, 0)), sst(d_qk, d_v), sst(1, d_qk), sst(1, 1)),
        compiler_params=_params(("arbitrary",)),
        name="mlstm",
    )(qk, v, iff, qk, v, iff, bias, g, c0, n0, m0)


def _mix_ln_kernel(yc_ref, ga_ref, gb_ref, hnp_ref, hns_ref, x1_ref, cb_ref, cg_ref, cbb_ref, w_ref, g_ref, b_ref,
                   o_ref, *, alpha, split):
    def body(hn_ref):
        z = _layer_norm(yc_ref[...] + cb_ref[...], cg_ref[...], cbb_ref[...])
        hb = (z * _sigmoid(z)).astype(BF16)
        mixin = ga_ref[...] * hn_ref[...] + gb_ref[...] * hb
        mix = _dot(mixin, w_ref[...])
        o_ref[...] = _layer_norm(alpha * x1_ref[...] + mix, g_ref[...], b_ref[...])

    _for_owner((hnp_ref, hns_ref), split, pl.program_id(0), body)


def _mix_ln(yc, ga, gb, hn_p, hn_s, x1, cb, cg, cbb, w, g, b, alpha):
    n, d = x1.shape
    tm = min(TM_MIX, hn_s.shape[0])
    assert hn_p.shape[0] % tm == 0 and hn_s.shape[0] % tm == 0 and hn_p.shape[0] + hn_s.shape[0] == n
    tok = pl.BlockSpec((tm, d), lambda i: (i, 0))
    vec = pl.BlockSpec((1, d), lambda i: (0, 0))
    hn_specs, split = _segment_specs([hn_p.shape[0], hn_s.shape[0]], tm, d)
    return pl.pallas_call(
        functools.partial(_mix_ln_kernel, alpha=alpha, split=split),
        out_shape=jax.ShapeDtypeStruct((n, d), F32),
        grid=(n // tm,),
        in_specs=[tok, tok, tok, *hn_specs, tok, vec, vec, vec, pl.BlockSpec((d, d), lambda i: (0, 0)), vec, vec],
        out_specs=tok,
        compiler_params=_params(("arbitrary",)),
        name="mix_ln",
    )(yc, ga, gb, hn_p, hn_s, x1, cb, cg, cbb, w, g, b)


def _layer(xs, layer, batch, seq, dec_batch, t_dec, c0, n0, m0, cache, w_in, p, alpha):
    (ffn1_w1, ffn1_w3, ffn1_w2, ln1_g, ln1_b, b_igate, b_fgate, mh_norm_g,
     conv_w, conv_b, conv_ln_g, conv_ln_b, w_out, ln2_g, ln2_b,
     ffn2_w1, ffn2_w3, ffn2_w2, ln3_g, ln3_b) = p
    d = xs[0].shape[1]
    n_heads, d_qk, d_v = c0.shape[1:]
    n_prompt, n_sample = batch * seq, dec_batch * t_dec
    hqk, dm = n_heads * d_qk, n_heads * d_v
    assert 2 * hqk == d and dm == d and conv_w.shape[1] == d
    vec = lambda a: a.reshape(1, -1).astype(F32)
    bf = lambda a: a.astype(BF16)

    (x1,) = _ffn_ln(xs, ffn1_w1, ffn1_w3, ffn1_w2, vec(ln1_g), vec(ln1_b), alpha, (n_prompt + n_sample,))
    w, wif = _prep_w_in(jnp.swapaxes(w_in, 1, 2), layer, 2 * hqk + 2 * dm, 2 * n_heads, 4 * d)
    qk, v, ga, gb, yc, iff, tile_tails, conv_s_t = _proj(x1, w, wif, conv_w.astype(F32), jnp.swapaxes(cache, 0, 1),
                                                         hqk, float(d_qk) ** -0.5, batch, seq, t_dec)

    gate_bias = jnp.pad(jnp.concatenate([b_igate, b_fgate]).astype(F32), (0, LANES - 2 * n_heads)).reshape(1, LANES)
    mh_g = vec(mh_norm_g)
    hn_p, c_p, n_p, m_p, hn_s, c_s, n_s, m_s = _mlstm(qk, v, iff, gate_bias, mh_g, c0,
                                                      n0.reshape(dec_batch, n_heads, 1, d_qk),
                                                      m0.reshape(dec_batch, n_heads, 1, 1), batch, seq, t_dec)

    x2 = _mix_ln(yc, ga, gb, hn_p, hn_s, x1, vec(conv_b), vec(conv_ln_g), vec(conv_ln_b), bf(w_out),
                 vec(ln2_g), vec(ln2_b), alpha)
    y_p, y_s = _ffn_ln((x2,), ffn2_w1, ffn2_w3, ffn2_w2, vec(ln3_g), vec(ln3_b), alpha, (n_prompt, n_sample))
    tiles_per_seq = tile_tails.shape[0] // batch
    conv_p = tile_tails[tiles_per_seq - 1::tiles_per_seq]
    states_p = (c_p, n_p.reshape(batch, n_heads, d_qk), m_p.reshape(batch, n_heads), conv_p)
    states_s = (c_s, n_s.reshape(dec_batch, n_heads, d_qk), m_s.reshape(dec_batch, n_heads),
                jnp.swapaxes(conv_s_t, 0, 1))
    return (y_p, y_s), states_p, states_s


def kernel(x_prompt, x_sample, state_C, state_n, state_m, cache_conv, ffn1_w1, ffn1_w3, ffn1_w2, ln1_g, ln1_b, w_in, b_igate, b_fgate, mh_norm_g, conv_w, conv_b, conv_ln_g, conv_ln_b, w_out, ln2_g, ln2_b, ffn2_w1, ffn2_w3, ffn2_w2, ln3_g, ln3_b):
    batch, seq, d = x_prompt.shape
    dec_batch, t_dec, _ = x_sample.shape
    depth = ffn1_w1.shape[0]
    alpha = (2.0 * depth) ** 0.25
    xs = (x_prompt.reshape(batch * seq, d), x_sample.reshape(dec_batch * t_dec, d))
    weights = (ffn1_w1, ffn1_w3, ffn1_w2, ln1_g, ln1_b, b_igate, b_fgate, mh_norm_g,
               conv_w, conv_b, conv_ln_g, conv_ln_b, w_out, ln2_g, ln2_b,
               ffn2_w1, ffn2_w3, ffn2_w2, ln3_g, ln3_b)
    outs_p, outs_s = [], []
    for l in range(depth):
        p = tuple(wt[l] for wt in weights)
        xs, st_p, st_s = _layer(xs, l, batch, seq, dec_batch, t_dec, state_C[l], state_n[l], state_m[l],
                                cache_conv[l], w_in, p, alpha)
        outs_p.append(st_p)
        outs_s.append(st_s)
    y_p = xs[0].reshape(batch, seq, d)
    y_s = xs[1].reshape(dec_batch, t_dec, d)
    stack = lambda outs, i: jnp.stack([o[i] for o in outs])
    return (y_p, y_s,
            stack(outs_p, 0), stack(outs_p, 1), stack(outs_p, 2), stack(outs_p, 3),
            stack(outs_s, 0), stack(outs_s, 1), stack(outs_s, 2), stack(outs_s, 3))
```

```python
import functools

import jax
import jax.numpy as jnp
from jax import lax
from jax.experimental import pallas as pl
from jax.experimental.pallas import tpu as pltpu

F32 = jnp.float32
BF16 = jnp.bfloat16
LN_EPS = 1e-5
LANES = 128
SUBLANES = 8
HALO = 32
VMEM_LIMIT = 56 * 1024 * 1024

TM_FFN = 512
TF_FFN = 512
TF_FFN_FIRST = 256
TILES_FFN_FIRST = 2
TM_PROJ = 512
TN_PROJ = 256
TW_PREP = 512
TR_PREP = 512
TM_MIX = 256
CONV_ROWS = 128
CHUNK = 128


def _params(sem):
    return pltpu.CompilerParams(dimension_semantics=sem, vmem_limit_bytes=VMEM_LIMIT)


def _sigmoid(x):
    return 0.5 * jnp.tanh(0.5 * x) + 0.5


def _layer_norm(z, g, b):
    mu = jnp.mean(z, axis=-1, keepdims=True)
    zc = z - mu
    var = jnp.mean(zc * zc, axis=-1, keepdims=True)
    return zc * lax.rsqrt(var + LN_EPS) * g + b


def _dot(a, b):
    return jnp.dot(a, b, preferred_element_type=F32)


def _for_owner(refs, split, i, fn):
    if len(refs) == 1:
        fn(refs[0])
        return
    pl.when(i < split)(lambda: fn(refs[0]))
    pl.when(i >= split)(lambda: fn(refs[1]))


def _segment_specs(arrays_rows, tm, d):
    if len(arrays_rows) == 1:
        return [pl.BlockSpec((tm, d), lambda i, *_: (i, 0))], 0
    split = arrays_rows[0] // tm
    last0 = split - 1
    return [pl.BlockSpec((tm, d), lambda i, *_: (jnp.minimum(i, last0), 0)),
            pl.BlockSpec((tm, d), lambda i, *_: (jnp.maximum(i - split, 0), 0))], split


def _ffn_ln_kernel(*refs, alpha, n_in, n_out, in_split, out_split, n_first, emit_w):
    x_refs = refs[:n_in]
    w1_ref, w3_ref, w2_ref, g_ref, b_ref = refs[n_in:n_in + 5]
    pos = n_in + 5
    first_ref = refs[pos] if n_first else None
    pos += bool(n_first)
    o_refs = refs[pos:pos + n_out]
    pos += n_out
    wb_refs = refs[pos:pos + 3] if emit_w else ()
    scratch = refs[pos + len(wb_refs):]
    xb_ref, acc_ref = scratch if len(scratch) == 2 else (scratch[0], o_refs[0])
    i = pl.program_id(0)
    f = pl.program_id(1)
    last_f = pl.num_programs(1) - 1

    def tile():
        @pl.when(f == 0)
        def _():
            def start(x_ref):
                x = x_ref[...]
                xb_ref[...] = x.astype(BF16)
                acc_ref[...] = (2.0 * alpha) * x
            _for_owner(x_refs, in_split, i, start)

        w1, w3, w2 = (r[...].astype(BF16) for r in (w1_ref, w3_ref, w2_ref))
        for wb_ref, w in zip(wb_refs, (w1, w3, w2)):
            wb_ref[...] = w
        xb = xb_ref[...]
        a = _dot(xb, w1)
        c = _dot(xb, w3)
        h = (a * _sigmoid(a) * c).astype(BF16)
        acc_ref[...] += _dot(h, w2)

        @pl.when(f == last_f)
        def _():
            acc_ref[...] = _layer_norm(0.5 * acc_ref[...], g_ref[...], b_ref[...])

            def emit(o_ref):
                o_ref[...] = acc_ref[...]
            if len(scratch) == 2:
                _for_owner(o_refs, out_split, i, emit)

    if not n_first:
        tile()
        return
    pl.when(i >= n_first)(tile)

    @pl.when(jnp.logical_and(i < n_first, f == last_f))
    def _():
        o_refs[0][...] = first_ref[...]


def _ffn_ln(xs, w1, w3, w2, g, b, alpha, out_rows):
    d = xs[0].shape[1]
    n = sum(x.shape[0] for x in xs)
    dff = w1.shape[1]
    tm, tf, tf0 = min(TM_FFN, n), min(TF_FFN, dff), min(TF_FFN_FIRST, dff)
    n_first = TILES_FFN_FIRST
    tm0 = n_first * tm
    assert sum(out_rows) == n and dff % tf == 0 and dff % tf0 == 0 and xs[0].shape[0] >= tm0 <= out_rows[0]
    assert all(x.shape[0] % tm == 0 for x in xs) and all(r % tm == 0 for r in out_rows)
    vec = pl.BlockSpec((1, d), lambda i, f: (0, 0))
    once = pl.BlockSpec((tm0, d), lambda i, f: (0, 0), pipeline_mode=pl.Buffered(1))
    scratch = lambda t, own_acc: [pltpu.VMEM((t, d), BF16)] + ([pltpu.VMEM((t, d), F32)] if own_acc else [])
    kern = functools.partial(_ffn_ln_kernel, alpha=alpha)

    w_specs = [pl.BlockSpec((d, tf0), lambda i, f: (0, f)), pl.BlockSpec((d, tf0), lambda i, f: (0, f)),
               pl.BlockSpec((tf0, d), lambda i, f: (f, 0))]
    first, w1b, w3b, w2b = pl.pallas_call(
        functools.partial(kern, n_in=1, n_out=1, in_split=0, out_split=0, n_first=0, emit_w=True),
        out_shape=(jax.ShapeDtypeStruct((tm0, d), F32),
                   *(jax.ShapeDtypeStruct(w.shape, BF16) for w in (w1, w3, w2))),
        grid=(1, dff // tf0),
        in_specs=[once] + w_specs + [vec, vec],
        out_specs=(once, *w_specs),
        scratch_shapes=scratch(tm0, False),
        compiler_params=_params(("arbitrary", "arbitrary")),
        name="ffn_ln_first",
    )(xs[0], w1, w3, w2, g, b)

    wf = lambda i, f: jnp.where(i < n_first, 0, f)
    w_specs = [pl.BlockSpec((d, tf), lambda i, f: (0, wf(i, f))), pl.BlockSpec((d, tf), lambda i, f: (0, wf(i, f))),
               pl.BlockSpec((tf, d), lambda i, f: (wf(i, f), 0))]
    in_specs, in_split = _segment_specs([x.shape[0] for x in xs], tm, d)
    out_specs, out_split = _segment_specs(list(out_rows), tm, d)
    return pl.pallas_call(
        functools.partial(kern, n_in=len(xs), n_out=len(out_rows), in_split=in_split, out_split=out_split,
                          n_first=n_first, emit_w=False),
        out_shape=tuple(jax.ShapeDtypeStruct((r, d), F32) for r in out_rows),
        grid=(n // tm, dff // tf),
        in_specs=in_specs + w_specs + [vec, vec, pl.BlockSpec((tm, d), lambda i, f: (jnp.minimum(i, n_first - 1), 0),
                                                              pipeline_mode=pl.Buffered(1))],
        out_specs=tuple(out_specs),
        scratch_shapes=scratch(tm, len(out_rows) > 1),
        compiler_params=_params(("arbitrary", "arbitrary")),
        name="ffn_ln",
    )(*xs, w1b, w3b, w2b, g, b, first)


def _prep_w_in_kernel(a_ref, nb_ref, if_ref, w_ref, wif_ref, *, n_aligned, n_gate):
    j = pl.program_id(0)
    tw, k_rows = a_ref.shape
    tr = min(TR_PREP, k_rows)

    def emit(rows_of):
        for c0 in range(0, k_rows, tr):
            w_ref[c0:c0 + tr, :] = rows_of(c0).T.astype(BF16)

    @pl.when(j < n_aligned)
    def _():
        emit(lambda c0: a_ref[:, c0:c0 + tr])

    @pl.when(j >= n_aligned)
    def _():
        emit(lambda c0: jnp.concatenate([a_ref[n_gate:tw, c0:c0 + tr], nb_ref[:, c0:c0 + tr]], axis=0))

    @pl.when(j == 0)
    def _():
        for c0 in range(0, k_rows, tr):
            rows = jnp.concatenate([if_ref[:, c0:c0 + tr], jnp.zeros((LANES - n_gate, tr), F32)], axis=0)
            wif_ref[c0:c0 + tr, :] = rows.T.astype(BF16)


def _prep_w_in(w_in_t, layer, n_aligned_cols, n_gate, n_shifted_cols):
    _, n_in, k_rows = w_in_t.shape
    tw = TW_PREP
    assert n_aligned_cols % tw == 0 and n_shifted_cols % tw == 0 and n_gate == SUBLANES
    assert k_rows % min(TR_PREP, k_rows) == 0 and n_aligned_cols + n_gate + n_shifted_cols == n_in
    n_aligned = n_aligned_cols // tw
    n_tiles = n_aligned + n_shifted_cols // tw
    per = tw // n_gate
    return pl.pallas_call(
        functools.partial(_prep_w_in_kernel, n_aligned=n_aligned, n_gate=n_gate),
        out_shape=(jax.ShapeDtypeStruct((k_rows, n_tiles * tw), BF16),
                   jax.ShapeDtypeStruct((k_rows, LANES), BF16)),
        grid=(n_tiles,),
        in_specs=[
            pl.BlockSpec((None, tw, k_rows), lambda j: (layer, j, 0)),
            pl.BlockSpec((None, n_gate, k_rows), lambda j: (layer, (j + 1) * per, 0)),
            pl.BlockSpec((None, n_gate, k_rows), lambda j: (layer, n_aligned * per, 0)),
        ],
        out_specs=(pl.BlockSpec((k_rows, tw), lambda j: (0, j)),
                   pl.BlockSpec((k_rows, LANES), lambda j: (0, 0))),
        compiler_params=_params(("arbitrary",)),
        name="prep_w_in",
    )(w_in_t, w_in_t, w_in_t)


def _conv_tile(ubuf_ref, w_ref, y_ref, u0, out0, rows, width, cs):
    y = None
    for r in range(min(SUBLANES, width)):
        q = None
        for a in range((width - 1 - r) // SUBLANES + 1):
            k = width - 1 - (SUBLANES * a + r)
            term = w_ref[k:k + 1, cs] * ubuf_ref[pl.ds(u0 - SUBLANES * (a + 1), rows + SUBLANES), cs]
            q = term if q is None else q + term
        part = q[SUBLANES - r:SUBLANES - r + rows]
        y = part if y is None else y + part
    y_ref[pl.ds(out0, rows), cs] = y


def _proj_kernel(x_ref, wqk_ref, wv_ref, wo_ref, wla_ref, wlb_ref, wga_ref, wgb_ref, wif_ref, cw_ref, cache_ref,
                 qk_ref, v_ref, ga_ref, gb_ref, yc_ref, if_ref, csp_ref, css_ref,
                 xb_ref, ubuf_ref, carry_ref, tbuf_ref, ybuf_ref, *, k_col0, k_scale, n_prompt_tiles, tiles_per_seq, width, t_dec):
    i = pl.program_id(0)
    j = pl.program_id(1)
    tm, tn = yc_ref.shape
    hist = width - 1

    @pl.when(j == 0)
    def _():
        xb0 = x_ref[...].astype(BF16)
        xb_ref[...] = xb0
        if_ref[...] = _dot(xb0, wif_ref[...])

    def glu(xb):
        return _dot(xb, wla_ref[...]) * _sigmoid(_dot(xb, wlb_ref[...]))

    def dense(xb):
        qk_ref[...] = (_dot(xb, wqk_ref[...]) * jnp.where(j * tn >= k_col0, k_scale, 1.0)).astype(BF16)
        v_ref[...] = _dot(xb, wv_ref[...]).astype(BF16)
        ga_ref[...] = (_sigmoid(_dot(xb, wo_ref[...])) * _sigmoid(_dot(xb, wga_ref[...]))).astype(BF16)
        gb_ref[...] = _sigmoid(_dot(xb, wgb_ref[...])).astype(BF16)

    @pl.when(i < n_prompt_tiles)
    def _():
        xb = xb_ref[...]
        u = glu(xb)
        first = (i % tiles_per_seq) == 0
        ubuf_ref[0:HALO, :] = jnp.where(first, 0.0, carry_ref[j])
        ubuf_ref[HALO:HALO + tm, :] = u
        carry_ref[j] = u[tm - HALO:tm]
        csp_ref[...] = u[tm - hist:tm]
        strip = min(CONV_ROWS, tm)
        for c0 in range(0, tn, LANES):
            for r0 in range(0, tm, strip):
                _conv_tile(ubuf_ref, cw_ref, yc_ref, HALO + r0, r0, strip, width, slice(c0, c0 + LANES))
        dense(xb)

    @pl.when(i >= n_prompt_tiles)
    def _():
        xb = xb_ref[...]
        nb = tm // t_dec
        u = glu(xb)
        for c in range(tn // LANES):
            cs = slice(c * LANES, (c + 1) * LANES)
            tbuf_ref[c] = u[:, cs]
            new = [tbuf_ref[c, pl.ds(t, nb, stride=t_dec), :] for t in range(t_dec)]
            row = lambda p: cache_ref[p, :, cs] if p < hist else new[p - hist]
            for t in range(t_dec):
                acc = cw_ref[0:1, cs] * row(t)
                for k in range(1, width):
                    acc = acc + cw_ref[k:k + 1, cs] * row(t + k)
                ybuf_ref[c, pl.ds(t, nb, stride=t_dec), :] = acc
            yc_ref[:, cs] = ybuf_ref[c]
            for p in range(hist):
                css_ref[p, :, cs] = row(p + t_dec)
        dense(xb)


def _proj(x1, w, wif, cw, cache_t, hqk, k_scale, batch, seq, t_dec):
    n, d = x1.shape
    hist, dec_batch, _ = cache_t.shape
    width = cw.shape[0]
    tm, tn = min(TM_PROJ, seq), min(TN_PROJ, d)
    n_prompt = batch * seq
    assert n_prompt % tm == 0 and (n - n_prompt) % tm == 0 and seq % tm == 0 and d % tn == 0
    assert hqk % tn == 0 and 2 * hqk == d and tn % LANES == 0
    assert hist == width - 1 and SUBLANES * (hist // SUBLANES + 1) <= HALO <= tm and tm % CONV_ROWS == 0
    assert n - n_prompt == dec_batch * t_dec and tm % t_dec == 0 and (tm // t_dec) % SUBLANES == 0 and t_dec <= hist
    npt, tps, nb = n_prompt // tm, seq // tm, tm // t_dec
    per = d // tn
    col = pl.BlockSpec((tm, tn), lambda i, j: (i, j))
    wspec = lambda g: pl.BlockSpec((d, tn), lambda i, j: (0, g * per + j))
    smp_i = lambda i: jnp.maximum(i - npt, 0)
    smp_j = lambda i, j: jnp.where(i >= npt, j, 0)
    prm_i = lambda i: jnp.minimum(i, npt - 1)
    prm_j = lambda i, j: jnp.where(i < npt, j, per - 1)
    return pl.pallas_call(
        functools.partial(_proj_kernel, k_col0=hqk, k_scale=k_scale, n_prompt_tiles=npt, tiles_per_seq=tps,
                          width=width, t_dec=t_dec),
        out_shape=(
            jax.ShapeDtypeStruct((n, d), BF16),
            jax.ShapeDtypeStruct((n, d), BF16),
            jax.ShapeDtypeStruct((n, d), BF16),
            jax.ShapeDtypeStruct((n, d), BF16),
            jax.ShapeDtypeStruct((n, d), F32),
            jax.ShapeDtypeStruct((n, LANES), F32),
            jax.ShapeDtypeStruct((npt, hist, d), F32),
            jax.ShapeDtypeStruct((hist, dec_batch, d), F32),
        ),
        grid=(n // tm, d // tn),
        in_specs=[pl.BlockSpec((tm, d), lambda i, j: (i, 0)),
                  wspec(0), wspec(1), wspec(2), wspec(3), wspec(4), wspec(5), wspec(6),
                  pl.BlockSpec((d, LANES), lambda i, j: (0, 0)),
                  pl.BlockSpec((width, tn), lambda i, j: (0, j)),
                  pl.BlockSpec((hist, nb, tn), lambda i, j: (0, smp_i(i), smp_j(i, j)))],
        out_specs=(col, col, col, col, col, pl.BlockSpec((tm, LANES), lambda i, j: (i, 0)),
                   pl.BlockSpec((None, hist, tn), lambda i, j: (prm_i(i), 0, prm_j(i, j))),
                   pl.BlockSpec((hist, nb, tn), lambda i, j: (0, smp_i(i), smp_j(i, j)))),
        scratch_shapes=[pltpu.VMEM((tm, d), BF16), pltpu.VMEM((HALO + tm, tn), F32),
                        pltpu.VMEM((per, HALO, tn), F32),
                        pltpu.VMEM((tn // LANES, tm, LANES), F32), pltpu.VMEM((tn // LANES, tm, LANES), F32)],
        compiler_params=_params(("arbitrary", "arbitrary")),
        name="proj",
    )(x1, w, w, w, w, w, w, w, wif, cw, cache_t)


def _mlstm_chunks(q, k, v, ig_c, lf_c, c0, n0, m0, gain, last):
    G, L, _ = q.shape
    row = lax.broadcasted_iota(jnp.int32, (L, L), 0)
    colm = lax.broadcasted_iota(jnp.int32, (L, L), 1)
    causal = (colm <= row)[None]
    eye = (colm == row)[None]
    upper = (row <= colm)[None]
    b_r = jnp.sum(jnp.where(upper, lf_c, 0.0), axis=1, keepdims=True)
    b_c = jnp.sum(jnp.where(eye, b_r, 0.0), axis=2, keepdims=True)
    ig_r = jnp.sum(jnp.where(eye, ig_c, 0.0), axis=1, keepdims=True)

    log_d = jnp.where(causal, b_c - b_r + ig_r, -jnp.inf)
    inter = b_c + m0
    m_t = jnp.maximum(inter, jnp.max(log_d, axis=2, keepdims=True))
    d = jnp.exp(log_d - m_t)
    w_inter = jnp.exp(inter - m_t)
    per_group = lambda fn: jnp.stack([fn(g) for g in range(G)])
    nt = (((1,), (1,)), ((), ()))
    tn = (((0,), (0,)), ((), ()))
    s = per_group(lambda g: lax.dot_general(q[g], k[g], nt, preferred_element_type=F32)) * d
    sb = s.astype(BF16)
    cb = c0.astype(BF16)
    num = per_group(lambda g: _dot(sb[g], v[g])) + w_inter * per_group(lambda g: _dot(q[g], cb[g]))
    qn = jnp.sum(s, axis=2, keepdims=True) + w_inter * jnp.sum(q.astype(F32) * n0, axis=2, keepdims=True)
    den = jnp.maximum(jnp.abs(qn), jnp.exp(-m_t))
    rden = 1.0 / den
    mu = jnp.mean(num, axis=2, keepdims=True)
    cen = num - mu
    var = jnp.mean(cen * cen, axis=2, keepdims=True)
    h = cen * (rden * lax.rsqrt(rden * rden * var + LN_EPS)) * gain

    m_new = m_t[:, last:last + 1, :]
    w_k = jnp.exp(b_c[:, last:last + 1, :] - b_c + ig_c - m_new)
    w_c = jnp.exp(inter[:, last:last + 1, :] - m_new)
    kw = k.astype(F32) * w_k
    kwb = kw.astype(BF16)
    c_new = w_c * c0 + per_group(lambda g: lax.dot_general(kwb[g], v[g], tn, preferred_element_type=F32))
    n_new = w_c * n0 + jnp.sum(kw, axis=1, keepdims=True)
    return h, c_new, n_new, m_new


def _log_sigmoid(x):
    return jnp.minimum(x, 0.0) - jnp.log1p(jnp.exp(-jnp.abs(x)))


def _mlstm_prompt_step(c, qk_ref, v_ref, if_ref, bias_ref, g_ref, hn_ref, c_ref, n_ref, m_ref, n_heads, d_qk, d_v):
    @pl.when(c == 0)
    def _():
        c_ref[...] = jnp.zeros_like(c_ref)
        n_ref[...] = jnp.zeros_like(n_ref)
        m_ref[...] = jnp.zeros_like(m_ref)

    gates = if_ref[...] + bias_ref[...]
    lf_all = _log_sigmoid(gates)
    L = gates.shape[0]
    heads = range(n_heads)
    q = jnp.stack([qk_ref[:, h * d_qk:(h + 1) * d_qk] for h in heads])
    k = jnp.stack([qk_ref[:, (n_heads + h) * d_qk:(n_heads + h + 1) * d_qk] for h in heads])
    v = jnp.stack([v_ref[:, h * d_v:(h + 1) * d_v] for h in heads])
    ig_c = jnp.stack([gates[:, h:h + 1] for h in heads])
    lf_c = jnp.stack([lf_all[:, n_heads + h:n_heads + h + 1] for h in heads])
    g = jnp.stack([g_ref[:, h * d_v:(h + 1) * d_v] for h in heads])
    hn, c_new, n_new, m_new = _mlstm_chunks(q, k, v, ig_c, lf_c, c_ref[0], n_ref[0], m_ref[0], g, L - 1)
    c_ref[0] = c_new
    n_ref[0] = n_new
    m_ref[0] = m_new
    hn = hn.astype(BF16)
    for h in heads:
        hn_ref[:, h * d_v:(h + 1) * d_v] = hn[h]


def _mlstm_sample_step(qk_ref, v_ref, if_ref, bias_ref, g_ref, c0_ref, n0_ref, m0_ref, hn_ref, c_ref, n_ref, m_ref,
                       n_heads, d_qk, d_v, t_dec, bs):
    pad = (-t_dec) % (2 * SUBLANES)

    def zpad_rows(x):
        x = x.reshape(bs, t_dec, x.shape[1])
        return jnp.concatenate([x, jnp.zeros((bs, pad, x.shape[2]), x.dtype)], axis=1) if pad else x

    qk_all = zpad_rows(qk_ref[...].astype(F32))
    v_all = zpad_rows(v_ref[...].astype(F32))
    gates = zpad_rows(if_ref[...] + bias_ref[...])
    lf_all = _log_sigmoid(gates)
    groups = [(b, h) for b in range(bs) for h in range(n_heads)]
    q = jnp.stack([qk_all[b][:, h * d_qk:(h + 1) * d_qk] for b, h in groups]).astype(BF16)
    k = jnp.stack([qk_all[b][:, (n_heads + h) * d_qk:(n_heads + h + 1) * d_qk] for b, h in groups]).astype(BF16)
    v = jnp.stack([v_all[b][:, h * d_v:(h + 1) * d_v] for b, h in groups]).astype(BF16)
    ig_c = jnp.stack([gates[b][:, h:h + 1] for b, h in groups])
    lf_c = jnp.stack([lf_all[b][:, n_heads + h:n_heads + h + 1] for b, h in groups])
    flat = lambda r: r[...].reshape((bs * n_heads,) + r.shape[2:])
    g = jnp.stack([g_ref[:, h * d_v:(h + 1) * d_v] for _, h in groups])
    hn, c_new, n_new, m_new = _mlstm_chunks(q, k, v, ig_c, lf_c, flat(c0_ref), flat(n0_ref), flat(m0_ref), g,
                                            t_dec - 1)
    c_ref[...] = c_new.reshape(c_ref.shape)
    n_ref[...] = n_new.reshape(n_ref.shape)
    m_ref[...] = m_new.reshape(m_ref.shape)
    hn = hn[:, :t_dec].astype(BF16)
    for i, (b, h) in enumerate(groups):
        hn_ref[b * t_dec:(b + 1) * t_dec, h * d_v:(h + 1) * d_v] = hn[i]


def _mlstm_kernel(qkp_ref, vp_ref, ifp_ref, qks_ref, vs_ref, ifs_ref, bias_ref, g_ref, c0_ref, n0_ref, m0_ref,
                  hnp_ref, cp_ref, np_ref, mp_ref, hns_ref, cs_ref, ns_ref, ms_ref,
                  *, n_chunks, n_heads, d_qk, d_v, t_dec, bs):
    _mlstm_prompt_step(pl.program_id(0) % n_chunks, qkp_ref, vp_ref, ifp_ref, bias_ref, g_ref,
                       hnp_ref, cp_ref, np_ref, mp_ref, n_heads, d_qk, d_v)
    _mlstm_sample_step(qks_ref, vs_ref, ifs_ref, bias_ref, g_ref, c0_ref, n0_ref, m0_ref,
                       hns_ref, cs_ref, ns_ref, ms_ref, n_heads, d_qk, d_v, t_dec, bs)


def _mlstm(qk, v, iff, bias, g, c0, n0, m0, batch, seq, t_dec):
    dec_batch, n_heads, d_qk, d_v = c0.shape
    d = v.shape[1]
    L = min(CHUNK, seq)
    assert seq % L == 0
    nc = seq // L
    steps = batch * nc
    assert dec_batch % steps == 0
    bs = dec_batch // steps
    rows = bs * t_dec
    n_prompt = batch * seq
    assert rows % (2 * SUBLANES) == 0 and n_prompt % rows == 0
    blk0 = n_prompt // rows
    ptok = lambda w: pl.BlockSpec((L, w), lambda i: (i, 0))
    stok = lambda w: pl.BlockSpec((rows, w), lambda i: (blk0 + i, 0))
    pst = lambda *tail: pl.BlockSpec((1, n_heads) + tail, lambda i: (i // nc, 0, 0, 0))
    sst = lambda *tail: pl.BlockSpec((bs, n_heads) + tail, lambda i: (i, 0, 0, 0))
    state_shapes = lambda nb: (jax.ShapeDtypeStruct((nb, n_heads, d_qk, d_v), F32),
                               jax.ShapeDtypeStruct((nb, n_heads, 1, d_qk), F32),
                               jax.ShapeDtypeStruct((nb, n_heads, 1, 1), F32))
    return pl.pallas_call(
        functools.partial(_mlstm_kernel, n_chunks=nc, n_heads=n_heads, d_qk=d_qk, d_v=d_v, t_dec=t_dec, bs=bs),
        out_shape=(jax.ShapeDtypeStruct((n_prompt, d), BF16), *state_shapes(batch),
                   jax.ShapeDtypeStruct((dec_batch * t_dec, d), BF16), *state_shapes(dec_batch)),
        grid=(steps,),
        in_specs=[ptok(qk.shape[1]), ptok(d), ptok(LANES), stok(qk.shape[1]), stok(d), stok(LANES),
                  pl.BlockSpec((1, LANES), lambda i: (0, 0)), pl.BlockSpec((1, d), lambda i: (0, 0)),
                  sst(d_qk, d_v), sst(1, d_qk), sst(1, 1)],
        out_specs=(ptok(d), pst(d_qk, d_v), pst(1, d_qk), pst(1, 1),
                   pl.BlockSpec((rows, d), lambda i: (i, 0)), sst(d_qk, d_v), sst(1, d_qk), sst(1, 1)),
        compiler_params=_params(("arbitrary",)),
        name="mlstm",
    )(qk, v, iff, qk, v, iff, bias, g, c0, n0, m0)


def _mix_ln_kernel(yc_ref, ga_ref, gb_ref, hnp_ref, hns_ref, x1_ref, cb_ref, cg_ref, cbb_ref, w_ref, g_ref, b_ref,
                   o_ref, *, alpha, split):
    def body(hn_ref):
        z = _layer_norm(yc_ref[...] + cb_ref[...], cg_ref[...], cbb_ref[...])
        hb = (z * _sigmoid(z)).astype(BF16)
        mixin = ga_ref[...] * hn_ref[...] + gb_ref[...] * hb
        mix = _dot(mixin, w_ref[...])
        o_ref[...] = _layer_norm(alpha * x1_ref[...] + mix, g_ref[...], b_ref[...])

    _for_owner((hnp_ref, hns_ref), split, pl.program_id(0), body)


def _mix_ln(yc, ga, gb, hn_p, hn_s, x1, cb, cg, cbb, w, g, b, alpha):
    n, d = x1.shape
    tm = min(TM_MIX, hn_s.shape[0])
    assert hn_p.shape[0] % tm == 0 and hn_s.shape[0] % tm == 0 and hn_p.shape[0] + hn_s.shape[0] == n
    tok = pl.BlockSpec((tm, d), lambda i: (i, 0))
    vec = pl.BlockSpec((1, d), lambda i: (0, 0))
    hn_specs, split = _segment_specs([hn_p.shape[0], hn_s.shape[0]], tm, d)
    return pl.pallas_call(
        functools.partial(_mix_ln_kernel, alpha=alpha, split=split),
        out_shape=jax.ShapeDtypeStruct((n, d), F32),
        grid=(n // tm,),
        in_specs=[tok, tok, tok, *hn_specs, tok, vec, vec, vec, pl.BlockSpec((d, d), lambda i: (0, 0)), vec, vec],
        out_specs=tok,
        compiler_params=_params(("arbitrary",)),
        name="mix_ln",
    )(yc, ga, gb, hn_p, hn_s, x1, cb, cg, cbb, w, g, b)


def _layer(xs, layer, batch, seq, dec_batch, t_dec, c0, n0, m0, cache, w_in, p, alpha):
    (ffn1_w1, ffn1_w3, ffn1_w2, ln1_g, ln1_b, b_igate, b_fgate, mh_norm_g,
     conv_w, conv_b, conv_ln_g, conv_ln_b, w_out, ln2_g, ln2_b,
     ffn2_w1, ffn2_w3, ffn2_w2, ln3_g, ln3_b) = p
    d = xs[0].shape[1]
    n_heads, d_qk, d_v = c0.shape[1:]
    n_prompt, n_sample = batch * seq, dec_batch * t_dec
    hqk, dm = n_heads * d_qk, n_heads * d_v
    assert 2 * hqk == d and dm == d and conv_w.shape[1] == d
    vec = lambda a: a.reshape(1, -1).astype(F32)
    bf = lambda a: a.astype(BF16)

    (x1,) = _ffn_ln(xs, ffn1_w1, ffn1_w3, ffn1_w2, vec(ln1_g), vec(ln1_b), alpha, (n_prompt + n_sample,))
    w, wif = _prep_w_in(jnp.swapaxes(w_in, 1, 2), layer, 2 * hqk + 2 * dm, 2 * n_heads, 4 * d)
    qk, v, ga, gb, yc, iff, tile_tails, conv_s_t = _proj(x1, w, wif, conv_w.astype(F32), jnp.swapaxes(cache, 0, 1),
                                                         hqk, float(d_qk) ** -0.5, batch, seq, t_dec)

    gate_bias = jnp.pad(jnp.concatenate([b_igate, b_fgate]).astype(F32), (0, LANES - 2 * n_heads)).reshape(1, LANES)
    mh_g = vec(mh_norm_g)
    hn_p, c_p, n_p, m_p, hn_s, c_s, n_s, m_s = _mlstm(qk, v, iff, gate_bias, mh_g, c0,
                                                      n0.reshape(dec_batch, n_heads, 1, d_qk),
                                                      m0.reshape(dec_batch, n_heads, 1, 1), batch, seq, t_dec)

    x2 = _mix_ln(yc, ga, gb, hn_p, hn_s, x1, vec(conv_b), vec(conv_ln_g), vec(conv_ln_b), bf(w_out),
                 vec(ln2_g), vec(ln2_b), alpha)
    y_p, y_s = _ffn_ln((x2,), ffn2_w1, ffn2_w3, ffn2_w2, vec(ln3_g), vec(ln3_b), alpha, (n_prompt, n_sample))
    tiles_per_seq = tile_tails.shape[0] // batch
    conv_p = tile_tails[tiles_per_seq - 1::tiles_per_seq]
    states_p = (c_p, n_p.reshape(batch, n_heads, d_qk), m_p.reshape(batch, n_heads), conv_p)
    states_s = (c_s, n_s.reshape(dec_batch, n_heads, d_qk), m_s.reshape(dec_batch, n_heads),
                jnp.swapaxes(conv_s_t, 0, 1))
    return (y_p, y_s), states_p, states_s


def kernel(x_prompt, x_sample, state_C, state_n, state_m, cache_conv, ffn1_w1, ffn1_w3, ffn1_w2, ln1_g, ln1_b, w_in, b_igate, b_fgate, mh_norm_g, conv_w, conv_b, conv_ln_g, conv_ln_b, w_out, ln2_g, ln2_b, ffn2_w1, ffn2_w3, ffn2_w2, ln3_g, ln3_b):
    batch, seq, d = x_prompt.shape
    dec_batch, t_dec, _ = x_sample.shape
    depth = ffn1_w1.shape[0]
    alpha = (2.0 * depth) ** 0.25
    xs = (x_prompt.reshape(batch * seq, d), x_sample.reshape(dec_batch * t_dec, d))
    weights = (ffn1_w1, ffn1_w3, ffn1_w2, ln1_g, ln1_b, b_igate, b_fgate, mh_norm_g,
               conv_w, conv_b, conv_ln_g, conv_ln_b, w_out, ln2_g, ln2_b,
               ffn2_w1, ffn2_w3, ffn2_w2, ln3_g, ln3_b)
    outs_p, outs_s = [], []
    for l in range(depth):
        p = tuple(wt[l] for wt in weights)
        xs, st_p, st_s = _layer(xs, l, batch, seq, dec_batch, t_dec, state_C[l], state_n[l], state_m[l],
                                cache_conv[l], w_in, p, alpha)
        outs_p.append(st_p)
        outs_s.append(st_s)
    y_p = xs[0].reshape(batch, seq, d)
    y_s = xs[1].reshape(dec_batch, t_dec, d)
    stack = lambda outs, i: jnp.stack([o[i] for o in outs])
    return (y_p, y_s,
            stack(outs_p, 0), stack(outs_p, 1), stack(outs_p, 2), stack(outs_p, 3),
            stack(outs_s, 0), stack(outs_s, 1), stack(outs_s, 2), stack(outs_s, 3))
```

```python
import functools

import jax
import jax.numpy as jnp
from jax import lax
from jax.experimental import pallas as pl
from jax.experimental.pallas import tpu as pltpu

F32 = jnp.float32
BF16 = jnp.bfloat16
LN_EPS = 1e-5
LANES = 128
SUBLANES = 8
HALO = 32
VMEM_LIMIT = 60 * 1024 * 1024

TM_FFN = 512
TF_FFN = 512
TF_FFN_FIRST = 256
TILES_FFN_FIRST = 2
TM_PROJ = 512
TN_PROJ = 256
TW_PREP = 512
TR_PREP = 512
TM_MIX = 256
CONV_ROWS = 128
CHUNK = 256


def _params(sem):
    return pltpu.CompilerParams(dimension_semantics=sem, vmem_limit_bytes=VMEM_LIMIT)


def _sigmoid(x):
    return 0.5 * jnp.tanh(0.5 * x) + 0.5


def _layer_norm(z, g, b):
    mu = jnp.mean(z, axis=-1, keepdims=True)
    zc = z - mu
    var = jnp.mean(zc * zc, axis=-1, keepdims=True)
    return zc * lax.rsqrt(var + LN_EPS) * g + b


def _dot(a, b):
    return jnp.dot(a, b, preferred_element_type=F32)


def _for_owner(refs, split, i, fn):
    if len(refs) == 1:
        fn(refs[0])
        return
    pl.when(i < split)(lambda: fn(refs[0]))
    pl.when(i >= split)(lambda: fn(refs[1]))


def _segment_specs(arrays_rows, tm, d):
    if len(arrays_rows) == 1:
        return [pl.BlockSpec((tm, d), lambda i, *_: (i, 0))], 0
    split = arrays_rows[0] // tm
    last0 = split - 1
    return [pl.BlockSpec((tm, d), lambda i, *_: (jnp.minimum(i, last0), 0)),
            pl.BlockSpec((tm, d), lambda i, *_: (jnp.maximum(i - split, 0), 0))], split


def _ffn_ln_kernel(*refs, alpha, n_in, n_out, in_split, out_split, n_first, emit_w):
    x_refs = refs[:n_in]
    w1_ref, w3_ref, w2_ref, g_ref, b_ref = refs[n_in:n_in + 5]
    pos = n_in + 5
    first_ref = refs[pos] if n_first else None
    pos += bool(n_first)
    o_refs = refs[pos:pos + n_out]
    pos += n_out
    wb_refs = refs[pos:pos + 3] if emit_w else ()
    scratch = refs[pos + len(wb_refs):]
    xb_ref, acc_ref = scratch if len(scratch) == 2 else (scratch[0], o_refs[0])
    i = pl.program_id(0)
    f = pl.program_id(1)
    last_f = pl.num_programs(1) - 1

    def tile():
        @pl.when(f == 0)
        def _():
            def start(x_ref):
                x = x_ref[...]
                xb_ref[...] = x.astype(BF16)
                acc_ref[...] = (2.0 * alpha) * x
            _for_owner(x_refs, in_split, i, start)

        w1, w3, w2 = (r[...].astype(BF16) for r in (w1_ref, w3_ref, w2_ref))
        for wb_ref, w in zip(wb_refs, (w1, w3, w2)):
            wb_ref[...] = w
        xb = xb_ref[...]
        a = _dot(xb, w1)
        c = _dot(xb, w3)
        h = (a * _sigmoid(a) * c).astype(BF16)
        acc_ref[...] += _dot(h, w2)

        @pl.when(f == last_f)
        def _():
            acc_ref[...] = _layer_norm(0.5 * acc_ref[...], g_ref[...], b_ref[...])

            def emit(o_ref):
                o_ref[...] = acc_ref[...]
            if len(scratch) == 2:
                _for_owner(o_refs, out_split, i, emit)

    if not n_first:
        tile()
        return
    pl.when(i >= n_first)(tile)

    @pl.when(jnp.logical_and(i < n_first, f == last_f))
    def _():
        o_refs[0][...] = first_ref[...]


def _ffn_ln(xs, w1, w3, w2, g, b, alpha, out_rows):
    d = xs[0].shape[1]
    n = sum(x.shape[0] for x in xs)
    dff = w1.shape[1]
    tm, tf, tf0 = min(TM_FFN, n), min(TF_FFN, dff), min(TF_FFN_FIRST, dff)
    n_first = TILES_FFN_FIRST
    tm0 = n_first * tm
    assert sum(out_rows) == n and dff % tf == 0 and dff % tf0 == 0 and xs[0].shape[0] >= tm0 <= out_rows[0]
    assert all(x.shape[0] % tm == 0 for x in xs) and all(r % tm == 0 for r in out_rows)
    vec = pl.BlockSpec((1, d), lambda i, f: (0, 0))
    once = pl.BlockSpec((tm0, d), lambda i, f: (0, 0), pipeline_mode=pl.Buffered(1))
    scratch = lambda t, own_acc: [pltpu.VMEM((t, d), BF16)] + ([pltpu.VMEM((t, d), F32)] if own_acc else [])
    kern = functools.partial(_ffn_ln_kernel, alpha=alpha)

    w_specs = [pl.BlockSpec((d, tf0), lambda i, f: (0, f)), pl.BlockSpec((d, tf0), lambda i, f: (0, f)),
               pl.BlockSpec((tf0, d), lambda i, f: (f, 0))]
    first, w1b, w3b, w2b = pl.pallas_call(
        functools.partial(kern, n_in=1, n_out=1, in_split=0, out_split=0, n_first=0, emit_w=True),
        out_shape=(jax.ShapeDtypeStruct((tm0, d), F32),
                   *(jax.ShapeDtypeStruct(w.shape, BF16) for w in (w1, w3, w2))),
        grid=(1, dff // tf0),
        in_specs=[once] + w_specs + [vec, vec],
        out_specs=(once, *w_specs),
        scratch_shapes=scratch(tm0, False),
        compiler_params=_params(("arbitrary", "arbitrary")),
        name="ffn_ln_first",
    )(xs[0], w1, w3, w2, g, b)

    wf = lambda i, f: jnp.where(i < n_first, 0, f)
    w_specs = [pl.BlockSpec((d, tf), lambda i, f: (0, wf(i, f))), pl.BlockSpec((d, tf), lambda i, f: (0, wf(i, f))),
               pl.BlockSpec((tf, d), lambda i, f: (wf(i, f), 0))]
    in_specs, in_split = _segment_specs([x.shape[0] for x in xs], tm, d)
    out_specs, out_split = _segment_specs(list(out_rows), tm, d)
    return pl.pallas_call(
        functools.partial(kern, n_in=len(xs), n_out=len(out_rows), in_split=in_split, out_split=out_split,
                          n_first=n_first, emit_w=False),
        out_shape=tuple(jax.ShapeDtypeStruct((r, d), F32) for r in out_rows),
        grid=(n // tm, dff // tf),
        in_specs=in_specs + w_specs + [vec, vec, pl.BlockSpec((tm, d), lambda i, f: (jnp.minimum(i, n_first - 1), 0),
                                                              pipeline_mode=pl.Buffered(1))],
        out_specs=tuple(out_specs),
        scratch_shapes=scratch(tm, len(out_rows) > 1),
        compiler_params=_params(("arbitrary", "arbitrary")),
        name="ffn_ln",
    )(*xs, w1b, w3b, w2b, g, b, first)


def _prep_w_in_kernel(a_ref, nb_ref, if_ref, w_ref, wif_ref, *, n_aligned, n_gate):
    j = pl.program_id(0)
    tw, k_rows = a_ref.shape
    tr = min(TR_PREP, k_rows)

    def emit(rows_of):
        for c0 in range(0, k_rows, tr):
            w_ref[c0:c0 + tr, :] = rows_of(c0).T.astype(BF16)

    @pl.when(j < n_aligned)
    def _():
        emit(lambda c0: a_ref[:, c0:c0 + tr])

    @pl.when(j >= n_aligned)
    def _():
        emit(lambda c0: jnp.concatenate([a_ref[n_gate:tw, c0:c0 + tr], nb_ref[:, c0:c0 + tr]], axis=0))

    @pl.when(j == 0)
    def _():
        for c0 in range(0, k_rows, tr):
            rows = jnp.concatenate([if_ref[:, c0:c0 + tr], jnp.zeros((LANES - n_gate, tr), F32)], axis=0)
            wif_ref[c0:c0 + tr, :] = rows.T.astype(BF16)


def _prep_w_in(w_in_t, layer, n_aligned_cols, n_gate, n_shifted_cols):
    _, n_in, k_rows = w_in_t.shape
    tw = TW_PREP
    assert n_aligned_cols % tw == 0 and n_shifted_cols % tw == 0 and n_gate == SUBLANES
    assert k_rows % min(TR_PREP, k_rows) == 0 and n_aligned_cols + n_gate + n_shifted_cols == n_in
    n_aligned = n_aligned_cols // tw
    n_tiles = n_aligned + n_shifted_cols // tw
    per = tw // n_gate
    return pl.pallas_call(
        functools.partial(_prep_w_in_kernel, n_aligned=n_aligned, n_gate=n_gate),
        out_shape=(jax.ShapeDtypeStruct((k_rows, n_tiles * tw), BF16),
                   jax.ShapeDtypeStruct((k_rows, LANES), BF16)),
        grid=(n_tiles,),
        in_specs=[
            pl.BlockSpec((None, tw, k_rows), lambda j: (layer, j, 0)),
            pl.BlockSpec((None, n_gate, k_rows), lambda j: (layer, (j + 1) * per, 0)),
            pl.BlockSpec((None, n_gate, k_rows), lambda j: (layer, n_aligned * per, 0)),
        ],
        out_specs=(pl.BlockSpec((k_rows, tw), lambda j: (0, j)),
                   pl.BlockSpec((k_rows, LANES), lambda j: (0, 0))),
        compiler_params=_params(("arbitrary",)),
        name="prep_w_in",
    )(w_in_t, w_in_t, w_in_t)


def _conv_tile(ubuf_ref, w_ref, y_ref, u0, out0, rows, width, cs):
    y = None
    for r in range(min(SUBLANES, width)):
        q = None
        for a in range((width - 1 - r) // SUBLANES + 1):
            k = width - 1 - (SUBLANES * a + r)
            term = w_ref[k:k + 1, cs] * ubuf_ref[pl.ds(u0 - SUBLANES * (a + 1), rows + SUBLANES), cs]
            q = term if q is None else q + term
        part = q[SUBLANES - r:SUBLANES - r + rows]
        y = part if y is None else y + part
    y_ref[pl.ds(out0, rows), cs] = y


def _proj_kernel(x_ref, wqk_ref, wv_ref, wo_ref, wla_ref, wlb_ref, wga_ref, wgb_ref, wif_ref, cw_ref, cache_ref,
                 qk_ref, v_ref, ga_ref, gb_ref, yc_ref, if_ref, csp_ref, css_ref,
                 xb_ref, ubuf_ref, carry_ref, tbuf_ref, ybuf_ref, *, k_col0, k_scale, n_prompt_tiles, tiles_per_seq, width, t_dec):
    i = pl.program_id(0)
    j = pl.program_id(1)
    tm, tn = yc_ref.shape
    hist = width - 1

    @pl.when(j == 0)
    def _():
        xb0 = x_ref[...].astype(BF16)
        xb_ref[...] = xb0
        if_ref[...] = _dot(xb0, wif_ref[...])

    def glu(xb):
        return _dot(xb, wla_ref[...]) * _sigmoid(_dot(xb, wlb_ref[...]))

    def dense(xb):
        qk_ref[...] = (_dot(xb, wqk_ref[...]) * jnp.where(j * tn >= k_col0, k_scale, 1.0)).astype(BF16)
        v_ref[...] = _dot(xb, wv_ref[...]).astype(BF16)
        ga_ref[...] = (_sigmoid(_dot(xb, wo_ref[...])) * _sigmoid(_dot(xb, wga_ref[...]))).astype(BF16)
        gb_ref[...] = _sigmoid(_dot(xb, wgb_ref[...])).astype(BF16)

    @pl.when(i < n_prompt_tiles)
    def _():
        xb = xb_ref[...]
        u = glu(xb)
        first = (i % tiles_per_seq) == 0
        ubuf_ref[0:HALO, :] = jnp.where(first, 0.0, carry_ref[j])
        ubuf_ref[HALO:HALO + tm, :] = u
        carry_ref[j] = u[tm - HALO:tm]
        csp_ref[...] = u[tm - hist:tm]
        strip = min(CONV_ROWS, tm)
        for c0 in range(0, tn, LANES):
            for r0 in range(0, tm, strip):
                _conv_tile(ubuf_ref, cw_ref, yc_ref, HALO + r0, r0, strip, width, slice(c0, c0 + LANES))
        dense(xb)

    @pl.when(i >= n_prompt_tiles)
    def _():
        xb = xb_ref[...]
        nb = tm // t_dec
        u = glu(xb)
        for c in range(tn // LANES):
            cs = slice(c * LANES, (c + 1) * LANES)
            tbuf_ref[c] = u[:, cs]
            new = [tbuf_ref[c, pl.ds(t, nb, stride=t_dec), :] for t in range(t_dec)]
            row = lambda p: cache_ref[p, :, cs] if p < hist else new[p - hist]
            for t in range(t_dec):
                acc = cw_ref[0:1, cs] * row(t)
                for k in range(1, width):
                    acc = acc + cw_ref[k:k + 1, cs] * row(t + k)
                ybuf_ref[c, pl.ds(t, nb, stride=t_dec), :] = acc
            yc_ref[:, cs] = ybuf_ref[c]
            for p in range(hist):
                css_ref[p, :, cs] = row(p + t_dec)
        dense(xb)


def _proj(x1, w, wif, cw, cache_t, hqk, k_scale, batch, seq, t_dec):
    n, d = x1.shape
    hist, dec_batch, _ = cache_t.shape
    width = cw.shape[0]
    tm, tn = min(TM_PROJ, seq), min(TN_PROJ, d)
    n_prompt = batch * seq
    assert n_prompt % tm == 0 and (n - n_prompt) % tm == 0 and seq % tm == 0 and d % tn == 0
    assert hqk % tn == 0 and 2 * hqk == d and tn % LANES == 0
    assert hist == width - 1 and SUBLANES * (hist // SUBLANES + 1) <= HALO <= tm and tm % CONV_ROWS == 0
    assert n - n_prompt == dec_batch * t_dec and tm % t_dec == 0 and (tm // t_dec) % SUBLANES == 0 and t_dec <= hist
    npt, tps, nb = n_prompt // tm, seq // tm, tm // t_dec
    per = d // tn
    col = pl.BlockSpec((tm, tn), lambda i, j: (i, j))
    wspec = lambda g: pl.BlockSpec((d, tn), lambda i, j: (0, g * per + j))
    smp_i = lambda i: jnp.maximum(i - npt, 0)
    smp_j = lambda i, j: jnp.where(i >= npt, j, 0)
    prm_i = lambda i: jnp.minimum(i, npt - 1)
    prm_j = lambda i, j: jnp.where(i < npt, j, per - 1)
    return pl.pallas_call(
        functools.partial(_proj_kernel, k_col0=hqk, k_scale=k_scale, n_prompt_tiles=npt, tiles_per_seq=tps,
                          width=width, t_dec=t_dec),
        out_shape=(
            jax.ShapeDtypeStruct((n, d), BF16),
            jax.ShapeDtypeStruct((n, d), BF16),
            jax.ShapeDtypeStruct((n, d), BF16),
            jax.ShapeDtypeStruct((n, d), BF16),
            jax.ShapeDtypeStruct((n, d), F32),
            jax.ShapeDtypeStruct((n, LANES), F32),
            jax.ShapeDtypeStruct((npt, hist, d), F32),
            jax.ShapeDtypeStruct((hist, dec_batch, d), F32),
        ),
        grid=(n // tm, d // tn),
        in_specs=[pl.BlockSpec((tm, d), lambda i, j: (i, 0)),
                  wspec(0), wspec(1), wspec(2), wspec(3), wspec(4), wspec(5), wspec(6),
                  pl.BlockSpec((d, LANES), lambda i, j: (0, 0)),
                  pl.BlockSpec((width, tn), lambda i, j: (0, j)),
                  pl.BlockSpec((hist, nb, tn), lambda i, j: (0, smp_i(i), smp_j(i, j)))],
        out_specs=(col, col, col, col, col, pl.BlockSpec((tm, LANES), lambda i, j: (i, 0)),
                   pl.BlockSpec((None, hist, tn), lambda i, j: (prm_i(i), 0, prm_j(i, j))),
                   pl.BlockSpec((hist, nb, tn), lambda i, j: (0, smp_i(i), smp_j(i, j)))),
        scratch_shapes=[pltpu.VMEM((tm, d), BF16), pltpu.VMEM((HALO + tm, tn), F32),
                        pltpu.VMEM((per, HALO, tn), F32),
                        pltpu.VMEM((tn // LANES, tm, LANES), F32), pltpu.VMEM((tn // LANES, tm, LANES), F32)],
        compiler_params=_params(("arbitrary", "arbitrary")),
        name="proj",
    )(x1, w, w, w, w, w, w, w, wif, cw, cache_t)


def _mlstm_chunks(q, k, v, ig_c, lf_c, c0, n0, m0, gain, last):
    G, L, _ = q.shape
    row = lax.broadcasted_iota(jnp.int32, (L, L), 0)
    colm = lax.broadcasted_iota(jnp.int32, (L, L), 1)
    causal = (colm <= row)[None]
    eye = (colm == row)[None]
    upper = (row <= colm)[None]
    b_r = jnp.sum(jnp.where(upper, lf_c, 0.0), axis=1, keepdims=True)
    b_c = jnp.sum(jnp.where(eye, b_r, 0.0), axis=2, keepdims=True)
    ig_r = jnp.sum(jnp.where(eye, ig_c, 0.0), axis=1, keepdims=True)

    log_d = jnp.where(causal, b_c - b_r + ig_r, -jnp.inf)
    inter = b_c + m0
    m_t = jnp.maximum(inter, jnp.max(log_d, axis=2, keepdims=True))
    d = jnp.exp(log_d - m_t)
    w_inter = jnp.exp(inter - m_t)
    per_group = lambda fn: jnp.stack([fn(g) for g in range(G)])
    nt = (((1,), (1,)), ((), ()))
    tn = (((0,), (0,)), ((), ()))
    s = per_group(lambda g: lax.dot_general(q[g], k[g], nt, preferred_element_type=F32)) * d
    sb = s.astype(BF16)
    cb = c0.astype(BF16)
    num = per_group(lambda g: _dot(sb[g], v[g])) + w_inter * per_group(lambda g: _dot(q[g], cb[g]))
    qn = jnp.sum(s, axis=2, keepdims=True) + w_inter * jnp.sum(q.astype(F32) * n0, axis=2, keepdims=True)
    den = jnp.maximum(jnp.abs(qn), jnp.exp(-m_t))
    rden = 1.0 / den
    mu = jnp.mean(num, axis=2, keepdims=True)
    cen = num - mu
    var = jnp.mean(cen * cen, axis=2, keepdims=True)
    h = cen * (rden * lax.rsqrt(rden * rden * var + LN_EPS)) * gain

    m_new = m_t[:, last:last + 1, :]
    w_k = jnp.exp(b_c[:, last:last + 1, :] - b_c + ig_c - m_new)
    w_c = jnp.exp(inter[:, last:last + 1, :] - m_new)
    kw = k.astype(F32) * w_k
    kwb = kw.astype(BF16)
    c_new = w_c * c0 + per_group(lambda g: lax.dot_general(kwb[g], v[g], tn, preferred_element_type=F32))
    n_new = w_c * n0 + jnp.sum(kw, axis=1, keepdims=True)
    return h, c_new, n_new, m_new


def _log_sigmoid(x):
    return jnp.minimum(x, 0.0) - jnp.log1p(jnp.exp(-jnp.abs(x)))


def _mlstm_prompt_step(c, qk_ref, v_ref, if_ref, bias_ref, g_ref, hn_ref, c_ref, n_ref, m_ref, n_heads, d_qk, d_v):
    @pl.when(c == 0)
    def _():
        c_ref[...] = jnp.zeros_like(c_ref)
        n_ref[...] = jnp.zeros_like(n_ref)
        m_ref[...] = jnp.zeros_like(m_ref)

    gates = if_ref[...] + bias_ref[...]
    lf_all = _log_sigmoid(gates)
    L = gates.shape[0]
    heads = range(n_heads)
    q = jnp.stack([qk_ref[:, h * d_qk:(h + 1) * d_qk] for h in heads])
    k = jnp.stack([qk_ref[:, (n_heads + h) * d_qk:(n_heads + h + 1) * d_qk] for h in heads])
    v = jnp.stack([v_ref[:, h * d_v:(h + 1) * d_v] for h in heads])
    ig_c = jnp.stack([gates[:, h:h + 1] for h in heads])
    lf_c = jnp.stack([lf_all[:, n_heads + h:n_heads + h + 1] for h in heads])
    g = jnp.stack([g_ref[:, h * d_v:(h + 1) * d_v] for h in heads])
    hn, c_new, n_new, m_new = _mlstm_chunks(q, k, v, ig_c, lf_c, c_ref[0], n_ref[0], m_ref[0], g, L - 1)
    c_ref[0] = c_new
    n_ref[0] = n_new
    m_ref[0] = m_new
    hn = hn.astype(BF16)
    for h in heads:
        hn_ref[:, h * d_v:(h + 1) * d_v] = hn[h]


def _mlstm_sample_step(qk_ref, v_ref, if_ref, bias_ref, g_ref, c0_ref, n0_ref, m0_ref, hn_ref, c_ref, n_ref, m_ref,
                       n_heads, d_qk, d_v, t_dec, bs):
    pad = (-t_dec) % (2 * SUBLANES)

    def zpad_rows(x):
        x = x.reshape(bs, t_dec, x.shape[1])
        return jnp.concatenate([x, jnp.zeros((bs, pad, x.shape[2]), x.dtype)], axis=1) if pad else x

    qk_all = zpad_rows(qk_ref[...].astype(F32))
    v_all = zpad_rows(v_ref[...].astype(F32))
    gates = zpad_rows(if_ref[...] + bias_ref[...])
    lf_all = _log_sigmoid(gates)
    groups = [(b, h) for b in range(bs) for h in range(n_heads)]
    q = jnp.stack([qk_all[b][:, h * d_qk:(h + 1) * d_qk] for b, h in groups]).astype(BF16)
    k = jnp.stack([qk_all[b][:, (n_heads + h) * d_qk:(n_heads + h + 1) * d_qk] for b, h in groups]).astype(BF16)
    v = jnp.stack([v_all[b][:, h * d_v:(h + 1) * d_v] for b, h in groups]).astype(BF16)
    ig_c = jnp.stack([gates[b][:, h:h + 1] for b, h in groups])
    lf_c = jnp.stack([lf_all[b][:, n_heads + h:n_heads + h + 1] for b, h in groups])
    flat = lambda r: r[...].reshape((bs * n_heads,) + r.shape[2:])
    g = jnp.stack([g_ref[:, h * d_v:(h + 1) * d_v] for _, h in groups])
    hn, c_new, n_new, m_new = _mlstm_chunks(q, k, v, ig_c, lf_c, flat(c0_ref), flat(n0_ref), flat(m0_ref), g,
                                            t_dec - 1)
    c_ref[...] = c_new.reshape(c_ref.shape)
    n_ref[...] = n_new.reshape(n_ref.shape)
    m_ref[...] = m_new.reshape(m_ref.shape)
    hn = hn[:, :t_dec].astype(BF16)
    for i, (b, h) in enumerate(groups):
        hn_ref[b * t_dec:(b + 1) * t_dec, h * d_v:(h + 1) * d_v] = hn[i]


def _mlstm_kernel(qkp_ref, vp_ref, ifp_ref, qks_ref, vs_ref, ifs_ref, bias_ref, g_ref, c0_ref, n0_ref, m0_ref,
                  hnp_ref, cp_ref, np_ref, mp_ref, hns_ref, cs_ref, ns_ref, ms_ref,
                  *, n_chunks, n_heads, d_qk, d_v, t_dec, bs):
    _mlstm_prompt_step(pl.program_id(0) % n_chunks, qkp_ref, vp_ref, ifp_ref, bias_ref, g_ref,
                       hnp_ref, cp_ref, np_ref, mp_ref, n_heads, d_qk, d_v)
    _mlstm_sample_step(qks_ref, vs_ref, ifs_ref, bias_ref, g_ref, c0_ref, n0_ref, m0_ref,
                       hns_ref, cs_ref, ns_ref, ms_ref, n_heads, d_qk, d_v, t_dec, bs)


def _mlstm(qk, v, iff, bias, g, c0, n0, m0, batch, seq, t_dec):
    dec_batch, n_heads, d_qk, d_v = c0.shape
    d = v.shape[1]
    L = min(CHUNK, seq)
    assert seq % L == 0
    nc = seq // L
    steps = batch * nc
    assert dec_batch % steps == 0
    bs = dec_batch // steps
    rows = bs * t_dec
    n_prompt = batch * seq
    assert rows % (2 * SUBLANES) == 0 and n_prompt % rows == 0
    blk0 = n_prompt // rows
    ptok = lambda w: pl.BlockSpec((L, w), lambda i: (i, 0))
    stok = lambda w: pl.BlockSpec((rows, w), lambda i: (blk0 + i, 0))
    pst = lambda *tail: pl.BlockSpec((1, n_heads) + tail, lambda i: (i // nc, 0, 0, 0))
    sst = lambda *tail: pl.BlockSpec((bs, n_heads) + tail, lambda i: (i, 0, 0, 0))
    state_shapes = lambda nb: (jax.ShapeDtypeStruct((nb, n_heads, d_qk, d_v), F32),
                               jax.ShapeDtypeStruct((nb, n_heads, 1, d_qk), F32),
                               jax.ShapeDtypeStruct((nb, n_heads, 1, 1), F32))
    return pl.pallas_call(
        functools.partial(_mlstm_kernel, n_chunks=nc, n_heads=n_heads, d_qk=d_qk, d_v=d_v, t_dec=t_dec, bs=bs),
        out_shape=(jax.ShapeDtypeStruct((n_prompt, d), BF16), *state_shapes(batch),
                   jax.ShapeDtypeStruct((dec_batch * t_dec, d), BF16), *state_shapes(dec_batch)),
        grid=(steps,),
        in_specs=[ptok(qk.shape[1]), ptok(d), ptok(LANES), stok(qk.shape[1]), stok(d), stok(LANES),
                  pl.BlockSpec((1, LANES), lambda i: (0, 0)), pl.BlockSpec((1, d), lambda i: (0, 0)),
                  sst(d_qk, d_v), sst(1, d_qk), sst(1, 1)],
        out_specs=(ptok(d), pst(d_qk, d_v), pst(1, d_qk), pst(1, 1),
                   pl.BlockSpec((rows, d), lambda i: (i, 0)), sst(d_qk, d_v), sst(1, d_qk), sst(1, 1)),
        compiler_params=_params(("arbitrary",)),
        name="mlstm",
    )(qk, v, iff, qk, v, iff, bias, g, c0, n0, m0)


def _mix_ln_kernel(yc_ref, ga_ref, gb_ref, hnp_ref, hns_ref, x1_ref, cb_ref, cg_ref, cbb_ref, w_ref, g_ref, b_ref,
                   o_ref, *, alpha, split):
    def body(hn_ref):
        z = _layer_norm(yc_ref[...] + cb_ref[...], cg_ref[...], cbb_ref[...])
        hb = (z * _sigmoid(z)).astype(BF16)
        mixin = ga_ref[...] * hn_ref[...] + gb_ref[...] * hb
        mix = _dot(mixin, w_ref[...])
        o_ref[...] = _layer_norm(alpha * x1_ref[...] + mix, g_ref[...], b_ref[...])

    _for_owner((hnp_ref, hns_ref), split, pl.program_id(0), body)


def _mix_ln(yc, ga, gb, hn_p, hn_s, x1, cb, cg, cbb, w, g, b, alpha):
    n, d = x1.shape
    tm = min(TM_MIX, hn_s.shape[0])
    assert hn_p.shape[0] % tm == 0 and hn_s.shape[0] % tm == 0 and hn_p.shape[0] + hn_s.shape[0] == n
    tok = pl.BlockSpec((tm, d), lambda i: (i, 0))
    vec = pl.BlockSpec((1, d), lambda i: (0, 0))
    hn_specs, split = _segment_specs([hn_p.shape[0], hn_s.shape[0]], tm, d)
    return pl.pallas_call(
        functools.partial(_mix_ln_kernel, alpha=alpha, split=split),
        out_shape=jax.ShapeDtypeStruct((n, d), F32),
        grid=(n // tm,),
        in_specs=[tok, tok, tok, *hn_specs, tok, vec, vec, vec, pl.BlockSpec((d, d), lambda i: (0, 0)), vec, vec],
        out_specs=tok,
        compiler_params=_params(("arbitrary",)),
        name="mix_ln",
    )(yc, ga, gb, hn_p, hn_s, x1, cb, cg, cbb, w, g, b)


def _layer(xs, layer, batch, seq, dec_batch, t_dec, c0, n0, m0, cache, w_in, p, alpha):
    (ffn1_w1, ffn1_w3, ffn1_w2, ln1_g, ln1_b, b_igate, b_fgate, mh_norm_g,
     conv_w, conv_b, conv_ln_g, conv_ln_b, w_out, ln2_g, ln2_b,
     ffn2_w1, ffn2_w3, ffn2_w2, ln3_g, ln3_b) = p
    d = xs[0].shape[1]
    n_heads, d_qk, d_v = c0.shape[1:]
    n_prompt, n_sample = batch * seq, dec_batch * t_dec
    hqk, dm = n_heads * d_qk, n_heads * d_v
    assert 2 * hqk == d and dm == d and conv_w.shape[1] == d
    vec = lambda a: a.reshape(1, -1).astype(F32)
    bf = lambda a: a.astype(BF16)

    (x1,) = _ffn_ln(xs, ffn1_w1, ffn1_w3, ffn1_w2, vec(ln1_g), vec(ln1_b), alpha, (n_prompt + n_sample,))
    w, wif = _prep_w_in(jnp.swapaxes(w_in, 1, 2), layer, 2 * hqk + 2 * dm, 2 * n_heads, 4 * d)
    qk, v, ga, gb, yc, iff, tile_tails, conv_s_t = _proj(x1, w, wif, conv_w.astype(F32), jnp.swapaxes(cache, 0, 1),
                                                         hqk, float(d_qk) ** -0.5, batch, seq, t_dec)

    gate_bias = jnp.pad(jnp.concatenate([b_igate, b_fgate]).astype(F32), (0, LANES - 2 * n_heads)).reshape(1, LANES)
    mh_g = vec(mh_norm_g)
    hn_p, c_p, n_p, m_p, hn_s, c_s, n_s, m_s = _mlstm(qk, v, iff, gate_bias, mh_g, c0,
                                                      n0.reshape(dec_batch, n_heads, 1, d_qk),
                                                      m0.reshape(dec_batch, n_heads, 1, 1), batch, seq, t_dec)

    x2 = _mix_ln(yc, ga, gb, hn_p, hn_s, x1, vec(conv_b), vec(conv_ln_g), vec(conv_ln_b), bf(w_out),
                 vec(ln2_g), vec(ln2_b), alpha)
    y_p, y_s = _ffn_ln((x2,), ffn2_w1, ffn2_w3, ffn2_w2, vec(ln3_g), vec(ln3_b), alpha, (n_prompt, n_sample))
    tiles_per_seq = tile_tails.shape[0] // batch
    conv_p = tile_tails[tiles_per_seq - 1::tiles_per_seq]
    states_p = (c_p, n_p.reshape(batch, n_heads, d_qk), m_p.reshape(batch, n_heads), conv_p)
    states_s = (c_s, n_s.reshape(dec_batch, n_heads, d_qk), m_s.reshape(dec_batch, n_heads),
                jnp.swapaxes(conv_s_t, 0, 1))
    return (y_p, y_s), states_p, states_s


def kernel(x_prompt, x_sample, state_C, state_n, state_m, cache_conv, ffn1_w1, ffn1_w3, ffn1_w2, ln1_g, ln1_b, w_in, b_igate, b_fgate, mh_norm_g, conv_w, conv_b, conv_ln_g, conv_ln_b, w_out, ln2_g, ln2_b, ffn2_w1, ffn2_w3, ffn2_w2, ln3_g, ln3_b):
    batch, seq, d = x_prompt.shape
    dec_batch, t_dec, _ = x_sample.shape
    depth = ffn1_w1.shape[0]
    alpha = (2.0 * depth) ** 0.25
    xs = (x_prompt.reshape(batch * seq, d), x_sample.reshape(dec_batch * t_dec, d))
    weights = (ffn1_w1, ffn1_w3, ffn1_w2, ln1_g, ln1_b, b_igate, b_fgate, mh_norm_g,
               conv_w, conv_b, conv_ln_g, conv_ln_b, w_out, ln2_g, ln2_b,
               ffn2_w1, ffn2_w3, ffn2_w2, ln3_g, ln3_b)
    outs_p, outs_s = [], []
    for l in range(depth):
        p = tuple(wt[l] for wt in weights)
        xs, st_p, st_s = _layer(xs, l, batch, seq, dec_batch, t_dec, state_C[l], state_n[l], state_m[l],
                                cache_conv[l], w_in, p, alpha)
        outs_p.append(st_p)
        outs_s.append(st_s)
    y_p = xs[0].reshape(batch, seq, d)
    y_s = xs[1].reshape(dec_batch, t_dec, d)
    stack = lambda outs, i: jnp.stack([o[i] for o in outs])
    return (y_p, y_s,
            stack(outs_p, 0), stack(outs_p, 1), stack(outs_p, 2), stack(outs_p, 3),
            stack(outs_s, 0), stack(outs_s, 1), stack(outs_s, 2), stack(outs_s, 3))
```
